```python
import jax, jax.numpy as jnp
from jax import lax
import numpy as np

D_MODEL = 4096
BATCH = 2
SEQ = 4096
DEPTH = 2

N_MIXERS = 2
N_HEADS = 32
HEAD_DIM = D_MODEL // N_HEADS
BRANCH = N_HEADS * HEAD_DIM
N_KV_HEADS = 4
GROUP = N_HEADS // N_KV_HEADS
Q_LORA = 1024
IDX_HEADS = 64
IDX_DIM = 128
TOPK_MAX = 256
FOX_HEADS = 32
FOX_HEAD_DIM = BRANCH // FOX_HEADS
BLOCK = 128
ROPE_THETA = 10000.0
LN_EPS = 1e-5
RMS_EPS = 1e-6
ALPHA = (2 * DEPTH) ** 0.25
BETA = (8 * DEPTH) ** -0.25
N_A = (DEPTH + 1) // 2
N_B = DEPTH // 2
A_SIZES = (Q_LORA, N_KV_HEADS * HEAD_DIM, N_KV_HEADS * HEAD_DIM, IDX_DIM, IDX_HEADS, BRANCH)
B_SIZES = (BRANCH, BRANCH, BRANCH, BRANCH, FOX_HEADS)
A_IN = sum(A_SIZES)
B_IN = sum(B_SIZES)

kernel_name = 'hybrid_dsa_fox_deepnorm'


def _offsets(sizes):
    out, acc = [], 0
    for s in sizes[:-1]:
        acc += s
        out.append(acc)
    return out


def _layernorm(x, g, b):
    xf = x.astype(jnp.float32)
    mu = jnp.mean(xf, axis=-1, keepdims=True)
    var = jnp.mean(jnp.square(xf - mu), axis=-1, keepdims=True)
    y = (xf - mu) * lax.rsqrt(var + LN_EPS) * g.astype(jnp.float32) + b.astype(jnp.float32)
    return y.astype(x.dtype)


def _rmsnorm(x, g):
    xf = x.astype(jnp.float32)
    y = xf * lax.rsqrt(jnp.mean(jnp.square(xf), axis=-1, keepdims=True) + RMS_EPS)
    return (y * g.astype(jnp.float32)).astype(x.dtype)


def _rope(x, pos):
    half = x.shape[-1] // 2
    inv = ROPE_THETA ** (-jnp.arange(half, dtype=jnp.float32) / half)
    ang = pos.astype(jnp.float32)[:, None] * inv[None, :]
    cos = jnp.cos(ang)[None, :, None, :]
    sin = jnp.sin(ang)[None, :, None, :]
    xf = x.astype(jnp.float32)
    x1, x2 = xf[..., :half], xf[..., half:]
    return jnp.concatenate([x1 * cos - x2 * sin, x2 * cos + x1 * sin], axis=-1).astype(x.dtype)


def _to_blocks(a):
    b, s = a.shape[:2]
    return a.reshape(b, s // BLOCK, BLOCK, *a.shape[2:]).swapaxes(0, 1)


def _from_blocks(a):
    nb, b = a.shape[:2]
    return a.swapaxes(0, 1).reshape(b, nb * BLOCK, *a.shape[3:])


def _dsa_mixer(x, w_in, q_norm_g, w_uq, kidx_g, kidx_b, w_out):
    b, s, _ = x.shape
    pos = jnp.arange(s)
    top = min(TOPK_MAX, s // 4)
    c_q, k, v, k_idx, w_idx, gate = jnp.split(x @ w_in, _offsets(A_SIZES), axis=-1)
    q_up = _rmsnorm(c_q, q_norm_g) @ w_uq
    q, q_idx = jnp.split(q_up, [BRANCH], axis=-1)
    q = _rope(q.reshape(b, s, N_HEADS, HEAD_DIM), pos)
    k = _rope(k.reshape(b, s, N_KV_HEADS, HEAD_DIM), pos)
    v = v.reshape(b, s, N_KV_HEADS, HEAD_DIM)
    q_idx = _rope(q_idx.reshape(b, s, IDX_HEADS, IDX_DIM), pos)
    k_idx = _rope(_layernorm(k_idx, kidx_g, kidx_b)[:, :, None, :], pos)[:, :, 0, :]
    w_idx = w_idx * (IDX_HEADS ** -0.5 * IDX_DIM ** -0.5)
    scale = HEAD_DIM ** -0.5

    def block(args):
        bi, qb, qib, wb = args
        qpos = bi * BLOCK + jnp.arange(BLOCK)
        dots = jnp.einsum('bqhd,bsd->bqhs', qib, k_idx).astype(jnp.float32)
        score = jnp.einsum('bqhs,bqh->bqs', jax.nn.relu(dots), wb.astype(jnp.float32))
        causal = pos[None, :] <= qpos[:, None]
        score = jnp.where(causal[None], score, -jnp.inf)
        _, idx = lax.top_k(score, top)
        kg = jax.vmap(lambda kk, ii: kk[ii])(k, idx)
        vg = jax.vmap(lambda vv, ii: vv[ii])(v, idx)
        valid = idx <= qpos[None, :, None]
        qg = qb.reshape(b, BLOCK, N_KV_HEADS, GROUP, HEAD_DIM)
        logits = jnp.einsum('bqgrd,bqkgd->bqgrk', qg, kg).astype(jnp.float32) * scale
        logits = jnp.where(valid[:, :, None, None, :], logits, -jnp.inf)
        p = jax.nn.softmax(logits, axis=-1).astype(v.dtype)
        o = jnp.einsum('bqgrk,bqkgd->bqgrd', p, vg)
        return o.reshape(b, BLOCK, BRANCH)

    nb = s // BLOCK
    o = lax.map(block, (jnp.arange(nb), _to_blocks(q), _to_blocks(q_idx), _to_blocks(w_idx)))
    o = _from_blocks(o)
    return (o * jax.nn.silu(gate)) @ w_out


def _fox_mixer(x, w_in, forget_bias, w_out):
    b, s, _ = x.shape
    pos = jnp.arange(s)
    q, k, v, gate, f_logit = jnp.split(x @ w_in, _offsets(B_SIZES), axis=-1)
    q = q.reshape(b, s, FOX_HEADS, FOX_HEAD_DIM)
    k = k.reshape(b, s, FOX_HEADS, FOX_HEAD_DIM)
    v = v.reshape(b, s, FOX_HEADS, FOX_HEAD_DIM)
    log_f = jax.nn.log_sigmoid(f_logit.astype(jnp.float32) + forget_bias.astype(jnp.float32))
    c = jnp.cumsum(log_f, axis=1)
    c_keys = c.transpose(0, 2, 1)
    scale = FOX_HEAD_DIM ** -0.5

    def block(args):
        bi, qb, cb = args
        qpos = bi * BLOCK + jnp.arange(BLOCK)
        logits = jnp.einsum('bqhd,bshd->bhqs', qb, k).astype(jnp.float32) * scale
        logits = logits + (cb.transpose(0, 2, 1)[:, :, :, None] - c_keys[:, :, None, :])
        causal = pos[None, :] <= qpos[:, None]
        logits = jnp.where(causal[None, None], logits, -jnp.inf)
        p = jax.nn.softmax(logits, axis=-1).astype(v.dtype)
        o = jnp.einsum('bhqs,bshd->bqhd', p, v)
        return o.reshape(b, BLOCK, BRANCH)

    nb = s // BLOCK
    o = _from_blocks(lax.map(block, (jnp.arange(nb), _to_blocks(q), _to_blocks(c))))
    return (o * jax.nn.silu(gate)) @ w_out


def setup_inputs(seed: int = 0) -> dict:
    key = jax.random.key(seed)
    ks = jax.random.split(key, 14)
    f32 = jnp.float32
    x = jax.random.normal(ks[0], (BATCH, SEQ, D_MODEL), f32)
    kv = N_KV_HEADS * HEAD_DIM
    a_scale = jnp.concatenate([jnp.ones((Q_LORA + kv,), f32), jnp.full((kv,), BETA, f32),
                               jnp.ones((IDX_DIM + IDX_HEADS + BRANCH,), f32)])
    a_w_in = jax.random.normal(ks[1], (N_A, D_MODEL, A_IN), f32) * D_MODEL ** -0.5 * a_scale
    a_q_norm_g = 1.0 + 0.01 * jax.random.normal(ks[2], (N_A, Q_LORA), f32)
    a_w_uq = jax.random.normal(ks[3], (N_A, Q_LORA, BRANCH + IDX_HEADS * IDX_DIM), f32) * Q_LORA ** -0.5
    a_kidx_norm_g = 1.0 + 0.01 * jax.random.normal(ks[4], (N_A, IDX_DIM), f32)
    a_kidx_norm_b = 0.01 * jax.random.normal(ks[5], (N_A, IDX_DIM), f32)
    a_w_out = jax.random.normal(ks[6], (N_A, BRANCH, D_MODEL), f32) * BRANCH ** -0.5 * BETA
    b_scale = jnp.concatenate([jnp.ones((2 * BRANCH,), f32), jnp.full((BRANCH,), BETA, f32),
                               jnp.ones((BRANCH + FOX_HEADS,), f32)])
    b_w_in = jax.random.normal(ks[7], (N_B, D_MODEL, B_IN), f32) * D_MODEL ** -0.5 * b_scale
    b_forget_bias = jax.random.uniform(ks[8], (N_B, FOX_HEADS), f32, minval=1.0, maxval=4.0)
    b_w_out = jax.random.normal(ks[9], (N_B, BRANCH, D_MODEL), f32) * BRANCH ** -0.5 * BETA
    ln_g = 1.0 + 0.01 * jax.random.normal(ks[10], (DEPTH, D_MODEL), f32)
    ln_b = 0.01 * jax.random.normal(ks[11], (DEPTH, D_MODEL), f32)
    return {'x': x, 'a_w_in': a_w_in, 'a_q_norm_g': a_q_norm_g, 'a_w_uq': a_w_uq,
            'a_kidx_norm_g': a_kidx_norm_g, 'a_kidx_norm_b': a_kidx_norm_b, 'a_w_out': a_w_out,
            'b_w_in': b_w_in, 'b_forget_bias': b_forget_bias, 'b_w_out': b_w_out,
            'ln_g': ln_g, 'ln_b': ln_b}


def reference(x, a_w_in, a_q_norm_g, a_w_uq, a_kidx_norm_g, a_kidx_norm_b, a_w_out,
              b_w_in, b_forget_bias, b_w_out, ln_g, ln_b):
    for i in range(DEPTH):
        j = i // N_MIXERS
        if i % N_MIXERS == 0:
            h = _dsa_mixer(x, a_w_in[j], a_q_norm_g[j], a_w_uq[j], a_kidx_norm_g[j],
                           a_kidx_norm_b[j], a_w_out[j])
        else:
            h = _fox_mixer(x, b_w_in[j], b_forget_bias[j], b_w_out[j])
        x = _layernorm(ALPHA * x + h, ln_g[i], ln_b[i])
    return x
```

```python
import functools
import math

import jax
import jax.numpy as jnp
from jax import lax
from jax.experimental import pallas as pl
from jax.experimental.pallas import tpu as pltpu

_CDT = jnp.bfloat16
_HEAD_DIM = 128
_GROUP = 8
_TOPK_MAX = 256
_ROPE_THETA = 10000.0
_LN_EPS = 1e-5
_RMS_EPS = 1e-6
_LANES = 128
_SUBLANES = 8
_VMEM_LIMIT = 56 * 1024 * 1024
_INT_MIN = -2 ** 31
_NEG_BIG = -1e30

_NT = (((1,), (1,)), ((), ()))


def _params(*sem):
    return pltpu.CompilerParams(dimension_semantics=sem, vmem_limit_bytes=_VMEM_LIMIT)


def _rope_tile(x, cos, sin_signed):
    return x * cos + pltpu.roll(x, _HEAD_DIM // 2, axis=1) * sin_signed


def _proj(a, w, epilogue, extras, extra_specs, out_shape, out_specs, tm, tn, name):
    m, k = a.shape
    n = w.shape[1]
    n_extra = len(extras)

    def body(a_ref, w_ref, *rest):
        res = jnp.dot(a_ref[...], w_ref[...], preferred_element_type=jnp.float32)
        epilogue(res, rest[:n_extra], rest[n_extra:])

    return pl.pallas_call(
        body,
        grid=(n // tn, m // tm),
        in_specs=[pl.BlockSpec((tm, k), lambda j, i: (i, 0)),
                  pl.BlockSpec((k, tn), lambda j, i: (0, j))] + list(extra_specs),
        out_specs=out_specs,
        out_shape=out_shape,
        compiler_params=_params("arbitrary", "arbitrary"),
        name=name,
    )(a, w, *extras)


def _ep_rms(res, extras, outs):
    (g_ref,) = extras
    (o_ref,) = outs
    y = res * lax.rsqrt(jnp.mean(res * res, axis=-1, keepdims=True) + _RMS_EPS)
    o_ref[...] = (y * g_ref[...]).astype(o_ref.dtype)


def _ep_heads(res, extras, outs, *, rope, scale):
    (o_ref,) = outs
    nh = res.shape[1] // _HEAD_DIM
    if rope:
        cos = extras[0][...]
        sin = extras[1][...]
    for h in range(nh):
        xh = res[:, h * _HEAD_DIM:(h + 1) * _HEAD_DIM]
        if rope:
            xh = _rope_tile(xh, cos, sin)
        if scale != 1.0:
            xh = xh * scale
        o_ref[0, h] = xh.astype(o_ref.dtype)


def _ep_kv(res, extras, outs, *, n_kv, tkv):
    cos = extras[0][...]
    sin = extras[1][...]
    k_ref, vt_ref = outs
    tm = res.shape[0]
    for h in range(n_kv):
        xh = res[:, h * _HEAD_DIM:(h + 1) * _HEAD_DIM]
        k_ref[0, h] = _rope_tile(xh, cos, sin).astype(k_ref.dtype)
    for h in range(n_kv):
        vh = res[:, (n_kv + h) * _HEAD_DIM:(n_kv + h + 1) * _HEAD_DIM]
        for c in range(tm // tkv):
            vt_ref[0, h, c] = vh[c * tkv:(c + 1) * tkv, :].T.astype(vt_ref.dtype)


def _ep_kidx(res, extras, outs, *, idx_heads, w_scale):
    cos_ref, sin_ref, g_ref, b_ref = extras
    kidx_ref, wt_ref = outs
    x = res[:, :_HEAD_DIM]
    mu = jnp.mean(x, axis=-1, keepdims=True)
    xc = x - mu
    var = jnp.mean(xc * xc, axis=-1, keepdims=True)
    y = xc * lax.rsqrt(var + _LN_EPS) * g_ref[...] + b_ref[...]
    kidx_ref[0] = _rope_tile(y, cos_ref[...], sin_ref[...]).astype(kidx_ref.dtype)
    wt = (res[:, _HEAD_DIM:] * w_scale).T
    wt_ref[0] = wt[:idx_heads, :]


def _ep_silu(res, extras, outs):
    (o_ref,) = outs
    o_ref[0, 0] = (res * jax.nn.sigmoid(res)).astype(o_ref.dtype)


def _ep_logf(res, extras, outs):
    (bias_ref,) = extras
    (o_ref,) = outs
    z = res + bias_ref[...]
    o_ref[0] = jnp.minimum(z, 0.0) - jnp.log(1.0 + jnp.exp(-jnp.abs(z)))


def _outproj(o, gate, w, tm, tn, name):
    b, g, s, c = o.shape
    d = w.shape[1]
    nsb = s // tm

    def body(o_ref, g_ref, w_ref, out_ref):
        acc = None
        for gi in range(g):
            a = (o_ref[0, gi].astype(jnp.float32) * g_ref[0, gi].astype(jnp.float32)).astype(_CDT)
            part = jnp.dot(a, w_ref[gi * c:(gi + 1) * c, :], preferred_element_type=jnp.float32)
            acc = part if acc is None else acc + part
        out_ref[...] = acc

    return pl.pallas_call(
        body,
        grid=(d // tn, b * nsb),
        in_specs=[pl.BlockSpec((1, g, tm, c), lambda j, i: (i // nsb, 0, i % nsb, 0)),
                  pl.BlockSpec((1, g, tm, c), lambda j, i: (i // nsb, 0, i % nsb, 0)),
                  pl.BlockSpec((g * c, tn), lambda j, i: (0, j))],
        out_specs=pl.BlockSpec((tm, tn), lambda j, i: (i, j)),
        out_shape=jax.ShapeDtypeStruct((b * s, d), jnp.float32),
        compiler_params=_params("arbitrary", "arbitrary"),
        name=name,
    )(o, gate, w)


def _resid_ln(h, x, g, b, alpha, tm, with_cdt, name):
    m, d = x.shape

    def body(h_ref, x_ref, g_ref, b_ref, *outs):
        y = alpha * x_ref[...] + h_ref[...]
        mu = jnp.mean(y, axis=-1, keepdims=True)
        yc = y - mu
        var = jnp.mean(yc * yc, axis=-1, keepdims=True)
        z = yc * lax.rsqrt(var + _LN_EPS) * g_ref[...] + b_ref[...]
        outs[0][...] = z
        if with_cdt:
            outs[1][...] = z.astype(_CDT)

    row = pl.BlockSpec((tm, d), lambda i: (i, 0))
    vec = pl.BlockSpec((1, d), lambda i: (0, 0))
    out_shape = [jax.ShapeDtypeStruct((m, d), jnp.float32)]
    out_specs = [row]
    if with_cdt:
        out_shape.append(jax.ShapeDtypeStruct((m, d), _CDT))
        out_specs.append(row)
    return pl.pallas_call(
        body,
        grid=(m // tm,),
        in_specs=[row, row, vec, vec],
        out_specs=out_specs,
        out_shape=out_shape,
        compiler_params=_params("arbitrary"),
        name=name,
    )(h, x, g.reshape(1, d), b.reshape(1, d))


def _float_sort_key(x):
    bits = pltpu.bitcast(x, jnp.int32)
    return bits ^ ((bits >> 31) & jnp.int32(0x7FFFFFFF))


def _dsa_core(q, qi, wt, kidx, k, vt, *, seq, topk, tk, hchunk):
    b, n_heads, s, _ = q.shape
    idx_heads = qi.shape[1]
    n_kv = k.shape[1]
    tq = _LANES
    gq = _GROUP * tq
    log_s = int(math.log2(s))
    assert 1 << log_s == s and s % tk == 0 and tk % tq == 0 and idx_heads % hchunk == 0

    def body(q_ref, qi_ref, wt_ref, kidx_ref, k_ref, vt_ref, o_ref,
             key_ref, bias_ref, m_ref, l_ref, acc_ref):
        i = pl.program_id(1)
        nkt = (i * tq + tq + tk - 1) // tk
        q_pos = i * tq + lax.broadcasted_iota(jnp.int32, (tk, tq), 1)
        s_iota = lax.broadcasted_iota(jnp.int32, (tk, tq), 0)

        def score_tile(kt, carry):
            ks = pl.multiple_of(kt * tk, tk)
            kx = kidx_ref[0, pl.ds(ks, tk), :]
            acc = jnp.zeros((tk, tq), jnp.float32)
            for hc in range(idx_heads // hchunk):
                qq = qi_ref[0, hc * hchunk:(hc + 1) * hchunk].reshape(hchunk * tq, _HEAD_DIM)
                d = lax.dot_general(kx, qq, _NT, preferred_element_type=jnp.float32)
                for hh in range(hchunk):
                    h = hc * hchunk + hh
                    acc = acc + jnp.maximum(d[:, hh * tq:(hh + 1) * tq], 0.0) * wt_ref[0, h:h + 1, :]
            key = _float_sort_key(acc + 0.0)
            key = jnp.where(ks + s_iota <= q_pos, key, _INT_MIN)
            key_ref[pl.ds(ks, tk), :] = key
            return carry

        lax.fori_loop(0, nkt, score_tile, 0)

        def count(pred_fn):
            def tile(kt, cnt):
                ks = pl.multiple_of(kt * tk, tk)
                c = jnp.where(pred_fn(key_ref[pl.ds(ks, tk), :], ks), 1.0, 0.0)
                return cnt + jnp.sum(c.reshape(tk // _SUBLANES, _SUBLANES, tq), axis=0)
            cnt = lax.fori_loop(0, nkt, tile, jnp.zeros((_SUBLANES, tq), jnp.float32))
            return jnp.sum(cnt, axis=0, keepdims=True)

        def value_step(t, prefix):
            cand = prefix | jnp.left_shift(jnp.int32(1), 31 - t)
            cand_s = cand ^ _INT_MIN
            cnt = count(lambda key, ks: key >= cand_s)
            return jnp.where(cnt >= topk, cand, prefix)

        prefix = lax.fori_loop(0, 32, value_step, jnp.zeros((1, tq), jnp.int32))
        thr = prefix ^ _INT_MIN
        need = topk - count(lambda key, ks: key > thr)

        def index_step(t, j):
            cand = j | jnp.left_shift(jnp.int32(1), log_s - 1 - t)
            cnt = count(lambda key, ks: (key == thr) & (ks + s_iota < cand))
            return jnp.where(cnt < need, cand, j)

        j_last = lax.fori_loop(0, log_s, index_step, jnp.zeros((1, tq), jnp.int32))

        def bias_tile(kt, carry):
            ks = pl.multiple_of(kt * tk, tk)
            key = key_ref[pl.ds(ks, tk), :]
            s_pos = ks + s_iota
            sel = (key > thr) | ((key == thr) & (s_pos <= j_last))
            bias_ref[pl.ds(ks, tk), :] = jnp.where(sel & (s_pos <= q_pos), 0.0, -jnp.inf)
            return carry

        lax.fori_loop(0, nkt, bias_tile, 0)

        def kv_group(g, carry):
            qg = q_ref[0, pl.ds(g * _GROUP, _GROUP)].reshape(gq, _HEAD_DIM)
            m_ref[...] = jnp.full((1, gq), _NEG_BIG, jnp.float32)
            l_ref[...] = jnp.zeros((1, gq), jnp.float32)
            acc_ref[...] = jnp.zeros((_HEAD_DIM, gq), jnp.float32)

            def att_tile(kt, c2):
                ks = pl.multiple_of(kt * tk, tk)
                kk = k_ref[0, g, pl.ds(ks, tk), :]
                sc = lax.dot_general(kk, qg, _NT, preferred_element_type=jnp.float32)
                bias = bias_ref[pl.ds(ks, tk), :]
                sc = sc + jnp.concatenate([bias] * _GROUP, axis=1)
                m_old = m_ref[...]
                m_new = jnp.maximum(m_old, jnp.max(sc, axis=0, keepdims=True))
                p = jnp.exp(sc - m_new)
                alpha = jnp.exp(m_old - m_new)
                l_ref[...] = alpha * l_ref[...] + jnp.sum(p, axis=0, keepdims=True)
                pv = jnp.dot(vt_ref[0, g, kt], p.astype(_CDT), preferred_element_type=jnp.float32)
                acc_ref[...] = alpha * acc_ref[...] + pv
                m_ref[...] = m_new
                return c2

            lax.fori_loop(0, nkt, att_tile, 0)
            ot = acc_ref[...] * (1.0 / l_ref[...])
            for hh in range(_GROUP):
                o_ref[0, g, :, hh * _HEAD_DIM:(hh + 1) * _HEAD_DIM] = (
                    ot[:, hh * tq:(hh + 1) * tq].T.astype(o_ref.dtype))
            return carry

        lax.fori_loop(0, n_kv, kv_group, 0)

    nq = s // tq
    return pl.pallas_call(
        body,
        grid=(b, nq),
        in_specs=[pl.BlockSpec((1, n_heads, tq, _HEAD_DIM), lambda bi, i: (bi, 0, i, 0)),
                  pl.BlockSpec((1, idx_heads, tq, _HEAD_DIM), lambda bi, i: (bi, 0, i, 0)),
                  pl.BlockSpec((1, idx_heads, tq), lambda bi, i: (bi, 0, i)),
                  pl.BlockSpec((1, s, _HEAD_DIM), lambda bi, i: (bi, 0, 0)),
                  pl.BlockSpec((1, n_kv, s, _HEAD_DIM), lambda bi, i: (bi, 0, 0, 0)),
                  pl.BlockSpec((1, n_kv, s // tk, _HEAD_DIM, tk), lambda bi, i: (bi, 0, 0, 0, 0))],
        out_specs=pl.BlockSpec((1, n_kv, tq, _GROUP * _HEAD_DIM), lambda bi, i: (bi, 0, i, 0)),
        out_shape=jax.ShapeDtypeStruct((b, n_kv, s, _GROUP * _HEAD_DIM), _CDT),
        scratch_shapes=[pltpu.VMEM((s, tq), jnp.int32),
                        pltpu.VMEM((s, tq), jnp.float32),
                        pltpu.VMEM((1, gq), jnp.float32),
                        pltpu.VMEM((1, gq), jnp.float32),
                        pltpu.VMEM((_HEAD_DIM, gq), jnp.float32)],
        compiler_params=_params("arbitrary", "arbitrary"),
        name="dsa_core",
    )(q, qi, wt, kidx, k, vt)


def _cumsum_t(lf, n_heads):
    b, s, w = lf.shape
    blk = _LANES

    def body(lf_ref, ct_ref, carry_ref):
        @pl.when(pl.program_id(1) == 0)
        def _():
            carry_ref[...] = jnp.zeros_like(carry_ref)
        r = lax.broadcasted_iota(jnp.int32, (blk, blk), 0)
        c = lax.broadcasted_iota(jnp.int32, (blk, blk), 1)
        tri = jnp.where(r >= c, 1.0, 0.0).astype(jnp.float32)
        cs = jnp.dot(tri, lf_ref[0], preferred_element_type=jnp.float32,
                     precision=lax.Precision.HIGHEST) + carry_ref[...]
        carry_ref[...] = cs[blk - 1:blk, :]
        ct_ref[0] = cs.T[:n_heads, :]

    return pl.pallas_call(
        body,
        grid=(b, s // blk),
        in_specs=[pl.BlockSpec((1, blk, w), lambda bi, i: (bi, i, 0))],
        out_specs=pl.BlockSpec((1, n_heads, blk), lambda bi, i: (bi, 0, i)),
        out_shape=jax.ShapeDtypeStruct((b, n_heads, s), jnp.float32),
        scratch_shapes=[pltpu.VMEM((1, w), jnp.float32)],
        compiler_params=_params("arbitrary", "arbitrary"),
        name="fox_cumsum",
    )(lf)


def _fox_core(q, k, v, ct, *, t):
    b, n_heads, s, _ = q.shape

    def body(q_ref, k_ref, v_ref, ct_ref, o_ref):
        i = pl.program_id(2)
        qq = q_ref[0, 0]

        def tile(j, carry, diagonal):
            m_old, l_old, acc = carry
            ks = pl.multiple_of(j * t, t)
            kk = k_ref[0, 0, pl.ds(ks, t), :]
            sc = lax.dot_general(qq, kk, _NT, preferred_element_type=jnp.float32) - ct_ref[0, 0, j]
            if diagonal:
                r = lax.broadcasted_iota(jnp.int32, (t, t), 0)
                c = lax.broadcasted_iota(jnp.int32, (t, t), 1)
                sc = jnp.where(c <= r, sc, -jnp.inf)
            m_new = jnp.maximum(m_old, jnp.max(sc, axis=1, keepdims=True))
            p = jnp.exp(sc - m_new)
            alpha = jnp.exp(m_old - m_new)
            l_new = alpha * l_old + jnp.sum(p, axis=1, keepdims=True)
            pv = jnp.dot(p.astype(_CDT), v_ref[0, 0, pl.ds(ks, t), :],
                         preferred_element_type=jnp.float32)
            return m_new, l_new, alpha * acc + pv

        init = (jnp.full((t, 1), _NEG_BIG, jnp.float32), jnp.zeros((t, 1), jnp.float32),
                jnp.zeros((t, _HEAD_DIM), jnp.float32))
        carry = lax.fori_loop(0, i, functools.partial(tile, diagonal=False), init)
        _, l_fin, acc = tile(i, carry, True)
        o_ref[0] = (acc * (1.0 / l_fin)).astype(o_ref.dtype)

    return pl.pallas_call(
        body,
        grid=(b, n_heads, s // t),
        in_specs=[pl.BlockSpec((1, 1, t, _HEAD_DIM), lambda bi, h, i: (bi, h, i, 0)),
                  pl.BlockSpec((1, 1, s, _HEAD_DIM), lambda bi, h, i: (bi, h, 0, 0)),
                  pl.BlockSpec((1, 1, s, _HEAD_DIM), lambda bi, h, i: (bi, h, 0, 0)),
                  pl.BlockSpec((1, 1, s // t, 1, t), lambda bi, h, i: (bi, h, 0, 0, 0))],
        out_specs=pl.BlockSpec((1, t, _HEAD_DIM), lambda bi, h, i: (bi, i, h)),
        out_shape=jax.ShapeDtypeStruct((b, s, n_heads * _HEAD_DIM), _CDT),
        compiler_params=_params("arbitrary", "arbitrary", "arbitrary"),
        name="fox_core",
    )(q, k, v, ct)


def _rope_tables(s):
    half = _HEAD_DIM // 2
    inv = _ROPE_THETA ** (-jnp.arange(half, dtype=jnp.float32) / half)
    ang = jnp.arange(s, dtype=jnp.float32)[:, None] * inv[None, :]
    cos, sin = jnp.cos(ang), jnp.sin(ang)
    return jnp.concatenate([cos, cos], axis=1), jnp.concatenate([-sin, sin], axis=1)


def _tiles(s, n):
    return min(512, s), min(1024, n)


def _heads_call(a, w, b, s, *, rope, scale, cos, sin, name):
    n = w.shape[1]
    tm, tn = _tiles(s, n)
    nsb = s // tm
    extras, extra_specs = (), ()
    if rope:
        extras = (cos, sin)
        extra_specs = (pl.BlockSpec((tm, _HEAD_DIM), lambda j, i: (i % nsb, 0)),) * 2
    nh_t = tn // _HEAD_DIM
    return _proj(
        a, w, functools.partial(_ep_heads, rope=rope, scale=scale), extras, extra_specs,
        jax.ShapeDtypeStruct((b, n // _HEAD_DIM, s, _HEAD_DIM), _CDT),
        pl.BlockSpec((1, nh_t, tm, _HEAD_DIM), lambda j, i: (i // nsb, j, i % nsb, 0)),
        tm, tn, name)


def _gate_call(a, w, b, s, groups, name):
    n = w.shape[1]
    c = n // groups
    tm, tn = _tiles(s, c)
    nsb = s // tm
    per = c // tn
    return _proj(
        a, w, _ep_silu, (), (),
        jax.ShapeDtypeStruct((b, groups, s, c), _CDT),
        pl.BlockSpec((1, 1, tm, tn), lambda j, i: (i // nsb, j // per, i % nsb, j % per)),
        tm, tn, name)


def _dsa_layer(xc, w_in, q_norm_g, w_uq, kidx_g, kidx_b, w_out, b, s):
    d = xc.shape[1]
    q_lora = w_uq.shape[0]
    branch = w_out.shape[0]
    idx_heads = (w_uq.shape[1] - branch) // _HEAD_DIM
    kv = (w_in.shape[1] - q_lora - _HEAD_DIM - idx_heads - branch) // 2
    n_kv = kv // _HEAD_DIM
    n_heads = branch // _HEAD_DIM
    assert n_heads == n_kv * _GROUP and idx_heads <= _LANES
    topk = min(_TOPK_MAX, s // 4)
    o0 = q_lora
    o1 = o0 + 2 * kv
    o2 = o1 + _HEAD_DIM
    o3 = o2 + idx_heads
    w_cq = w_in[:, :o0].astype(_CDT)
    w_kv = w_in[:, o0:o1].astype(_CDT)
    w_ki = jnp.pad(w_in[:, o1:o3], ((0, 0), (0, _LANES - idx_heads))).astype(_CDT)
    w_gate = w_in[:, o3:].astype(_CDT)
    w_q = w_uq[:, :branch].astype(_CDT)
    w_qi = w_uq[:, branch:].astype(_CDT)
    cos, sin = _rope_tables(s)
    tm, _ = _tiles(s, d)
    nsb = s // tm
    tk = min(256, s)
    rope_specs = (pl.BlockSpec((tm, _HEAD_DIM), lambda j, i: (i % nsb, 0)),) * 2

    cq = _proj(xc, w_cq, _ep_rms, (q_norm_g.reshape(1, q_lora),),
               (pl.BlockSpec((1, q_lora), lambda j, i: (0, 0)),),
               jax.ShapeDtypeStruct((b * s, q_lora), _CDT),
               pl.BlockSpec((tm, q_lora), lambda j, i: (i, 0)), tm, q_lora, "dsa_cq")
    k, vt = _proj(xc, w_kv, functools.partial(_ep_kv, n_kv=n_kv, tkv=tk), (cos, sin), rope_specs,
                  [jax.ShapeDtypeStruct((b, n_kv, s, _HEAD_DIM), _CDT),
                   jax.ShapeDtypeStruct((b, n_kv, s // tk, _HEAD_DIM, tk), _CDT)],
                  [pl.BlockSpec((1, n_kv, tm, _HEAD_DIM), lambda j, i: (i // nsb, 0, i % nsb, 0)),
                   pl.BlockSpec((1, n_kv, tm // tk, _HEAD_DIM, tk),
                                lambda j, i: (i // nsb, 0, i % nsb, 0, 0))],
                  tm, 2 * kv, "dsa_kv")
    w_scale = idx_heads ** -0.5 * _HEAD_DIM ** -0.5
    vec = pl.BlockSpec((1, _HEAD_DIM), lambda j, i: (0, 0))
    kidx, wt = _proj(xc, w_ki, functools.partial(_ep_kidx, idx_heads=idx_heads, w_scale=w_scale),
                     (cos, sin, kidx_g.reshape(1, _HEAD_DIM), kidx_b.reshape(1, _HEAD_DIM)),
                     rope_specs + (vec, vec),
                     [jax.ShapeDtypeStruct((b, s, _HEAD_DIM), _CDT),
                      jax.ShapeDtypeStruct((b, idx_heads, s), jnp.float32)],
                     [pl.BlockSpec((1, tm, _HEAD_DIM), lambda j, i: (i // nsb, i % nsb, 0)),
                      pl.BlockSpec((1, idx_heads, tm), lambda j, i: (i // nsb, 0, i % nsb))],
                     tm, 2 * _LANES, "dsa_kidx")
    gate = _gate_call(xc, w_gate, b, s, n_kv, "dsa_gate")
    q = _heads_call(cq, w_q, b, s, rope=True, scale=_HEAD_DIM ** -0.5, cos=cos, sin=sin, name="dsa_q")
    qi = _heads_call(cq, w_qi, b, s, rope=True, scale=1.0, cos=cos, sin=sin, name="dsa_qi")
    o = _dsa_core(q, qi, wt, kidx, k, vt, seq=s, topk=topk, tk=tk, hchunk=min(8, idx_heads))
    return _outproj(o, gate, w_out.astype(_CDT), tm, min(1024, d), "dsa_out")


def _fox_layer(xc, w_in, forget_bias, w_out, b, s):
    d = xc.shape[1]
    branch = w_out.shape[0]
    n_heads = forget_bias.shape[0]
    assert branch == n_heads * _HEAD_DIM and n_heads <= _LANES
    w_q = w_in[:, :branch].astype(_CDT)
    w_k = w_in[:, branch:2 * branch].astype(_CDT)
    w_v = w_in[:, 2 * branch:3 * branch].astype(_CDT)
    w_gate = w_in[:, 3 * branch:4 * branch].astype(_CDT)
    w_f = jnp.pad(w_in[:, 4 * branch:], ((0, 0), (0, _LANES - n_heads))).astype(_CDT)
    fb = jnp.pad(forget_bias, (0, _LANES - n_heads)).reshape(1, _LANES)
    tm, _ = _tiles(s, d)
    nsb = s // tm
    t = min(512, s)
    q = _heads_call(xc, w_q, b, s, rope=False, scale=_HEAD_DIM ** -0.5, cos=None, sin=None, name="fox_q")
    k = _heads_call(xc, w_k, b, s, rope=False, scale=1.0, cos=None, sin=None, name="fox_k")
    v = _heads_call(xc, w_v, b, s, rope=False, scale=1.0, cos=None, sin=None, name="fox_v")
    gate = _gate_call(xc, w_gate, b, s, 1, "fox_gate")
    lf = _proj(xc, w_f, _ep_logf, (fb,), (pl.BlockSpec((1, _LANES), lambda j, i: (0, 0)),),
               jax.ShapeDtypeStruct((b, s, _LANES), jnp.float32),
               pl.BlockSpec((1, tm, _LANES), lambda j, i: (i // nsb, i % nsb, 0)),
               tm, _LANES, "fox_logf")
    ct = _cumsum_t(lf, n_heads).reshape(b, n_heads, s // t, 1, t)
    o = _fox_core(q, k, v, ct, t=t)
    return _outproj(o.reshape(b, 1, s, branch), gate, w_out.astype(_CDT), tm, min(1024, d), "fox_out")


def kernel(x, a_w_in, a_q_norm_g, a_w_uq, a_kidx_norm_g, a_kidx_norm_b, a_w_out,
           b_w_in, b_forget_bias, b_w_out, ln_g, ln_b):
    b, s, d = x.shape
    depth = ln_g.shape[0]
    alpha = (2 * depth) ** 0.25
    xf = x.reshape(b * s, d)
    xc = xf.astype(_CDT)
    tm_ln = min(256, s)
    for i in range(depth):
        j = i // 2
        if i % 2 == 0:
            h = _dsa_layer(xc, a_w_in[j], a_q_norm_g[j], a_w_uq[j], a_kidx_norm_g[j],
                           a_kidx_norm_b[j], a_w_out[j], b, s)
        else:
            h = _fox_layer(xc, b_w_in[j], b_forget_bias[j], b_w_out[j], b, s)
        last = i == depth - 1
        outs = _resid_ln(h, xf, ln_g[i], ln_b[i], alpha, tm_ln, not last, "resid_ln_%d" % i)
        xf = outs[0]
        if not last:
            xc = outs[1]
    return xf.reshape(b, s, d)
```

```python
import functools
import math

import jax
import jax.numpy as jnp
from jax import lax
from jax.experimental import pallas as pl
from jax.experimental.pallas import tpu as pltpu

_CDT = jnp.bfloat16
_HEAD_DIM = 128
_GROUP = 8
_TOPK_MAX = 256
_ROPE_THETA = 10000.0
_LN_EPS = 1e-5
_RMS_EPS = 1e-6
_LANES = 128
_SUBLANES = 8
_MXU_DIM = 256
_VMEM_LIMIT = 56 * 1024 * 1024
_VT_ROWS = _HEAD_DIM + 16
_INT_MIN = -2 ** 31
_MASKED = -1e30
_M_INIT = -1e29
_LOG2E = math.log2(math.e)

_NT = (((1,), (1,)), ((), ()))


def _params(*sem):
    return pltpu.CompilerParams(dimension_semantics=sem, vmem_limit_bytes=_VMEM_LIMIT)


def _rope_tile(x, cos, sin_signed):
    return x * cos + pltpu.roll(x, _HEAD_DIM // 2, axis=1) * sin_signed


def _proj(a, w, epilogue, extras, extra_specs, out_shape, out_specs, tm, tn, name):
    m, k = a.shape
    n = w.shape[1]
    n_extra = len(extras)

    def body(a_ref, w_ref, *rest):
        res = jnp.dot(a_ref[...], w_ref[...], preferred_element_type=jnp.float32)
        epilogue(res, rest[:n_extra], rest[n_extra:])

    return pl.pallas_call(
        body,
        grid=(n // tn, m // tm),
        in_specs=[pl.BlockSpec((tm, k), lambda j, i: (i, 0)),
                  pl.BlockSpec((k, tn), lambda j, i: (0, j))] + list(extra_specs),
        out_specs=out_specs,
        out_shape=out_shape,
        compiler_params=_params("arbitrary", "arbitrary"),
        name=name,
    )(a, w, *extras)


def _ep_rms(res, extras, outs):
    (g_ref,) = extras
    (o_ref,) = outs
    y = res * lax.rsqrt(jnp.mean(res * res, axis=-1, keepdims=True) + _RMS_EPS)
    o_ref[...] = (y * g_ref[...]).astype(o_ref.dtype)


def _ep_heads(res, extras, outs, *, rope, scale):
    (o_ref,) = outs
    nh = res.shape[1] // _HEAD_DIM
    if rope:
        cos = extras[0][...]
        sin = extras[1][...]
    for h in range(nh):
        xh = res[:, h * _HEAD_DIM:(h + 1) * _HEAD_DIM]
        if rope:
            xh = _rope_tile(xh, cos, sin)
        if scale != 1.0:
            xh = xh * scale
        o_ref[0, h] = xh.astype(o_ref.dtype)


def _store_vt(vt_ref, h, vh, t):
    for c in range(vh.shape[0] // t):
        vt_ref[0, h, c, :_HEAD_DIM, :] = vh[c * t:(c + 1) * t, :].T.astype(vt_ref.dtype)
        vt_ref[0, h, c, _HEAD_DIM:, :] = jnp.ones((_VT_ROWS - _HEAD_DIM, t), vt_ref.dtype)


def _ep_vt(res, extras, outs, *, t):
    (vt_ref,) = outs
    for h in range(res.shape[1] // _HEAD_DIM):
        _store_vt(vt_ref, h, res[:, h * _HEAD_DIM:(h + 1) * _HEAD_DIM], t)


def _ep_kv(res, extras, outs, *, n_kv, t):
    cos = extras[0][...]
    sin = extras[1][...]
    k_ref, vt_ref = outs
    for h in range(n_kv):
        xh = res[:, h * _HEAD_DIM:(h + 1) * _HEAD_DIM]
        k_ref[0, h] = _rope_tile(xh, cos, sin).astype(k_ref.dtype)
    for h in range(n_kv):
        _store_vt(vt_ref, h, res[:, (n_kv + h) * _HEAD_DIM:(n_kv + h + 1) * _HEAD_DIM], t)


def _ep_kidx(res, extras, outs, *, idx_heads, w_scale):
    cos_ref, sin_ref, g_ref, b_ref = extras
    kidx_ref, wt_ref = outs
    x = res[:, :_HEAD_DIM]
    mu = jnp.mean(x, axis=-1, keepdims=True)
    xc = x - mu
    var = jnp.mean(xc * xc, axis=-1, keepdims=True)
    y = xc * lax.rsqrt(var + _LN_EPS) * g_ref[...] + b_ref[...]
    kidx_ref[0] = _rope_tile(y, cos_ref[...], sin_ref[...]).astype(kidx_ref.dtype)
    wt = (res[:, _HEAD_DIM:] * w_scale).T
    wt_ref[0] = wt[:idx_heads, :]


def _ep_silu(res, extras, outs):
    (o_ref,) = outs
    o_ref[0, 0] = (res * jax.nn.sigmoid(res)).astype(o_ref.dtype)


def _ep_logf(res, extras, outs):
    (bias_ref,) = extras
    (o_ref,) = outs
    z = res + bias_ref[...]
    o_ref[0] = jnp.minimum(z, 0.0) - jnp.log(1.0 + jnp.exp(-jnp.abs(z)))


def _outproj(o, gate, w, tm, tn, name):
    b, g, s, c = o.shape
    d = w.shape[1]
    nsb = s // tm

    def body(o_ref, g_ref, w_ref, out_ref):
        acc = None
        for gi in range(g):
            a = (o_ref[0, gi].astype(jnp.float32) * g_ref[0, gi].astype(jnp.float32)).astype(_CDT)
            part = jnp.dot(a, w_ref[gi * c:(gi + 1) * c, :], preferred_element_type=jnp.float32)
            acc = part if acc is None else acc + part
        out_ref[...] = acc

    return pl.pallas_call(
        body,
        grid=(d // tn, b * nsb),
        in_specs=[pl.BlockSpec((1, g, tm, c), lambda j, i: (i // nsb, 0, i % nsb, 0)),
                  pl.BlockSpec((1, g, tm, c), lambda j, i: (i // nsb, 0, i % nsb, 0)),
                  pl.BlockSpec((g * c, tn), lambda j, i: (0, j))],
        out_specs=pl.BlockSpec((tm, tn), lambda j, i: (i, j)),
        out_shape=jax.ShapeDtypeStruct((b * s, d), jnp.float32),
        compiler_params=_params("arbitrary", "arbitrary"),
        name=name,
    )(o, gate, w)


def _resid_ln(h, x, g, b, alpha, tm, with_cdt, name):
    m, d = x.shape

    def body(h_ref, x_ref, g_ref, b_ref, *outs):
        y = alpha * x_ref[...] + h_ref[...]
        mu = jnp.mean(y, axis=-1, keepdims=True)
        yc = y - mu
        var = jnp.mean(yc * yc, axis=-1, keepdims=True)
        z = yc * lax.rsqrt(var + _LN_EPS) * g_ref[...] + b_ref[...]
        outs[0][...] = z
        if with_cdt:
            outs[1][...] = z.astype(_CDT)

    row = pl.BlockSpec((tm, d), lambda i: (i, 0))
    vec = pl.BlockSpec((1, d), lambda i: (0, 0))
    out_shape = [jax.ShapeDtypeStruct((m, d), jnp.float32)]
    out_specs = [row]
    if with_cdt:
        out_shape.append(jax.ShapeDtypeStruct((m, d), _CDT))
        out_specs.append(row)
    return pl.pallas_call(
        body,
        grid=(m // tm,),
        in_specs=[row, row, vec, vec],
        out_specs=out_specs,
        out_shape=out_shape,
        compiler_params=_params("arbitrary"),
        name=name,
    )(h, x, g.reshape(1, d), b.reshape(1, d))


def _col_reduce(x, op):
    rows, n = x.shape
    fold = 8 * _SUBLANES
    if rows > fold and rows % fold == 0:
        x = op(x.reshape(rows // fold, fold, n), axis=0)
    return op(x, axis=0, keepdims=True)


def _softmax_step(sc, m_ref, cols):
    m_old = m_ref[:, cols]
    m_new = jnp.maximum(m_old, _col_reduce(sc, jnp.max))
    m_ref[:, cols] = m_new
    return jnp.exp2(sc - m_new).astype(_CDT), jnp.exp2(m_old - m_new)


def _pv_step(vta, p, alpha, acc_ref, cols):
    acc_ref[:, cols] = alpha * acc_ref[:, cols] + jnp.dot(vta, p, preferred_element_type=jnp.float32)


def _float_sort_key(x):
    bits = pltpu.bitcast(x, jnp.int32)
    return bits ^ ((bits >> 31) & jnp.int32(0x7FFFFFFF))


def _dsa_core(q, qi, wt, kidx, k, vt, *, topk, tk1, tk, hchunk):
    b, n_heads, s, _ = q.shape
    idx_heads = qi.shape[1]
    n_kv = k.shape[1]
    tq = _LANES
    gq = _GROUP * tq
    log_s = int(math.log2(s))
    assert 1 << log_s == s and s % tk == 0 and tk % tk1 == 0 and tk1 % tq == 0
    assert idx_heads % hchunk == 0 and gq % _MXU_DIM == 0

    def body(q_ref, qi_ref, wt_ref, kidx_ref, k_ref, vt_ref, o_ref,
             key_ref, mask_ref, sc_ref, m_ref, acc_ref):
        i = pl.program_id(1)
        n1 = (i * tq + tq + tk1 - 1) // tk1
        n3 = (i * tq + tq + tk - 1) // tk

        q_pos1 = i * tq + lax.broadcasted_iota(jnp.int32, (tk1, tq), 1)
        s_iota1 = lax.broadcasted_iota(jnp.int32, (tk1, tq), 0)

        def score_tile(kt, carry):
            ks = pl.multiple_of(kt * tk1, tk1)
            kx = kidx_ref[0, pl.ds(ks, tk1), :]
            acc = jnp.zeros((tk1, tq), jnp.float32)
            for hc in range(idx_heads // hchunk):
                qq = qi_ref[0, hc * hchunk:(hc + 1) * hchunk].reshape(hchunk * tq, _HEAD_DIM)
                d = lax.dot_general(kx, qq, _NT, preferred_element_type=jnp.float32)
                for hh in range(hchunk):
                    h = hc * hchunk + hh
                    acc = acc + jnp.maximum(d[:, hh * tq:(hh + 1) * tq], 0.0) * wt_ref[0, h:h + 1, :]
            key = _float_sort_key(acc + 0.0)
            key_ref[pl.ds(ks, tk1), :] = jnp.where(ks + s_iota1 <= q_pos1, key, _INT_MIN)
            return carry

        lax.fori_loop(0, n1, score_tile, 0)

        def pad_tile(kt, carry):
            key_ref[pl.ds(pl.multiple_of(kt * tk1, tk1), tk1), :] = jnp.full((tk1, tq), _INT_MIN, jnp.int32)
            return carry

        lax.fori_loop(n1, n3 * (tk // tk1), pad_tile, 0)

        q_pos = i * tq + lax.broadcasted_iota(jnp.int32, (tk, tq), 1)
        s_iota = lax.broadcasted_iota(jnp.int32, (tk, tq), 0)

        fold = 8 * _SUBLANES

        def count(pred_fn):
            def tile(kt, cnt):
                ks = pl.multiple_of(kt * tk, tk)
                c = jnp.where(pred_fn(key_ref[pl.ds(ks, tk), :], ks), 1.0, 0.0)
                return cnt + jnp.sum(c.reshape(tk // fold, fold, tq), axis=0)
            cnt = lax.fori_loop(0, n3, tile, jnp.zeros((fold, tq), jnp.float32))
            return jnp.sum(cnt, axis=0, keepdims=True)

        def value_step(t, prefix):
            cand = prefix | jnp.left_shift(jnp.int32(1), 31 - t)
            cand_s = cand ^ _INT_MIN
            cnt = count(lambda key, ks: key >= cand_s)
            return jnp.where(cnt >= topk, cand, prefix)

        prefix = lax.fori_loop(0, 32, value_step, jnp.zeros((1, tq), jnp.int32))
        thr = prefix ^ _INT_MIN

        def tie_break():
            need = topk - count(lambda key, ks: key > thr)

            def index_step(t, j):
                cand = j | jnp.left_shift(jnp.int32(1), log_s - 1 - t)
                cnt = count(lambda key, ks: (key == thr) & (ks + s_iota < cand))
                return jnp.where(cnt < need, cand, j)

            return lax.fori_loop(0, log_s, index_step, jnp.zeros((1, tq), jnp.int32))

        n_ge = count(lambda key, ks: key >= thr)
        j_last = lax.cond(jnp.max(n_ge) > topk, tie_break,
                          lambda: jnp.full((1, tq), s, jnp.int32))

        def mask_tile(kt, carry):
            ks = pl.multiple_of(kt * tk, tk)
            key = key_ref[pl.ds(ks, tk), :]
            s_pos = ks + s_iota
            sel = (key > thr) | ((key == thr) & (s_pos <= j_last))
            mask_ref[pl.ds(ks, tk), :] = jnp.where(sel & (s_pos <= q_pos), 0.0, _MASKED).astype(_CDT)
            return carry

        lax.fori_loop(0, n3, mask_tile, 0)

        r = lax.broadcasted_iota(jnp.int32, (gq, tq), 0)
        c = lax.broadcasted_iota(jnp.int32, (gq, tq), 1)
        one_hot = jnp.where((r & (tq - 1)) == c, 1.0, 0.0).astype(_CDT)

        chunks = [slice(cc * _MXU_DIM, (cc + 1) * _MXU_DIM) for cc in range(gq // _MXU_DIM)]

        def kv_group(g, carry):
            qg = q_ref[0, pl.ds(g * _GROUP, _GROUP)].reshape(gq, _HEAD_DIM)
            qaug = jnp.concatenate([qg, one_hot], axis=1)
            m_ref[...] = jnp.full((1, gq), _M_INIT, jnp.float32)
            acc_ref[...] = jnp.zeros((_VT_ROWS, gq), jnp.float32)

            def logits(kt, slot):
                ks = pl.multiple_of(kt * tk, tk)
                kaug = jnp.concatenate([k_ref[0, g, pl.ds(ks, tk), :], mask_ref[pl.ds(ks, tk), :]], axis=1)
                for cols in chunks:
                    sc_ref[slot, :, cols] = lax.dot_general(kaug, qaug[cols], _NT,
                                                            preferred_element_type=jnp.float32)

            def softmax_pv(kt, slot, prefetch):
                pa = [_softmax_step(sc_ref[slot, :, cols], m_ref, cols) for cols in chunks]
                if prefetch:
                    logits(kt + 1, 1 - slot)
                vta = vt_ref[0, g, kt]
                for cols, (p, alpha) in zip(chunks, pa):
                    _pv_step(vta, p, alpha, acc_ref, cols)

            def att_tile(kt, c2):
                softmax_pv(kt, kt & 1, True)
                return c2

            logits(0, 0)
            lax.fori_loop(0, n3 - 1, att_tile, 0)
            softmax_pv(n3 - 1, (n3 - 1) & 1, False)
            acc = acc_ref[...]
            ot = acc[:_HEAD_DIM] * (1.0 / acc[_HEAD_DIM:_HEAD_DIM + 1])
            for hh in range(_GROUP):
                o_ref[0, g, :, hh * _HEAD_DIM:(hh + 1) * _HEAD_DIM] = (
                    ot[:, hh * tq:(hh + 1) * tq].T.astype(o_ref.dtype))
            return carry

        lax.fori_loop(0, n_kv, kv_group, 0)

    nq = s // tq
    return pl.pallas_call(
        body,
        grid=(b, nq),
        in_specs=[pl.BlockSpec((1, n_heads, tq, _HEAD_DIM), lambda bi, i: (bi, 0, i, 0)),
                  pl.BlockSpec((1, idx_heads, tq, _HEAD_DIM), lambda bi, i: (bi, 0, i, 0)),
                  pl.BlockSpec((1, idx_heads, tq), lambda bi, i: (bi, 0, i)),
                  pl.BlockSpec((1, s, _HEAD_DIM), lambda bi, i: (bi, 0, 0)),
                  pl.BlockSpec((1, n_kv, s, _HEAD_DIM), lambda bi, i: (bi, 0, 0, 0)),
                  pl.BlockSpec((1, n_kv, s // tk, _VT_ROWS, tk), lambda bi, i: (bi, 0, 0, 0, 0))],
        out_specs=pl.BlockSpec((1, n_kv, tq, _GROUP * _HEAD_DIM), lambda bi, i: (bi, 0, i, 0)),
        out_shape=jax.ShapeDtypeStruct((b, n_kv, s, _GROUP * _HEAD_DIM), _CDT),
        scratch_shapes=[pltpu.VMEM((s, tq), jnp.int32),
                        pltpu.VMEM((s, tq), _CDT),
                        pltpu.VMEM((2, tk, gq), jnp.float32),
                        pltpu.VMEM((1, gq), jnp.float32),
                        pltpu.VMEM((_VT_ROWS, gq), jnp.float32)],
        compiler_params=_params("arbitrary", "arbitrary"),
        name="dsa_core",
    )(q, qi, wt, kidx, k, vt)


def _cumsum_aug(lf, n_heads):
    b, s, w = lf.shape
    blk = _LANES

    def body(lf_ref, out_ref, carry_ref):
        @pl.when(pl.program_id(1) == 0)
        def _():
            carry_ref[...] = jnp.zeros_like(carry_ref)
        r = lax.broadcasted_iota(jnp.int32, (blk, blk), 0)
        c = lax.broadcasted_iota(jnp.int32, (blk, blk), 1)
        tri = jnp.where(r >= c, 1.0, 0.0).astype(jnp.float32)
        cs = jnp.dot(tri, lf_ref[0], preferred_element_type=jnp.float32,
                     precision=lax.Precision.HIGHEST) + carry_ref[...]
        carry_ref[...] = cs[blk - 1:blk, :]
        c2 = cs * _LOG2E
        for h in range(n_heads):
            col = c2[:, h:h + 1]
            hi = col.astype(_CDT).astype(jnp.float32)
            mid = (col - hi).astype(_CDT).astype(jnp.float32)
            lo = (col - hi - mid).astype(_CDT).astype(jnp.float32)
            tile = jnp.where(c == 0, hi, jnp.where(c == 1, mid, jnp.where(c == 2, lo, 0.0)))
            out_ref[0, h] = tile.astype(out_ref.dtype)

    return pl.pallas_call(
        body,
        grid=(b, s // blk),
        in_specs=[pl.BlockSpec((1, blk, w), lambda bi, i: (bi, i, 0))],
        out_specs=pl.BlockSpec((1, n_heads, blk, _LANES), lambda bi, i: (bi, 0, i, 0)),
        out_shape=jax.ShapeDtypeStruct((b, n_heads, s, _LANES), _CDT),
        scratch_shapes=[pltpu.VMEM((1, w), jnp.float32)],
        compiler_params=_params("arbitrary", "arbitrary"),
        name="fox_cumsum",
    )(lf)


def _fox_core(q, k, caug, vt, *, t):
    b, n_heads, s, _ = q.shape
    assert t % _MXU_DIM == 0 and s % t == 0

    chunks = [slice(cc * _MXU_DIM, (cc + 1) * _MXU_DIM) for cc in range(t // _MXU_DIM)]

    def body(q_ref, k_ref, c_ref, vt_ref, o_ref, kaug_ref, sc_ref, m_ref, acc_ref):
        i = pl.program_id(2)

        @pl.when(i == 0)
        def _():
            kaug_ref[:, :_HEAD_DIM] = k_ref[0, 0]
            kaug_ref[:, _HEAD_DIM:] = c_ref[0, 0]

        lane = lax.broadcasted_iota(jnp.int32, (t, _HEAD_DIM), 1)
        qaug = jnp.concatenate([q_ref[0, 0], jnp.where(lane < 3, -1.0, 0.0).astype(_CDT)], axis=1)
        m_ref[...] = jnp.full((1, t), _M_INIT, jnp.float32)
        acc_ref[...] = jnp.zeros((_VT_ROWS, t), jnp.float32)

        def logits_chunk(j, slot, cols):
            ka = kaug_ref[pl.ds(pl.multiple_of(j * t, t), t), :]
            sc_ref[slot, :, cols] = lax.dot_general(ka, qaug[cols], _NT,
                                                    preferred_element_type=jnp.float32)

        def logits(j, slot):
            for cols in chunks:
                logits_chunk(j, slot, cols)

        def softmax_pv(j, slot, diagonal):
            vta = vt_ref[0, 0, j]
            for cols in chunks:
                sc = sc_ref[slot, :, cols]
                if diagonal:
                    key_i = lax.broadcasted_iota(jnp.int32, (t, _MXU_DIM), 0)
                    qry_i = cols.start + lax.broadcasted_iota(jnp.int32, (t, _MXU_DIM), 1)
                    sc = jnp.where(key_i <= qry_i, sc, -jnp.inf)
                p, alpha = _softmax_step(sc, m_ref, cols)
                if not diagonal:
                    logits_chunk(j + 1, 1 - slot, cols)
                _pv_step(vta, p, alpha, acc_ref, cols)

        def off_diagonal(j, carry):
            softmax_pv(j, j & 1, False)
            return carry

        logits(0, 0)
        lax.fori_loop(0, i, off_diagonal, 0)
        softmax_pv(i, i & 1, True)
        acc = acc_ref[...]
        ot = acc[:_HEAD_DIM] * (1.0 / acc[_HEAD_DIM:_HEAD_DIM + 1])
        for cq in range(t // _LANES):
            o_ref[0, cq * _LANES:(cq + 1) * _LANES, :] = (
                ot[:, cq * _LANES:(cq + 1) * _LANES].T.astype(o_ref.dtype))

    head = pl.BlockSpec((1, 1, s, _HEAD_DIM), lambda bi, h, i: (bi, h, 0, 0))
    return pl.pallas_call(
        body,
        grid=(b, n_heads, s // t),
        in_specs=[pl.BlockSpec((1, 1, t, _HEAD_DIM), lambda bi, h, i: (bi, h, i, 0)),
                  head, head,
                  pl.BlockSpec((1, 1, s // t, _VT_ROWS, t), lambda bi, h, i: (bi, h, 0, 0, 0))],
        out_specs=pl.BlockSpec((1, t, _HEAD_DIM), lambda bi, h, i: (bi, i, h)),
        out_shape=jax.ShapeDtypeStruct((b, s, n_heads * _HEAD_DIM), _CDT),
        scratch_shapes=[pltpu.VMEM((s, 2 * _HEAD_DIM), _CDT),
                        pltpu.VMEM((2, t, t), jnp.float32),
                        pltpu.VMEM((1, t), jnp.float32),
                        pltpu.VMEM((_VT_ROWS, t), jnp.float32)],
        compiler_params=_params("arbitrary", "arbitrary", "arbitrary"),
        name="fox_core",
    )(q, k, caug, vt)


def _rope_tables(s):
    half = _HEAD_DIM // 2
    inv = _ROPE_THETA ** (-jnp.arange(half, dtype=jnp.float32) / half)
    ang = jnp.arange(s, dtype=jnp.float32)[:, None] * inv[None, :]
    cos, sin = jnp.cos(ang), jnp.sin(ang)
    return jnp.concatenate([cos, cos], axis=1), jnp.concatenate([-sin, sin], axis=1)


def _tiles(s, n):
    return min(512, s), min(1024, n)


def _heads_call(a, w, b, s, *, rope, scale, cos, sin, name):
    n = w.shape[1]
    tm, tn = _tiles(s, n)
    nsb = s // tm
    extras, extra_specs = (), ()
    if rope:
        extras = (cos, sin)
        extra_specs = (pl.BlockSpec((tm, _HEAD_DIM), lambda j, i: (i % nsb, 0)),) * 2
    nh_t = tn // _HEAD_DIM
    return _proj(
        a, w, functools.partial(_ep_heads, rope=rope, scale=scale), extras, extra_specs,
        jax.ShapeDtypeStruct((b, n // _HEAD_DIM, s, _HEAD_DIM), _CDT),
        pl.BlockSpec((1, nh_t, tm, _HEAD_DIM), lambda j, i: (i // nsb, j, i % nsb, 0)),
        tm, tn, name)


def _vt_call(a, w, b, s, t, name):
    n = w.shape[1]
    tm, tn = _tiles(s, n)
    nsb = s // tm
    nh_t = tn // _HEAD_DIM
    return _proj(
        a, w, functools.partial(_ep_vt, t=t), (), (),
        jax.ShapeDtypeStruct((b, n // _HEAD_DIM, s // t, _VT_ROWS, t), _CDT),
        pl.BlockSpec((1, nh_t, tm // t, _VT_ROWS, t), lambda j, i: (i // nsb, j, i % nsb, 0, 0)),
        tm, tn, name)


def _gate_call(a, w, b, s, groups, name):
    n = w.shape[1]
    c = n // groups
    tm, tn = _tiles(s, c)
    nsb = s // tm
    per = c // tn
    return _proj(
        a, w, _ep_silu, (), (),
        jax.ShapeDtypeStruct((b, groups, s, c), _CDT),
        pl.BlockSpec((1, 1, tm, tn), lambda j, i: (i // nsb, j // per, i % nsb, j % per)),
        tm, tn, name)


def _dsa_layer(xc, w_in, q_norm_g, w_uq, kidx_g, kidx_b, w_out, b, s):
    d = xc.shape[1]
    q_lora = w_uq.shape[0]
    branch = w_out.shape[0]
    idx_heads = (w_uq.shape[1] - branch) // _HEAD_DIM
    kv = (w_in.shape[1] - q_lora - _HEAD_DIM - idx_heads - branch) // 2
    n_kv = kv // _HEAD_DIM
    n_heads = branch // _HEAD_DIM
    assert n_heads == n_kv * _GROUP and idx_heads <= _LANES
    topk = min(_TOPK_MAX, s // 4)
    o0 = q_lora
    o1 = o0 + 2 * kv
    o2 = o1 + _HEAD_DIM
    o3 = o2 + idx_heads
    w_cq = w_in[:, :o0].astype(_CDT)
    w_kv = w_in[:, o0:o1].astype(_CDT)
    w_ki = jnp.pad(w_in[:, o1:o3], ((0, 0), (0, _LANES - idx_heads))).astype(_CDT)
    w_gate = w_in[:, o3:].astype(_CDT)
    w_q = w_uq[:, :branch].astype(_CDT)
    w_qi = w_uq[:, branch:].astype(_CDT)
    cos, sin = _rope_tables(s)
    tm, _ = _tiles(s, d)
    nsb = s // tm
    tk1 = min(256, s)
    tk = min(512, s)
    rope_specs = (pl.BlockSpec((tm, _HEAD_DIM), lambda j, i: (i % nsb, 0)),) * 2

    cq = _proj(xc, w_cq, _ep_rms, (q_norm_g.reshape(1, q_lora),),
               (pl.BlockSpec((1, q_lora), lambda j, i: (0, 0)),),
               jax.ShapeDtypeStruct((b * s, q_lora), _CDT),
               pl.BlockSpec((tm, q_lora), lambda j, i: (i, 0)), tm, q_lora, "dsa_cq")
    k, vt = _proj(xc, w_kv, functools.partial(_ep_kv, n_kv=n_kv, t=tk), (cos, sin), rope_specs,
                  [jax.ShapeDtypeStruct((b, n_kv, s, _HEAD_DIM), _CDT),
                   jax.ShapeDtypeStruct((b, n_kv, s // tk, _VT_ROWS, tk), _CDT)],
                  [pl.BlockSpec((1, n_kv, tm, _HEAD_DIM), lambda j, i: (i // nsb, 0, i % nsb, 0)),
                   pl.BlockSpec((1, n_kv, tm // tk, _VT_ROWS, tk),
                                lambda j, i: (i // nsb, 0, i % nsb, 0, 0))],
                  tm, 2 * kv, "dsa_kv")
    w_scale = idx_heads ** -0.5 * _HEAD_DIM ** -0.5
    vec = pl.BlockSpec((1, _HEAD_DIM), lambda j, i: (0, 0))
    kidx, wt = _proj(xc, w_ki, functools.partial(_ep_kidx, idx_heads=idx_heads, w_scale=w_scale),
                     (cos, sin, kidx_g.reshape(1, _HEAD_DIM), kidx_b.reshape(1, _HEAD_DIM)),
                     rope_specs + (vec, vec),
                     [jax.ShapeDtypeStruct((b, s, _HEAD_DIM), _CDT),
                      jax.ShapeDtypeStruct((b, idx_heads, s), jnp.float32)],
                     [pl.BlockSpec((1, tm, _HEAD_DIM), lambda j, i: (i // nsb, i % nsb, 0)),
                      pl.BlockSpec((1, idx_heads, tm), lambda j, i: (i // nsb, 0, i % nsb))],
                     tm, 2 * _LANES, "dsa_kidx")
    gate = _gate_call(xc, w_gate, b, s, n_kv, "dsa_gate")
    q = _heads_call(cq, w_q, b, s, rope=True, scale=_HEAD_DIM ** -0.5 * _LOG2E, cos=cos, sin=sin,
                    name="dsa_q")
    qi = _heads_call(cq, w_qi, b, s, rope=True, scale=1.0, cos=cos, sin=sin, name="dsa_qi")
    o = _dsa_core(q, qi, wt, kidx, k, vt, topk=topk, tk1=tk1, tk=tk, hchunk=min(8, idx_heads))
    return _outproj(o, gate, w_out.astype(_CDT), tm, min(1024, d), "dsa_out")


def _fox_layer(xc, w_in, forget_bias, w_out, b, s):
    d = xc.shape[1]
    branch = w_out.shape[0]
    n_heads = forget_bias.shape[0]
    assert branch == n_heads * _HEAD_DIM and n_heads <= _LANES
    w_q = w_in[:, :branch].astype(_CDT)
    w_k = w_in[:, branch:2 * branch].astype(_CDT)
    w_v = w_in[:, 2 * branch:3 * branch].astype(_CDT)
    w_gate = w_in[:, 3 * branch:4 * branch].astype(_CDT)
    w_f = jnp.pad(w_in[:, 4 * branch:], ((0, 0), (0, _LANES - n_heads))).astype(_CDT)
    fb = jnp.pad(forget_bias, (0, _LANES - n_heads)).reshape(1, _LANES)
    tm, _ = _tiles(s, d)
    nsb = s // tm
    t = min(512, s)
    q = _heads_call(xc, w_q, b, s, rope=False, scale=_HEAD_DIM ** -0.5 * _LOG2E, cos=None, sin=None,
                    name="fox_q")
    k = _heads_call(xc, w_k, b, s, rope=False, scale=1.0, cos=None, sin=None, name="fox_k")
    vt = _vt_call(xc, w_v, b, s, t, "fox_v")
    gate = _gate_call(xc, w_gate, b, s, 1, "fox_gate")
    lf = _proj(xc, w_f, _ep_logf, (fb,), (pl.BlockSpec((1, _LANES), lambda j, i: (0, 0)),),
               jax.ShapeDtypeStruct((b, s, _LANES), jnp.float32),
               pl.BlockSpec((1, tm, _LANES), lambda j, i: (i // nsb, i % nsb, 0)),
               tm, _LANES, "fox_logf")
    caug = _cumsum_aug(lf, n_heads)
    o = _fox_core(q, k, caug, vt, t=t)
    return _outproj(o.reshape(b, 1, s, branch), gate, w_out.astype(_CDT), tm, min(1024, d), "fox_out")


def kernel(x, a_w_in, a_q_norm_g, a_w_uq, a_kidx_norm_g, a_kidx_norm_b, a_w_out,
           b_w_in, b_forget_bias, b_w_out, ln_g, ln_b):
    b, s, d = x.shape
    depth = ln_g.shape[0]
    alpha = (2 * depth) ** 0.25
    xf = x.reshape(b * s, d)
    xc = xf.astype(_CDT)
    tm_ln = min(256, s)
    for i in range(depth):
        j = i // 2
        if i % 2 == 0:
            h = _dsa_layer(xc, a_w_in[j], a_q_norm_g[j], a_w_uq[j], a_kidx_norm_g[j],
                           a_kidx_norm_b[j], a_w_out[j], b, s)
        else:
            h = _fox_layer(xc, b_w_in[j], b_forget_bias[j], b_w_out[j], b, s)
        last = i == depth - 1
        outs = _resid_ln(h, xf, ln_g[i], ln_b[i], alpha, tm_ln, not last, "resid_ln_%d" % i)
        xf = outs[0]
        if not last:
            xc = outs[1]
    return xf.reshape(b, s, d)
```

```python
import functools
import math

import jax
import jax.numpy as jnp
from jax import lax
from jax.experimental import pallas as pl
from jax.experimental.pallas import tpu as pltpu

_CDT = jnp.bfloat16
_HEAD_DIM = 128
_GROUP = 8
_TOPK_MAX = 256
_ROPE_THETA = 10000.0
_LN_EPS = 1e-5
_RMS_EPS = 1e-6
_LANES = 128
_SUBLANES = 8
_MXU_DIM = 256
_VMEM_LIMIT = 56 * 1024 * 1024
_VT_ROWS = _HEAD_DIM + 16
_INT_MIN = -2 ** 31
_MASKED = -1e30
_M_INIT = -1e29
_LOG2E = math.log2(math.e)

_NT = (((1,), (1,)), ((), ()))


def _params(*sem):
    return pltpu.CompilerParams(dimension_semantics=sem, vmem_limit_bytes=_VMEM_LIMIT)


def _rope_tile(x, cos, sin_signed):
    return x * cos + pltpu.roll(x, _HEAD_DIM // 2, axis=1) * sin_signed


def _proj(a, w, epilogue, extras, extra_specs, out_shape, out_specs, tm, tn, name):
    m, k = a.shape
    n = w.shape[1]
    n_extra = len(extras)

    def body(a_ref, w_ref, *rest):
        res = jnp.dot(a_ref[...], w_ref[...], preferred_element_type=jnp.float32)
        epilogue(res, rest[:n_extra], rest[n_extra:])

    return pl.pallas_call(
        body,
        grid=(n // tn, m // tm),
        in_specs=[pl.BlockSpec((tm, k), lambda j, i: (i, 0)),
                  pl.BlockSpec((k, tn), lambda j, i: (0, j))] + list(extra_specs),
        out_specs=out_specs,
        out_shape=out_shape,
        compiler_params=_params("arbitrary", "arbitrary"),
        name=name,
    )(a, w, *extras)


def _ep_rms(res, extras, outs):
    (g_ref,) = extras
    (o_ref,) = outs
    y = res * lax.rsqrt(jnp.mean(res * res, axis=-1, keepdims=True) + _RMS_EPS)
    o_ref[...] = (y * g_ref[...]).astype(o_ref.dtype)


def _ep_heads(res, extras, outs, *, rope, scale):
    (o_ref,) = outs
    nh = res.shape[1] // _HEAD_DIM
    if rope:
        cos = extras[0][...]
        sin = extras[1][...]
    for h in range(nh):
        xh = res[:, h * _HEAD_DIM:(h + 1) * _HEAD_DIM]
        if rope:
            xh = _rope_tile(xh, cos, sin)
        if scale != 1.0:
            xh = xh * scale
        o_ref[0, h] = xh.astype(o_ref.dtype)


def _store_vt(vt_ref, h, vh, t):
    for c in range(vh.shape[0] // t):
        vt_ref[0, h, c, :_HEAD_DIM, :] = vh[c * t:(c + 1) * t, :].T.astype(vt_ref.dtype)
        vt_ref[0, h, c, _HEAD_DIM:, :] = jnp.ones((_VT_ROWS - _HEAD_DIM, t), vt_ref.dtype)


def _ep_vt(res, extras, outs, *, t):
    (vt_ref,) = outs
    for h in range(res.shape[1] // _HEAD_DIM):
        _store_vt(vt_ref, h, res[:, h * _HEAD_DIM:(h + 1) * _HEAD_DIM], t)


def _ep_kv(res, extras, outs, *, n_kv, t):
    cos = extras[0][...]
    sin = extras[1][...]
    k_ref, vt_ref = outs
    for h in range(n_kv):
        xh = res[:, h * _HEAD_DIM:(h + 1) * _HEAD_DIM]
        k_ref[0, h] = _rope_tile(xh, cos, sin).astype(k_ref.dtype)
    for h in range(n_kv):
        _store_vt(vt_ref, h, res[:, (n_kv + h) * _HEAD_DIM:(n_kv + h + 1) * _HEAD_DIM], t)


def _ep_kidx(res, extras, outs, *, idx_heads, w_scale):
    cos_ref, sin_ref, g_ref, b_ref = extras
    kidx_ref, wt_ref = outs
    x = res[:, :_HEAD_DIM]
    mu = jnp.mean(x, axis=-1, keepdims=True)
    xc = x - mu
    var = jnp.mean(xc * xc, axis=-1, keepdims=True)
    y = xc * lax.rsqrt(var + _LN_EPS) * g_ref[...] + b_ref[...]
    kidx_ref[0] = _rope_tile(y, cos_ref[...], sin_ref[...]).astype(kidx_ref.dtype)
    wt = (res[:, _HEAD_DIM:] * w_scale).T
    wt_ref[0] = wt[:idx_heads, :]


def _ep_silu(res, extras, outs):
    (o_ref,) = outs
    o_ref[0, 0] = (res * jax.nn.sigmoid(res)).astype(o_ref.dtype)


def _ep_logf(res, extras, outs):
    (bias_ref,) = extras
    (o_ref,) = outs
    z = res + bias_ref[...]
    o_ref[0] = jnp.minimum(z, 0.0) - jnp.log(1.0 + jnp.exp(-jnp.abs(z)))


def _outproj(o, gate, w, tm, tn, name):
    b, g, s, c = o.shape
    d = w.shape[1]
    nsb = s // tm

    def body(o_ref, g_ref, w_ref, out_ref):
        acc = None
        for gi in range(g):
            a = (o_ref[0, gi].astype(jnp.float32) * g_ref[0, gi].astype(jnp.float32)).astype(_CDT)
            part = jnp.dot(a, w_ref[gi * c:(gi + 1) * c, :], preferred_element_type=jnp.float32)
            acc = part if acc is None else acc + part
        out_ref[...] = acc

    return pl.pallas_call(
        body,
        grid=(d // tn, b * nsb),
        in_specs=[pl.BlockSpec((1, g, tm, c), lambda j, i: (i // nsb, 0, i % nsb, 0)),
                  pl.BlockSpec((1, g, tm, c), lambda j, i: (i // nsb, 0, i % nsb, 0)),
                  pl.BlockSpec((g * c, tn), lambda j, i: (0, j))],
        out_specs=pl.BlockSpec((tm, tn), lambda j, i: (i, j)),
        out_shape=jax.ShapeDtypeStruct((b * s, d), jnp.float32),
        compiler_params=_params("arbitrary", "arbitrary"),
        name=name,
    )(o, gate, w)


def _resid_ln(h, x, g, b, alpha, tm, with_cdt, name):
    m, d = x.shape

    def body(h_ref, x_ref, g_ref, b_ref, *outs):
        y = alpha * x_ref[...] + h_ref[...]
        mu = jnp.mean(y, axis=-1, keepdims=True)
        yc = y - mu
        var = jnp.mean(yc * yc, axis=-1, keepdims=True)
        z = yc * lax.rsqrt(var + _LN_EPS) * g_ref[...] + b_ref[...]
        outs[0][...] = z
        if with_cdt:
            outs[1][...] = z.astype(_CDT)

    row = pl.BlockSpec((tm, d), lambda i: (i, 0))
    vec = pl.BlockSpec((1, d), lambda i: (0, 0))
    out_shape = [jax.ShapeDtypeStruct((m, d), jnp.float32)]
    out_specs = [row]
    if with_cdt:
        out_shape.append(jax.ShapeDtypeStruct((m, d), _CDT))
        out_specs.append(row)
    return pl.pallas_call(
        body,
        grid=(m // tm,),
        in_specs=[row, row, vec, vec],
        out_specs=out_specs,
        out_shape=out_shape,
        compiler_params=_params("arbitrary"),
        name=name,
    )(h, x, g.reshape(1, d), b.reshape(1, d))


def _col_reduce(x, op):
    rows, n = x.shape
    fold = 8 * _SUBLANES
    if rows > fold and rows % fold == 0:
        x = op(x.reshape(rows // fold, fold, n), axis=0)
    return op(x, axis=0, keepdims=True)


def _softmax_step(sc, m_ref, cols):
    m_old = m_ref[:, cols]
    m_new = jnp.maximum(m_old, _col_reduce(sc, jnp.max))
    m_ref[:, cols] = m_new
    return jnp.exp2(sc - m_new).astype(_CDT), jnp.exp2(m_old - m_new)


def _pv_step(vta, p, alpha, acc_ref, cols):
    acc_ref[:, cols] = alpha * acc_ref[:, cols] + jnp.dot(vta, p, preferred_element_type=jnp.float32)


def _float_sort_key(x):
    bits = pltpu.bitcast(x, jnp.int32)
    return bits ^ ((bits >> 31) & jnp.int32(0x7FFFFFFF))


def _dsa_core(q, qi, wt, kidx, k, vt, *, topk, tk1, tk, hchunk):
    b, n_heads, s, _ = q.shape
    idx_heads = qi.shape[1]
    n_kv = k.shape[1]
    tq = _LANES
    gq = _GROUP * tq
    log_s = int(math.log2(s))
    assert 1 << log_s == s and s % tk == 0 and tk % tk1 == 0 and tk1 % tq == 0
    assert idx_heads % hchunk == 0 and gq % _MXU_DIM == 0

    def body(q_ref, qi_ref, wt_ref, kidx_ref, k_ref, vt_ref, o_ref,
             key_ref, mask_ref, sc_ref, p_ref, al_ref, m_ref, acc_ref):
        i = pl.program_id(1)
        n1 = (i * tq + tq + tk1 - 1) // tk1
        n3 = (i * tq + tq + tk - 1) // tk

        q_pos1 = i * tq + lax.broadcasted_iota(jnp.int32, (tk1, tq), 1)
        s_iota1 = lax.broadcasted_iota(jnp.int32, (tk1, tq), 0)

        def score_tile(kt, carry):
            ks = pl.multiple_of(kt * tk1, tk1)
            kx = kidx_ref[0, pl.ds(ks, tk1), :]
            acc = jnp.zeros((tk1, tq), jnp.float32)
            for hc in range(idx_heads // hchunk):
                qq = qi_ref[0, hc * hchunk:(hc + 1) * hchunk].reshape(hchunk * tq, _HEAD_DIM)
                d = lax.dot_general(kx, qq, _NT, preferred_element_type=jnp.float32)
                for hh in range(hchunk):
                    h = hc * hchunk + hh
                    acc = acc + jnp.maximum(d[:, hh * tq:(hh + 1) * tq], 0.0) * wt_ref[0, h:h + 1, :]
            key = _float_sort_key(acc + 0.0)
            key_ref[pl.ds(ks, tk1), :] = jnp.where(ks + s_iota1 <= q_pos1, key, _INT_MIN)
            return carry

        lax.fori_loop(0, n1, score_tile, 0)

        def pad_tile(kt, carry):
            key_ref[pl.ds(pl.multiple_of(kt * tk1, tk1), tk1), :] = jnp.full((tk1, tq), _INT_MIN, jnp.int32)
            return carry

        lax.fori_loop(n1, n3 * (tk // tk1), pad_tile, 0)

        q_pos = i * tq + lax.broadcasted_iota(jnp.int32, (tk, tq), 1)
        s_iota = lax.broadcasted_iota(jnp.int32, (tk, tq), 0)

        fold = 8 * _SUBLANES

        def count(pred_fn):
            def tile(kt, cnt):
                ks = pl.multiple_of(kt * tk, tk)
                c = jnp.where(pred_fn(key_ref[pl.ds(ks, tk), :], ks), 1.0, 0.0)
                return cnt + jnp.sum(c.reshape(tk // fold, fold, tq), axis=0)
            cnt = lax.fori_loop(0, n3, tile, jnp.zeros((fold, tq), jnp.float32))
            return jnp.sum(cnt, axis=0, keepdims=True)

        def value_step(t, prefix):
            cand = prefix | jnp.left_shift(jnp.int32(1), 31 - t)
            cand_s = cand ^ _INT_MIN
            cnt = count(lambda key, ks: key >= cand_s)
            return jnp.where(cnt >= topk, cand, prefix)

        prefix = lax.fori_loop(0, 32, value_step, jnp.zeros((1, tq), jnp.int32))
        thr = prefix ^ _INT_MIN

        def tie_break():
            need = topk - count(lambda key, ks: key > thr)

            def index_step(t, j):
                cand = j | jnp.left_shift(jnp.int32(1), log_s - 1 - t)
                cnt = count(lambda key, ks: (key == thr) & (ks + s_iota < cand))
                return jnp.where(cnt < need, cand, j)

            return lax.fori_loop(0, log_s, index_step, jnp.zeros((1, tq), jnp.int32))

        n_ge = count(lambda key, ks: key >= thr)
        j_last = lax.cond(jnp.max(n_ge) > topk, tie_break,
                          lambda: jnp.full((1, tq), s, jnp.int32))

        def mask_tile(kt, carry):
            ks = pl.multiple_of(kt * tk, tk)
            key = key_ref[pl.ds(ks, tk), :]
            s_pos = ks + s_iota
            sel = (key > thr) | ((key == thr) & (s_pos <= j_last))
            mask_ref[pl.ds(ks, tk), :] = jnp.where(sel & (s_pos <= q_pos), 0.0, _MASKED).astype(_CDT)
            return carry

        lax.fori_loop(0, n3, mask_tile, 0)

        r = lax.broadcasted_iota(jnp.int32, (gq, tq), 0)
        c = lax.broadcasted_iota(jnp.int32, (gq, tq), 1)
        one_hot = jnp.where((r & (tq - 1)) == c, 1.0, 0.0).astype(_CDT)

        chunks = [slice(cc * _MXU_DIM, (cc + 1) * _MXU_DIM) for cc in range(gq // _MXU_DIM)]

        def kv_group(g, carry):
            qg = q_ref[0, pl.ds(g * _GROUP, _GROUP)].reshape(gq, _HEAD_DIM)
            qaug = jnp.concatenate([qg, one_hot], axis=1)
            m_ref[...] = jnp.full((1, gq), _M_INIT, jnp.float32)
            acc_ref[...] = jnp.zeros((_VT_ROWS, gq), jnp.float32)

            def logits(kt):
                kaug = jnp.concatenate([k_ref[0, g, pl.ds(kt * tk, tk), :],
                                        mask_ref[pl.ds(kt * tk, tk), :]], axis=1)
                for cols in chunks:
                    sc_ref[kt & 1, :, cols] = lax.dot_general(kaug, qaug[cols], _NT,
                                                              preferred_element_type=jnp.float32)

            def softmax(kt):
                for cols in chunks:
                    p_ref[kt & 1, :, cols], al_ref[kt & 1, :, cols] = _softmax_step(
                        sc_ref[kt & 1, :, cols], m_ref, cols)

            def pv(kt):
                vta = vt_ref[0, g, kt]
                for cols in chunks:
                    _pv_step(vta, p_ref[kt & 1, :, cols], al_ref[kt & 1, :, cols], acc_ref, cols)

            for n_tiles in range(1, s // tk + 1):
                @pl.when(n3 == n_tiles)
                def _():
                    for step in range(n_tiles + 2):
                        if step < n_tiles:
                            logits(step)
                        if 1 <= step <= n_tiles:
                            softmax(step - 1)
                        if step >= 2:
                            pv(step - 2)

            acc = acc_ref[...]
            ot = acc[:_HEAD_DIM] * (1.0 / acc[_HEAD_DIM:_HEAD_DIM + 1])
            for hh in range(_GROUP):
                o_ref[0, g, :, hh * _HEAD_DIM:(hh + 1) * _HEAD_DIM] = (
                    ot[:, hh * tq:(hh + 1) * tq].T.astype(o_ref.dtype))
            return carry

        lax.fori_loop(0, n_kv, kv_group, 0)

    nq = s // tq
    return pl.pallas_call(
        body,
        grid=(b, nq),
        in_specs=[pl.BlockSpec((1, n_heads, tq, _HEAD_DIM), lambda bi, i: (bi, 0, i, 0)),
                  pl.BlockSpec((1, idx_heads, tq, _HEAD_DIM), lambda bi, i: (bi, 0, i, 0)),
                  pl.BlockSpec((1, idx_heads, tq), lambda bi, i: (bi, 0, i)),
                  pl.BlockSpec((1, s, _HEAD_DIM), lambda bi, i: (bi, 0, 0)),
                  pl.BlockSpec((1, n_kv, s, _HEAD_DIM), lambda bi, i: (bi, 0, 0, 0)),
                  pl.BlockSpec((1, n_kv, s // tk, _VT_ROWS, tk), lambda bi, i: (bi, 0, 0, 0, 0))],
        out_specs=pl.BlockSpec((1, n_kv, tq, _GROUP * _HEAD_DIM), lambda bi, i: (bi, 0, i, 0)),
        out_shape=jax.ShapeDtypeStruct((b, n_kv, s, _GROUP * _HEAD_DIM), _CDT),
        scratch_shapes=[pltpu.VMEM((s, tq), jnp.int32),
                        pltpu.VMEM((s, tq), _CDT),
                        pltpu.VMEM((2, tk, gq), jnp.float32),
                        pltpu.VMEM((2, tk, gq), _CDT),
                        pltpu.VMEM((2, 1, gq), jnp.float32),
                        pltpu.VMEM((1, gq), jnp.float32),
                        pltpu.VMEM((_VT_ROWS, gq), jnp.float32)],
        compiler_params=_params("arbitrary", "arbitrary"),
        name="dsa_core",
    )(q, qi, wt, kidx, k, vt)


def _cumsum_aug(lf, n_heads):
    b, s, w = lf.shape
    blk = _LANES

    def body(lf_ref, out_ref, carry_ref):
        @pl.when(pl.program_id(1) == 0)
        def _():
            carry_ref[...] = jnp.zeros_like(carry_ref)
        r = lax.broadcasted_iota(jnp.int32, (blk, blk), 0)
        c = lax.broadcasted_iota(jnp.int32, (blk, blk), 1)
        tri = jnp.where(r >= c, 1.0, 0.0).astype(jnp.float32)
        cs = jnp.dot(tri, lf_ref[0], preferred_element_type=jnp.float32,
                     precision=lax.Precision.HIGHEST) + carry_ref[...]
        carry_ref[...] = cs[blk - 1:blk, :]
        c2 = cs * _LOG2E
        for h in range(n_heads):
            col = c2[:, h:h + 1]
            hi = col.astype(_CDT).astype(jnp.float32)
            mid = (col - hi).astype(_CDT).astype(jnp.float32)
            lo = (col - hi - mid).astype(_CDT).astype(jnp.float32)
            tile = jnp.where(c == 0, hi, jnp.where(c == 1, mid, jnp.where(c == 2, lo, 0.0)))
            out_ref[0, h] = tile.astype(out_ref.dtype)

    return pl.pallas_call(
        body,
        grid=(b, s // blk),
        in_specs=[pl.BlockSpec((1, blk, w), lambda bi, i: (bi, i, 0))],
        out_specs=pl.BlockSpec((1, n_heads, blk, _LANES), lambda bi, i: (bi, 0, i, 0)),
        out_shape=jax.ShapeDtypeStruct((b, n_heads, s, _LANES), _CDT),
        scratch_shapes=[pltpu.VMEM((1, w), jnp.float32)],
        compiler_params=_params("arbitrary", "arbitrary"),
        name="fox_cumsum",
    )(lf)


def _fox_core(q, k, caug, vt, *, t):
    b, n_heads, s, _ = q.shape
    assert t % _MXU_DIM == 0 and s % t == 0

    chunks = [slice(cc * _MXU_DIM, (cc + 1) * _MXU_DIM) for cc in range(t // _MXU_DIM)]

    def body(q_ref, k_ref, c_ref, vt_ref, o_ref, kaug_ref, sc_ref, p_ref, al_ref, m_ref, acc_ref):
        i = pl.program_id(2)

        @pl.when(i == 0)
        def _():
            kaug_ref[:, :_HEAD_DIM] = k_ref[0, 0]
            kaug_ref[:, _HEAD_DIM:] = c_ref[0, 0]

        lane = lax.broadcasted_iota(jnp.int32, (t, _HEAD_DIM), 1)
        qaug = jnp.concatenate([q_ref[0, 0], jnp.where(lane < 3, -1.0, 0.0).astype(_CDT)], axis=1)
        m_ref[...] = jnp.full((1, t), _M_INIT, jnp.float32)
        acc_ref[...] = jnp.zeros((_VT_ROWS, t), jnp.float32)

        def logits(j):
            ka = kaug_ref[pl.ds(j * t, t), :]
            for cols in chunks:
                sc_ref[j & 1, :, cols] = lax.dot_general(ka, qaug[cols], _NT,
                                                         preferred_element_type=jnp.float32)

        def softmax(j, diagonal):
            for cols in chunks:
                sc = sc_ref[j & 1, :, cols]
                if diagonal:
                    key_i = lax.broadcasted_iota(jnp.int32, (t, _MXU_DIM), 0)
                    qry_i = cols.start + lax.broadcasted_iota(jnp.int32, (t, _MXU_DIM), 1)
                    sc = jnp.where(key_i <= qry_i, sc, -jnp.inf)
                p_ref[j & 1, :, cols], al_ref[j & 1, :, cols] = _softmax_step(sc, m_ref, cols)

        def pv(j):
            vta = vt_ref[0, 0, j]
            for cols in chunks:
                _pv_step(vta, p_ref[j & 1, :, cols], al_ref[j & 1, :, cols], acc_ref, cols)

        for n_off in range(s // t):
            @pl.when(i == n_off)
            def _():
                for step in range(n_off + 3):
                    if step <= n_off:
                        logits(step)
                    if 1 <= step <= n_off + 1:
                        softmax(step - 1, step - 1 == n_off)
                    if step >= 2:
                        pv(step - 2)

        acc = acc_ref[...]
        ot = acc[:_HEAD_DIM] * (1.0 / acc[_HEAD_DIM:_HEAD_DIM + 1])
        for cq in range(t // _LANES):
            o_ref[0, cq * _LANES:(cq + 1) * _LANES, :] = (
                ot[:, cq * _LANES:(cq + 1) * _LANES].T.astype(o_ref.dtype))

    head = pl.BlockSpec((1, 1, s, _HEAD_DIM), lambda bi, h, i: (bi, h, 0, 0))
    return pl.pallas_call(
        body,
        grid=(b, n_heads, s // t),
        in_specs=[pl.BlockSpec((1, 1, t, _HEAD_DIM), lambda bi, h, i: (bi, h, i, 0)),
                  head, head,
                  pl.BlockSpec((1, 1, s // t, _VT_ROWS, t), lambda bi, h, i: (bi, h, 0, 0, 0))],
        out_specs=pl.BlockSpec((1, t, _HEAD_DIM), lambda bi, h, i: (bi, i, h)),
        out_shape=jax.ShapeDtypeStruct((b, s, n_heads * _HEAD_DIM), _CDT),
        scratch_shapes=[pltpu.VMEM((s, 2 * _HEAD_DIM), _CDT),
                        pltpu.VMEM((2, t, t), jnp.float32),
                        pltpu.VMEM((2, t, t), _CDT),
                        pltpu.VMEM((2, 1, t), jnp.float32),
                        pltpu.VMEM((1, t), jnp.float32),
                        pltpu.VMEM((_VT_ROWS, t), jnp.float32)],
        compiler_params=_params("arbitrary", "arbitrary", "arbitrary"),
        name="fox_core",
    )(q, k, caug, vt)


def _rope_tables(s):
    half = _HEAD_DIM // 2
    inv = _ROPE_THETA ** (-jnp.arange(half, dtype=jnp.float32) / half)
    ang = jnp.arange(s, dtype=jnp.float32)[:, None] * inv[None, :]
    cos, sin = jnp.cos(ang), jnp.sin(ang)
    return jnp.concatenate([cos, cos], axis=1), jnp.concatenate([-sin, sin], axis=1)


def _tiles(s, n):
    return min(512, s), min(1024, n)


def _heads_call(a, w, b, s, *, rope, scale, cos, sin, name):
    n = w.shape[1]
    tm, tn = _tiles(s, n)
    nsb = s // tm
    extras, extra_specs = (), ()
    if rope:
        extras = (cos, sin)
        extra_specs = (pl.BlockSpec((tm, _HEAD_DIM), lambda j, i: (i % nsb, 0)),) * 2
    nh_t = tn // _HEAD_DIM
    return _proj(
        a, w, functools.partial(_ep_heads, rope=rope, scale=scale), extras, extra_specs,
        jax.ShapeDtypeStruct((b, n // _HEAD_DIM, s, _HEAD_DIM), _CDT),
        pl.BlockSpec((1, nh_t, tm, _HEAD_DIM), lambda j, i: (i // nsb, j, i % nsb, 0)),
        tm, tn, name)


def _vt_call(a, w, b, s, t, name):
    n = w.shape[1]
    tm, tn = _tiles(s, n)
    nsb = s // tm
    nh_t = tn // _HEAD_DIM
    return _proj(
        a, w, functools.partial(_ep_vt, t=t), (), (),
        jax.ShapeDtypeStruct((b, n // _HEAD_DIM, s // t, _VT_ROWS, t), _CDT),
        pl.BlockSpec((1, nh_t, tm // t, _VT_ROWS, t), lambda j, i: (i // nsb, j, i % nsb, 0, 0)),
        tm, tn, name)


def _gate_call(a, w, b, s, groups, name):
    n = w.shape[1]
    c = n // groups
    tm, tn = _tiles(s, c)
    nsb = s // tm
    per = c // tn
    return _proj(
        a, w, _ep_silu, (), (),
        jax.ShapeDtypeStruct((b, groups, s, c), _CDT),
        pl.BlockSpec((1, 1, tm, tn), lambda j, i: (i // nsb, j // per, i % nsb, j % per)),
        tm, tn, name)


def _dsa_layer(xc, w_in, q_norm_g, w_uq, kidx_g, kidx_b, w_out, b, s):
    d = xc.shape[1]
    q_lora = w_uq.shape[0]
    branch = w_out.shape[0]
    idx_heads = (w_uq.shape[1] - branch) // _HEAD_DIM
    kv = (w_in.shape[1] - q_lora - _HEAD_DIM - idx_heads - branch) // 2
    n_kv = kv // _HEAD_DIM
    n_heads = branch // _HEAD_DIM
    assert n_heads == n_kv * _GROUP and idx_heads <= _LANES
    topk = min(_TOPK_MAX, s // 4)
    o0 = q_lora
    o1 = o0 + 2 * kv
    o2 = o1 + _HEAD_DIM
    o3 = o2 + idx_heads
    w_cq = w_in[:, :o0].astype(_CDT)
    w_kv = w_in[:, o0:o1].astype(_CDT)
    w_ki = jnp.pad(w_in[:, o1:o3], ((0, 0), (0, _LANES - idx_heads))).astype(_CDT)
    w_gate = w_in[:, o3:].astype(_CDT)
    w_q = w_uq[:, :branch].astype(_CDT)
    w_qi = w_uq[:, branch:].astype(_CDT)
    cos, sin = _rope_tables(s)
    tm, _ = _tiles(s, d)
    nsb = s // tm
    tk1 = min(256, s)
    tk = min(512, s)
    rope_specs = (pl.BlockSpec((tm, _HEAD_DIM), lambda j, i: (i % nsb, 0)),) * 2

    cq = _proj(xc, w_cq, _ep_rms, (q_norm_g.reshape(1, q_lora),),
               (pl.BlockSpec((1, q_lora), lambda j, i: (0, 0)),),
               jax.ShapeDtypeStruct((b * s, q_lora), _CDT),
               pl.BlockSpec((tm, q_lora), lambda j, i: (i, 0)), tm, q_lora, "dsa_cq")
    k, vt = _proj(xc, w_kv, functools.partial(_ep_kv, n_kv=n_kv, t=tk), (cos, sin), rope_specs,
                  [jax.ShapeDtypeStruct((b, n_kv, s, _HEAD_DIM), _CDT),
                   jax.ShapeDtypeStruct((b, n_kv, s // tk, _VT_ROWS, tk), _CDT)],
                  [pl.BlockSpec((1, n_kv, tm, _HEAD_DIM), lambda j, i: (i // nsb, 0, i % nsb, 0)),
                   pl.BlockSpec((1, n_kv, tm // tk, _VT_ROWS, tk),
                                lambda j, i: (i // nsb, 0, i % nsb, 0, 0))],
                  tm, 2 * kv, "dsa_kv")
    w_scale = idx_heads ** -0.5 * _HEAD_DIM ** -0.5
    vec = pl.BlockSpec((1, _HEAD_DIM), lambda j, i: (0, 0))
    kidx, wt = _proj(xc, w_ki, functools.partial(_ep_kidx, idx_heads=idx_heads, w_scale=w_scale),
                     (cos, sin, kidx_g.reshape(1, _HEAD_DIM), kidx_b.reshape(1, _HEAD_DIM)),
                     rope_specs + (vec, vec),
                     [jax.ShapeDtypeStruct((b, s, _HEAD_DIM), _CDT),
                      jax.ShapeDtypeStruct((b, idx_heads, s), jnp.float32)],
                     [pl.BlockSpec((1, tm, _HEAD_DIM), lambda j, i: (i // nsb, i % nsb, 0)),
                      pl.BlockSpec((1, idx_heads, tm), lambda j, i: (i // nsb, 0, i % nsb))],
                     tm, 2 * _LANES, "dsa_kidx")
    gate = _gate_call(xc, w_gate, b, s, n_kv, "dsa_gate")
    q = _heads_call(cq, w_q, b, s, rope=True, scale=_HEAD_DIM ** -0.5 * _LOG2E, cos=cos, sin=sin,
                    name="dsa_q")
    qi = _heads_call(cq, w_qi, b, s, rope=True, scale=1.0, cos=cos, sin=sin, name="dsa_qi")
    o = _dsa_core(q, qi, wt, kidx, k, vt, topk=topk, tk1=tk1, tk=tk, hchunk=min(8, idx_heads))
    return _outproj(o, gate, w_out.astype(_CDT), tm, min(1024, d), "dsa_out")


def _fox_layer(xc, w_in, forget_bias, w_out, b, s):
    d = xc.shape[1]
    branch = w_out.shape[0]
    n_heads = forget_bias.shape[0]
    assert branch == n_heads * _HEAD_DIM and n_heads <= _LANES
    w_q = w_in[:, :branch].astype(_CDT)
    w_k = w_in[:, branch:2 * branch].astype(_CDT)
    w_v = w_in[:, 2 * branch:3 * branch].astype(_CDT)
    w_gate = w_in[:, 3 * branch:4 * branch].astype(_CDT)
    w_f = jnp.pad(w_in[:, 4 * branch:], ((0, 0), (0, _LANES - n_heads))).astype(_CDT)
    fb = jnp.pad(forget_bias, (0, _LANES - n_heads)).reshape(1, _LANES)
    tm, _ = _tiles(s, d)
    nsb = s // tm
    t = min(512, s)
    q = _heads_call(xc, w_q, b, s, rope=False, scale=_HEAD_DIM ** -0.5 * _LOG2E, cos=None, sin=None,
                    name="fox_q")
    k = _heads_call(xc, w_k, b, s, rope=False, scale=1.0, cos=None, sin=None, name="fox_k")
    vt = _vt_call(xc, w_v, b, s, t, "fox_v")
    gate = _gate_call(xc, w_gate, b, s, 1, "fox_gate")
    lf = _proj(xc, w_f, _ep_logf, (fb,), (pl.BlockSpec((1, _LANES), lambda j, i: (0, 0)),),
               jax.ShapeDtypeStruct((b, s, _LANES), jnp.float32),
               pl.BlockSpec((1, tm, _LANES), lambda j, i: (i // nsb, i % nsb, 0)),
               tm, _LANES, "fox_logf")
    caug = _cumsum_aug(lf, n_heads)
    o = _fox_core(q, k, caug, vt, t=t)
    return _outproj(o.reshape(b, 1, s, branch), gate, w_out.astype(_CDT), tm, min(1024, d), "fox_out")


def kernel(x, a_w_in, a_q_norm_g, a_w_uq, a_kidx_norm_g, a_kidx_norm_b, a_w_out,
           b_w_in, b_forget_bias, b_w_out, ln_g, ln_b):
    b, s, d = x.shape
    depth = ln_g.shape[0]
    alpha = (2 * depth) ** 0.25
    xf = x.reshape(b * s, d)
    xc = xf.astype(_CDT)
    tm_ln = min(256, s)
    for i in range(depth):
        j = i // 2
        if i % 2 == 0:
            h = _dsa_layer(xc, a_w_in[j], a_q_norm_g[j], a_w_uq[j], a_kidx_norm_g[j],
                           a_kidx_norm_b[j], a_w_out[j], b, s)
        else:
            h = _fox_layer(xc, b_w_in[j], b_forget_bias[j], b_w_out[j], b, s)
        last = i == depth - 1
        outs = _resid_ln(h, xf, ln_g[i], ln_b[i], alpha, tm_ln, not last, "resid_ln_%d" % i)
        xf = outs[0]
        if not last:
            xc = outs[1]
    return xf.reshape(b, s, d)
```

```python
import functools
import math

import jax
import jax.numpy as jnp
from jax import lax
from jax.experimental import pallas as pl
from jax.experimental.pallas import tpu as pltpu

_CDT = jnp.bfloat16
_HEAD_DIM = 128
_GROUP = 8
_TOPK_MAX = 256
_ROPE_THETA = 10000.0
_LN_EPS = 1e-5
_RMS_EPS = 1e-6
_LANES = 128
_SUBLANES = 8
_MXU_DIM = 256
_VMEM_LIMIT = 56 * 1024 * 1024
_VT_ROWS = _HEAD_DIM + 16
_INT_MIN = -2 ** 31
_MASKED = -1e30
_M_INIT = -1e29
_LOG2E = math.log2(math.e)

_NT = (((1,), (1,)), ((), ()))


def _params(*sem):
    return pltpu.CompilerParams(dimension_semantics=sem, vmem_limit_bytes=_VMEM_LIMIT)


def _rope_tile(x, cos, sin_signed):
    return x * cos + pltpu.roll(x, _HEAD_DIM // 2, axis=1) * sin_signed


class _Cols:
    def __init__(self, w, col0=0, n=None, layer=None):
        self.w, self.col0, self.layer = w, col0, layer
        self.n = w.shape[-1] - col0 if n is None else n
        self.k = w.shape[-2]

    def aligned(self, tn):
        return self.col0 % tn == 0 and self.col0 + -(-self.n // tn) * tn <= self.w.shape[-1]

    def materialize(self):
        w = self.w if self.layer is None else self.w[self.layer]
        return _Cols(w[:, self.col0:self.col0 + self.n].astype(_CDT))


def _proj_tiles(s, cols):
    tn = min(cols.n, 1024)
    while cols.k * tn * cols.w.dtype.itemsize > 8 * 1024 * 1024:
        tn //= 2
    return min(s, 1024 if tn <= 512 else 512), tn


def _proj(a, cols, epilogue, extras, extra_specs, out_shape, out_specs, tm, tn, name):
    m, k = a.shape
    assert cols.aligned(tn) and k == cols.k
    c0 = cols.col0 // tn
    convert = cols.w.dtype != _CDT
    n_extra = len(extras)

    def body(a_ref, w_ref, *rest):
        if convert:
            wc_ref = rest[-1]
            rest = rest[:-1]

            @pl.when(pl.program_id(1) == 0)
            def _():
                wc_ref[...] = w_ref[...].astype(_CDT)

            w_ref = wc_ref
        res = jnp.dot(a_ref[...], w_ref[...], preferred_element_type=jnp.float32)
        epilogue(res, rest[:n_extra], rest[n_extra:])

    if cols.layer is None:
        w_spec = pl.BlockSpec((k, tn), lambda j, i: (0, c0 + j))
    else:
        w_spec = pl.BlockSpec((None, k, tn), lambda j, i: (cols.layer, 0, c0 + j))
    return pl.pallas_call(
        body,
        grid=(-(-cols.n // tn), m // tm),
        in_specs=[pl.BlockSpec((tm, k), lambda j, i: (i, 0)), w_spec] + list(extra_specs),
        out_specs=out_specs,
        out_shape=out_shape,
        scratch_shapes=[pltpu.VMEM((k, tn), _CDT)] if convert else [],
        compiler_params=_params("arbitrary", "arbitrary"),
        name=name,
    )(a, cols.w, *extras)


def _ep_rms(res, extras, outs):
    (g_ref,) = extras
    (o_ref,) = outs
    y = res * lax.rsqrt(jnp.mean(res * res, axis=-1, keepdims=True) + _RMS_EPS)
    o_ref[...] = (y * g_ref[...]).astype(o_ref.dtype)


def _ep_heads(res, extras, outs, *, rope, scale):
    (o_ref,) = outs
    nh = res.shape[1] // _HEAD_DIM
    if rope:
        cos = extras[0][...]
        sin = extras[1][...]
    for h in range(nh):
        xh = res[:, h * _HEAD_DIM:(h + 1) * _HEAD_DIM]
        if rope:
            xh = _rope_tile(xh, cos, sin)
        if scale != 1.0:
            xh = xh * scale
        o_ref[0, h] = xh.astype(o_ref.dtype)


def _store_vt(vt_ref, h, vh, t):
    for c in range(vh.shape[0] // t):
        vt_ref[0, h, c, :_HEAD_DIM, :] = vh[c * t:(c + 1) * t, :].T.astype(vt_ref.dtype)
        vt_ref[0, h, c, _HEAD_DIM:, :] = jnp.ones((_VT_ROWS - _HEAD_DIM, t), vt_ref.dtype)


def _ep_vt(res, extras, outs, *, t):
    (vt_ref,) = outs
    for h in range(res.shape[1] // _HEAD_DIM):
        _store_vt(vt_ref, h, res[:, h * _HEAD_DIM:(h + 1) * _HEAD_DIM], t)


def _ep_kidx(res, extras, outs, *, idx_heads, w_scale):
    cos_ref, sin_ref, g_ref, b_ref = extras
    kidx_ref, wt_ref = outs
    x = res[:, :_HEAD_DIM]
    mu = jnp.mean(x, axis=-1, keepdims=True)
    xc = x - mu
    var = jnp.mean(xc * xc, axis=-1, keepdims=True)
    y = xc * lax.rsqrt(var + _LN_EPS) * g_ref[...] + b_ref[...]
    kidx_ref[0] = _rope_tile(y, cos_ref[...], sin_ref[...]).astype(kidx_ref.dtype)
    wt = (res[:, _HEAD_DIM:] * w_scale).T
    wt_ref[0] = wt[:idx_heads, :]


def _ep_silu(res, extras, outs):
    (o_ref,) = outs
    o_ref[0, 0] = (res * jax.nn.sigmoid(res)).astype(o_ref.dtype)


def _ep_logf(res, extras, outs):
    (bias_ref,) = extras
    (o_ref,) = outs
    z = res + bias_ref[...]
    o_ref[0] = jnp.minimum(z, 0.0) - jnp.log(1.0 + jnp.exp(-jnp.abs(z)))


def _outproj(o, gate, w, tm, tn, name):
    b, g, s, c = o.shape
    d = w.shape[1]
    nsb = s // tm

    def body(o_ref, g_ref, w_ref, out_ref):
        acc = None
        for gi in range(g):
            a = (o_ref[0, gi].astype(jnp.float32) * g_ref[0, gi].astype(jnp.float32)).astype(_CDT)
            part = jnp.dot(a, w_ref[gi * c:(gi + 1) * c, :], preferred_element_type=jnp.float32)
            acc = part if acc is None else acc + part
        out_ref[...] = acc

    return pl.pallas_call(
        body,
        grid=(d // tn, b * nsb),
        in_specs=[pl.BlockSpec((1, g, tm, c), lambda j, i: (i // nsb, 0, i % nsb, 0)),
                  pl.BlockSpec((1, g, tm, c), lambda j, i: (i // nsb, 0, i % nsb, 0)),
                  pl.BlockSpec((g * c, tn), lambda j, i: (0, j))],
        out_specs=pl.BlockSpec((tm, tn), lambda j, i: (i, j)),
        out_shape=jax.ShapeDtypeStruct((b * s, d), jnp.float32),
        compiler_params=_params("arbitrary", "arbitrary"),
        name=name,
    )(o, gate, w)


def _resid_ln(h, x, g, b, alpha, tm, with_cdt, name):
    m, d = x.shape

    def body(h_ref, x_ref, g_ref, b_ref, *outs):
        y = alpha * x_ref[...] + h_ref[...]
        mu = jnp.mean(y, axis=-1, keepdims=True)
        yc = y - mu
        var = jnp.mean(yc * yc, axis=-1, keepdims=True)
        z = yc * lax.rsqrt(var + _LN_EPS) * g_ref[...] + b_ref[...]
        outs[0][...] = z
        if with_cdt:
            outs[1][...] = z.astype(_CDT)

    row = pl.BlockSpec((tm, d), lambda i: (i, 0))
    vec = pl.BlockSpec((1, d), lambda i: (0, 0))
    out_shape = [jax.ShapeDtypeStruct((m, d), jnp.float32)]
    out_specs = [row]
    if with_cdt:
        out_shape.append(jax.ShapeDtypeStruct((m, d), _CDT))
        out_specs.append(row)
    return pl.pallas_call(
        body,
        grid=(m // tm,),
        in_specs=[row, row, vec, vec],
        out_specs=out_specs,
        out_shape=out_shape,
        compiler_params=_params("arbitrary"),
        name=name,
    )(h, x, g.reshape(1, d), b.reshape(1, d))


def _col_reduce(x, op):
    rows, n = x.shape
    fold = 8 * _SUBLANES
    if rows > fold and rows % fold == 0:
        x = op(x.reshape(rows // fold, fold, n), axis=0)
    return op(x, axis=0, keepdims=True)


def _softmax_step(sc, m_ref, cols):
    m_old = m_ref[:, cols]
    m_new = jnp.maximum(m_old, _col_reduce(sc, jnp.max))
    m_ref[:, cols] = m_new
    return jnp.exp2(sc - m_new).astype(_CDT), jnp.exp2(m_old - m_new)


def _pv_step(vta, p, alpha, acc_ref, cols):
    pv = jnp.dot(vta, p, preferred_element_type=jnp.float32)
    acc_ref[:, cols] = pv if alpha is None else alpha * acc_ref[:, cols] + pv


def _float_sort_key(x):
    bits = pltpu.bitcast(x, jnp.int32)
    return bits ^ ((bits >> 31) & jnp.int32(0x7FFFFFFF))


def _dsa_core(q, qi, wt, kidx, k, vt, *, topk, tk1, tk, hchunk):
    b, n_heads, s, _ = q.shape
    idx_heads = qi.shape[1]
    n_kv = k.shape[1]
    tq = _LANES
    gq = _GROUP * tq
    log_s = int(math.log2(s))
    assert 1 << log_s == s and s % tk == 0 and tk % tk1 == 0 and tk1 % tq == 0
    assert idx_heads % hchunk == 0 and gq % _MXU_DIM == 0

    def body(q_ref, qi_ref, wt_ref, kidx_ref, k_ref, vt_ref, o_ref,
             key_ref, mask_ref, sc_ref, p_ref, al_ref, m_ref, acc_ref):
        i = pl.program_id(1)
        n1 = (i * tq + tq + tk1 - 1) // tk1
        n3 = (i * tq + tq + tk - 1) // tk

        q_pos1 = i * tq + lax.broadcasted_iota(jnp.int32, (tk1, tq), 1)
        s_iota1 = lax.broadcasted_iota(jnp.int32, (tk1, tq), 0)

        def score_tile(kt, carry):
            ks = pl.multiple_of(kt * tk1, tk1)
            kx = kidx_ref[0, pl.ds(ks, tk1), :]
            acc = jnp.zeros((tk1, tq), jnp.float32)
            for hc in range(idx_heads // hchunk):
                qq = qi_ref[0, hc * hchunk:(hc + 1) * hchunk].reshape(hchunk * tq, _HEAD_DIM)
                d = lax.dot_general(kx, qq, _NT, preferred_element_type=jnp.float32)
                for hh in range(hchunk):
                    h = hc * hchunk + hh
                    acc = acc + jnp.maximum(d[:, hh * tq:(hh + 1) * tq], 0.0) * wt_ref[0, h:h + 1, :]
            key = _float_sort_key(acc + 0.0)
            key_ref[pl.ds(ks, tk1), :] = jnp.where(ks + s_iota1 <= q_pos1, key, _INT_MIN)
            return carry

        lax.fori_loop(0, n1, score_tile, 0)

        def pad_tile(kt, carry):
            key_ref[pl.ds(pl.multiple_of(kt * tk1, tk1), tk1), :] = jnp.full((tk1, tq), _INT_MIN, jnp.int32)
            return carry

        lax.fori_loop(n1, n3 * (tk // tk1), pad_tile, 0)

        q_pos = i * tq + lax.broadcasted_iota(jnp.int32, (tk, tq), 1)
        s_iota = lax.broadcasted_iota(jnp.int32, (tk, tq), 0)

        fold = 8 * _SUBLANES

        def count(pred_fn):
            def tile(kt, cnt):
                ks = pl.multiple_of(kt * tk, tk)
                c = jnp.where(pred_fn(key_ref[pl.ds(ks, tk), :], ks), 1.0, 0.0)
                return cnt + jnp.sum(c.reshape(tk // fold, fold, tq), axis=0)
            cnt = lax.fori_loop(0, n3, tile, jnp.zeros((fold, tq), jnp.float32))
            return jnp.sum(cnt, axis=0, keepdims=True)

        def value_step(t, prefix):
            cand = prefix | jnp.left_shift(jnp.int32(1), 31 - t)
            cand_s = cand ^ _INT_MIN
            cnt = count(lambda key, ks: key >= cand_s)
            return jnp.where(cnt >= topk, cand, prefix)

        prefix = lax.fori_loop(0, 32, value_step, jnp.zeros((1, tq), jnp.int32))
        thr = prefix ^ _INT_MIN

        def tie_break():
            need = topk - count(lambda key, ks: key > thr)

            def index_step(t, j):
                cand = j | jnp.left_shift(jnp.int32(1), log_s - 1 - t)
                cnt = count(lambda key, ks: (key == thr) & (ks + s_iota < cand))
                return jnp.where(cnt < need, cand, j)

            return lax.fori_loop(0, log_s, index_step, jnp.zeros((1, tq), jnp.int32))

        n_ge = count(lambda key, ks: key >= thr)
        j_last = lax.cond(jnp.max(n_ge) > topk, tie_break,
                          lambda: jnp.full((1, tq), s, jnp.int32))

        def mask_tile(kt, carry):
            ks = pl.multiple_of(kt * tk, tk)
            key = key_ref[pl.ds(ks, tk), :]
            s_pos = ks + s_iota
            sel = (key > thr) | ((key == thr) & (s_pos <= j_last))
            mask_ref[pl.ds(ks, tk), :] = jnp.where(sel & (s_pos <= q_pos), 0.0, _MASKED).astype(_CDT)
            return carry

        lax.fori_loop(0, n3, mask_tile, 0)

        r = lax.broadcasted_iota(jnp.int32, (gq, tq), 0)
        c = lax.broadcasted_iota(jnp.int32, (gq, tq), 1)
        one_hot = jnp.where((r & (tq - 1)) == c, 1.0, 0.0).astype(_CDT)

        chunks = [slice(cc * _MXU_DIM, (cc + 1) * _MXU_DIM) for cc in range(gq // _MXU_DIM)]

        def kv_group(g, carry):
            qg = q_ref[0, pl.ds(g * _GROUP, _GROUP)].reshape(gq, _HEAD_DIM)
            qaug = jnp.concatenate([qg, one_hot], axis=1)
            m_ref[...] = jnp.full((1, gq), _M_INIT, jnp.float32)

            def logits(kt):
                kaug = jnp.concatenate([k_ref[0, g, pl.ds(kt * tk, tk), :],
                                        mask_ref[pl.ds(kt * tk, tk), :]], axis=1)
                for cols in chunks:
                    sc_ref[kt & 1, :, cols] = lax.dot_general(kaug, qaug[cols], _NT,
                                                              preferred_element_type=jnp.float32)

            def softmax(kt):
                for cols in chunks:
                    p_ref[kt & 1, :, cols], al_ref[kt & 1, :, cols] = _softmax_step(
                        sc_ref[kt & 1, :, cols], m_ref, cols)

            def pv(kt):
                vta = vt_ref[0, g, kt]
                for cols in chunks:
                    _pv_step(vta, p_ref[kt & 1, :, cols], None if kt == 0 else al_ref[kt & 1, :, cols],
                             acc_ref, cols)

            for n_tiles in range(1, s // tk + 1):
                @pl.when(n3 == n_tiles)
                def _():
                    for step in range(n_tiles + 2):
                        if step < n_tiles:
                            logits(step)
                        if 1 <= step <= n_tiles:
                            softmax(step - 1)
                        if step >= 2:
                            pv(step - 2)

            acc = acc_ref[...]
            ot = acc[:_HEAD_DIM] * (1.0 / acc[_HEAD_DIM:_HEAD_DIM + 1])
            for hh in range(_GROUP):
                o_ref[0, g, :, hh * _HEAD_DIM:(hh + 1) * _HEAD_DIM] = (
                    ot[:, hh * tq:(hh + 1) * tq].T.astype(o_ref.dtype))
            return carry

        lax.fori_loop(0, n_kv, kv_group, 0)

    nq = s // tq
    return pl.pallas_call(
        body,
        grid=(b, nq),
        in_specs=[pl.BlockSpec((1, n_heads, tq, _HEAD_DIM), lambda bi, i: (bi, 0, i, 0)),
                  pl.BlockSpec((1, idx_heads, tq, _HEAD_DIM), lambda bi, i: (bi, 0, i, 0)),
                  pl.BlockSpec((1, idx_heads, tq), lambda bi, i: (bi, 0, i)),
                  pl.BlockSpec((1, s, _HEAD_DIM), lambda bi, i: (bi, 0, 0)),
                  pl.BlockSpec((1, n_kv, s, _HEAD_DIM), lambda bi, i: (bi, 0, 0, 0)),
                  pl.BlockSpec((1, n_kv, s // tk, _VT_ROWS, tk), lambda bi, i: (bi, 0, 0, 0, 0))],
        out_specs=pl.BlockSpec((1, n_kv, tq, _GROUP * _HEAD_DIM), lambda bi, i: (bi, 0, i, 0)),
        out_shape=jax.ShapeDtypeStruct((b, n_kv, s, _GROUP * _HEAD_DIM), _CDT),
        scratch_shapes=[pltpu.VMEM((s, tq), jnp.int32),
                        pltpu.VMEM((s, tq), _CDT),
                        pltpu.VMEM((2, tk, gq), jnp.float32),
                        pltpu.VMEM((2, tk, gq), _CDT),
                        pltpu.VMEM((2, 1, gq), jnp.float32),
                        pltpu.VMEM((1, gq), jnp.float32),
                        pltpu.VMEM((_VT_ROWS, gq), jnp.float32)],
        compiler_params=_params("arbitrary", "arbitrary"),
        name="dsa_core",
    )(q, qi, wt, kidx, k, vt)


def _cumsum_aug(lf, n_heads):
    b, s, w = lf.shape
    blk = _LANES

    def body(lf_ref, out_ref, carry_ref):
        @pl.when(pl.program_id(1) == 0)
        def _():
            carry_ref[...] = jnp.zeros_like(carry_ref)
        r = lax.broadcasted_iota(jnp.int32, (blk, blk), 0)
        c = lax.broadcasted_iota(jnp.int32, (blk, blk), 1)
        tri = jnp.where(r >= c, 1.0, 0.0).astype(jnp.float32)
        cs = jnp.dot(tri, lf_ref[0], preferred_element_type=jnp.float32,
                     precision=lax.Precision.HIGHEST) + carry_ref[...]
        carry_ref[...] = cs[blk - 1:blk, :]
        c2 = cs * _LOG2E
        for h in range(n_heads):
            col = c2[:, h:h + 1]
            hi = col.astype(_CDT).astype(jnp.float32)
            mid = (col - hi).astype(_CDT).astype(jnp.float32)
            lo = (col - hi - mid).astype(_CDT).astype(jnp.float32)
            tile = jnp.where(c == 0, hi, jnp.where(c == 1, mid, jnp.where(c == 2, lo, 0.0)))
            out_ref[0, h] = tile.astype(out_ref.dtype)

    return pl.pallas_call(
        body,
        grid=(b, s // blk),
        in_specs=[pl.BlockSpec((1, blk, w), lambda bi, i: (bi, i, 0))],
        out_specs=pl.BlockSpec((1, n_heads, blk, _LANES), lambda bi, i: (bi, 0, i, 0)),
        out_shape=jax.ShapeDtypeStruct((b, n_heads, s, _LANES), _CDT),
        scratch_shapes=[pltpu.VMEM((1, w), jnp.float32)],
        compiler_params=_params("arbitrary", "arbitrary"),
        name="fox_cumsum",
    )(lf)


def _fox_core(q, k, caug, vt, *, t):
    b, n_heads, s, _ = q.shape
    assert t % _MXU_DIM == 0 and s % t == 0

    chunks = [slice(cc * _MXU_DIM, (cc + 1) * _MXU_DIM) for cc in range(t // _MXU_DIM)]

    tiles = [(i, j) for i in range(s // t) for j in range(i + 1)]

    def body(q_ref, k_ref, c_ref, vt_ref, o_ref, kaug_ref, sc_ref, p_ref, al_ref, m_ref, acc_ref):
        kaug_ref[:, :_HEAD_DIM] = k_ref[0, 0]
        kaug_ref[:, _HEAD_DIM:] = c_ref[0, 0]
        lane = lax.broadcasted_iota(jnp.int32, (t, _HEAD_DIM), 1)
        minus_ones = jnp.where(lane < 3, -1.0, 0.0).astype(_CDT)

        def logits(n):
            i, j = tiles[n]
            qaug = jnp.concatenate([q_ref[0, 0, i * t:(i + 1) * t, :], minus_ones], axis=1)
            ka = kaug_ref[j * t:(j + 1) * t, :]
            for cols in chunks:
                sc_ref[n & 1, :, cols] = lax.dot_general(ka, qaug[cols], _NT,
                                                         preferred_element_type=jnp.float32)

        def softmax(n):
            i, j = tiles[n]
            if j == 0:
                m_ref[...] = jnp.full((1, t), _M_INIT, jnp.float32)
            for cols in chunks:
                sc = sc_ref[n & 1, :, cols]
                if i == j:
                    key_i = lax.broadcasted_iota(jnp.int32, (t, _MXU_DIM), 0)
                    qry_i = cols.start + lax.broadcasted_iota(jnp.int32, (t, _MXU_DIM), 1)
                    sc = jnp.where(key_i <= qry_i, sc, -jnp.inf)
                p_ref[n & 1, :, cols], al_ref[n & 1, :, cols] = _softmax_step(sc, m_ref, cols)

        def pv(n):
            i, j = tiles[n]
            vta = vt_ref[0, 0, j]
            for cols in chunks:
                _pv_step(vta, p_ref[n & 1, :, cols], None if j == 0 else al_ref[n & 1, :, cols],
                         acc_ref, cols)
            if j == i:
                acc = acc_ref[...]
                ot = acc[:_HEAD_DIM] * (1.0 / acc[_HEAD_DIM:_HEAD_DIM + 1])
                for cq in range(t // _LANES):
                    o_ref[0, i * t + cq * _LANES:i * t + (cq + 1) * _LANES, :] = (
                        ot[:, cq * _LANES:(cq + 1) * _LANES].T.astype(o_ref.dtype))

        for step in range(len(tiles) + 2):
            if step < len(tiles):
                logits(step)
            if 1 <= step <= len(tiles):
                softmax(step - 1)
            if step >= 2:
                pv(step - 2)

    head = pl.BlockSpec((1, 1, s, _HEAD_DIM), lambda bi, h: (bi, h, 0, 0))
    return pl.pallas_call(
        body,
        grid=(b, n_heads),
        in_specs=[head, head, head,
                  pl.BlockSpec((1, 1, s // t, _VT_ROWS, t), lambda bi, h: (bi, h, 0, 0, 0))],
        out_specs=pl.BlockSpec((1, s, _HEAD_DIM), lambda bi, h: (bi, 0, h)),
        out_shape=jax.ShapeDtypeStruct((b, s, n_heads * _HEAD_DIM), _CDT),
        scratch_shapes=[pltpu.VMEM((s, 2 * _HEAD_DIM), _CDT),
                        pltpu.VMEM((2, t, t), jnp.float32),
                        pltpu.VMEM((2, t, t), _CDT),
                        pltpu.VMEM((2, 1, t), jnp.float32),
                        pltpu.VMEM((1, t), jnp.float32),
                        pltpu.VMEM((_VT_ROWS, t), jnp.float32)],
        compiler_params=_params("arbitrary", "arbitrary"),
        name="fox_core",
    )(q, k, caug, vt)


def _rope_tables(s):
    half = _HEAD_DIM // 2
    inv = _ROPE_THETA ** (-jnp.arange(half, dtype=jnp.float32) / half)
    ang = jnp.arange(s, dtype=jnp.float32)[:, None] * inv[None, :]
    cos, sin = jnp.cos(ang), jnp.sin(ang)
    return jnp.concatenate([cos, cos], axis=1), jnp.concatenate([-sin, sin], axis=1)


def _placed(cols, s, tn_cap=None):
    tm, tn = _proj_tiles(s, cols)
    if tn_cap is not None:
        tn = min(tn, tn_cap)
    if not cols.aligned(tn):
        cols = cols.materialize()
        tm, tn = _proj_tiles(s, cols)
        if tn_cap is not None:
            tn = min(tn, tn_cap)
    return cols, tm, tn


def _heads_call(a, cols, b, s, *, rope, scale, cos, sin, name):
    cols, tm, tn = _placed(cols, s)
    nsb = s // tm
    extras, extra_specs = (), ()
    if rope:
        extras = (cos, sin)
        extra_specs = (pl.BlockSpec((tm, _HEAD_DIM), lambda j, i: (i % nsb, 0)),) * 2
    nh_t = tn // _HEAD_DIM
    return _proj(
        a, cols, functools.partial(_ep_heads, rope=rope, scale=scale), extras, extra_specs,
        jax.ShapeDtypeStruct((b, cols.n // _HEAD_DIM, s, _HEAD_DIM), _CDT),
        pl.BlockSpec((1, nh_t, tm, _HEAD_DIM), lambda j, i: (i // nsb, j, i % nsb, 0)),
        tm, tn, name)


def _vt_call(a, cols, b, s, t, name):
    cols, tm, tn = _placed(cols, s)
    nsb = s // tm
    nh_t = tn // _HEAD_DIM
    return _proj(
        a, cols, functools.partial(_ep_vt, t=t), (), (),
        jax.ShapeDtypeStruct((b, cols.n // _HEAD_DIM, s // t, _VT_ROWS, t), _CDT),
        pl.BlockSpec((1, nh_t, tm // t, _VT_ROWS, t), lambda j, i: (i // nsb, j, i % nsb, 0, 0)),
        tm, tn, name)


def _gate_call(a, cols, b, s, groups, name):
    c = cols.n // groups
    cols, tm, tn = _placed(cols, s, tn_cap=c)
    nsb = s // tm
    per = c // tn
    return _proj(
        a, cols, _ep_silu, (), (),
        jax.ShapeDtypeStruct((b, groups, s, c), _CDT),
        pl.BlockSpec((1, 1, tm, tn), lambda j, i: (i // nsb, j // per, i % nsb, j % per)),
        tm, tn, name)


def _dsa_layer(xc, w_in, q_norm_g, w_uq, kidx_g, kidx_b, w_out, layer, b, s):
    d = xc.shape[1]
    q_lora = w_uq.shape[1]
    branch = w_out.shape[1]
    idx_heads = (w_uq.shape[2] - branch) // _HEAD_DIM
    kv = (w_in.shape[2] - q_lora - _HEAD_DIM - idx_heads - branch) // 2
    n_kv = kv // _HEAD_DIM
    n_heads = branch // _HEAD_DIM
    assert n_heads == n_kv * _GROUP and idx_heads <= _LANES
    topk = min(_TOPK_MAX, s // 4)
    o0 = q_lora
    o1 = o0 + 2 * kv
    o2 = o1 + _HEAD_DIM
    o3 = o2 + idx_heads
    cos, sin = _rope_tables(s)
    tk1 = min(256, s)
    tk = min(512, s)

    c_cq = _Cols(w_in, 0, q_lora, layer).materialize()
    tm = min(512, s)
    cq = _proj(xc, c_cq, _ep_rms, (q_norm_g[layer].reshape(1, q_lora),),
               (pl.BlockSpec((1, q_lora), lambda j, i: (0, 0)),),
               jax.ShapeDtypeStruct((b * s, q_lora), _CDT),
               pl.BlockSpec((tm, q_lora), lambda j, i: (i, 0)), tm, q_lora, "dsa_cq")
    k = _heads_call(xc, _Cols(w_in, o0, kv, layer), b, s, rope=True, scale=1.0, cos=cos, sin=sin,
                    name="dsa_k")
    vt = _vt_call(xc, _Cols(w_in, o0 + kv, kv, layer), b, s, tk, "dsa_v")

    c_ki = _Cols(w_in, o1, 2 * _LANES, layer)
    if not c_ki.aligned(2 * _LANES):
        w_ki = jnp.pad(w_in[layer][:, o1:o3], ((0, 0), (0, _LANES - idx_heads)))
        c_ki = _Cols(w_ki.astype(_CDT))
    nsb = s // tm
    rope_specs = (pl.BlockSpec((tm, _HEAD_DIM), lambda j, i: (i % nsb, 0)),) * 2
    w_scale = idx_heads ** -0.5 * _HEAD_DIM ** -0.5
    vec = pl.BlockSpec((1, _HEAD_DIM), lambda j, i: (0, 0))
    kidx, wt = _proj(xc, c_ki, functools.partial(_ep_kidx, idx_heads=idx_heads, w_scale=w_scale),
                     (cos, sin, kidx_g[layer].reshape(1, _HEAD_DIM), kidx_b[layer].reshape(1, _HEAD_DIM)),
                     rope_specs + (vec, vec),
                     [jax.ShapeDtypeStruct((b, s, _HEAD_DIM), _CDT),
                      jax.ShapeDtypeStruct((b, idx_heads, s), jnp.float32)],
                     [pl.BlockSpec((1, tm, _HEAD_DIM), lambda j, i: (i // nsb, i % nsb, 0)),
                      pl.BlockSpec((1, idx_heads, tm), lambda j, i: (i // nsb, 0, i % nsb))],
                     tm, 2 * _LANES, "dsa_kidx")
    gate = _gate_call(xc, _Cols(w_in, o3, branch, layer), b, s, n_kv, "dsa_gate")
    q = _heads_call(cq, _Cols(w_uq, 0, branch, layer), b, s, rope=True,
                    scale=_HEAD_DIM ** -0.5 * _LOG2E, cos=cos, sin=sin, name="dsa_q")
    qi = _heads_call(cq, _Cols(w_uq, branch, None, layer), b, s, rope=True, scale=1.0, cos=cos, sin=sin,
                     name="dsa_qi")
    o = _dsa_core(q, qi, wt, kidx, k, vt, topk=topk, tk1=tk1, tk=tk, hchunk=min(8, idx_heads))
    return _outproj(o, gate, w_out[layer].astype(_CDT), min(512, s), min(1024, d), "dsa_out")


def _fox_layer(xc, w_in, forget_bias, w_out, layer, b, s):
    d = xc.shape[1]
    branch = w_out.shape[1]
    n_heads = forget_bias.shape[1]
    assert branch == n_heads * _HEAD_DIM and n_heads <= _LANES
    w_f = jnp.pad(w_in[layer][:, 4 * branch:], ((0, 0), (0, _LANES - n_heads))).astype(_CDT)
    fb = jnp.pad(forget_bias[layer], (0, _LANES - n_heads)).reshape(1, _LANES)
    tm = min(512, s)
    nsb = s // tm
    t = min(512, s)
    q = _heads_call(xc, _Cols(w_in, 0, branch, layer), b, s, rope=False,
                    scale=_HEAD_DIM ** -0.5 * _LOG2E, cos=None, sin=None, name="fox_q")
    k = _heads_call(xc, _Cols(w_in, branch, branch, layer), b, s, rope=False, scale=1.0,
                    cos=None, sin=None, name="fox_k")
    vt = _vt_call(xc, _Cols(w_in, 2 * branch, branch, layer), b, s, t, "fox_v")
    gate = _gate_call(xc, _Cols(w_in, 3 * branch, branch, layer), b, s, 1, "fox_gate")
    lf = _proj(xc, _Cols(w_f), _ep_logf, (fb,), (pl.BlockSpec((1, _LANES), lambda j, i: (0, 0)),),
               jax.ShapeDtypeStruct((b, s, _LANES), jnp.float32),
               pl.BlockSpec((1, tm, _LANES), lambda j, i: (i // nsb, i % nsb, 0)),
               tm, _LANES, "fox_logf")
    caug = _cumsum_aug(lf, n_heads)
    o = _fox_core(q, k, caug, vt, t=t)
    return _outproj(o.reshape(b, 1, s, branch), gate, w_out[layer].astype(_CDT), tm, min(1024, d),
                    "fox_out")


def kernel(x, a_w_in, a_q_norm_g, a_w_uq, a_kidx_norm_g, a_kidx_norm_b, a_w_out,
           b_w_in, b_forget_bias, b_w_out, ln_g, ln_b):
    b, s, d = x.shape
    depth = ln_g.shape[0]
    alpha = (2 * depth) ** 0.25
    xf = x.reshape(b * s, d)
    xc = xf.astype(_CDT)
    tm_ln = min(256, s)
    for i in range(depth):
        j = i // 2
        if i % 2 == 0:
            h = _dsa_layer(xc, a_w_in, a_q_norm_g, a_w_uq, a_kidx_norm_g, a_kidx_norm_b, a_w_out, j, b, s)
        else:
            h = _fox_layer(xc, b_w_in, b_forget_bias, b_w_out, j, b, s)
        last = i == depth - 1
        outs = _resid_ln(h, xf, ln_g[i], ln_b[i], alpha, tm_ln, not last, "resid_ln_%d" % i)
        xf = outs[0]
        if not last:
            xc = outs[1]
    return xf.reshape(b, s, d)
```

```python
import functools
import math

import jax
import jax.numpy as jnp
from jax import lax
from jax.experimental import pallas as pl
from jax.experimental.pallas import tpu as pltpu

_CDT = jnp.bfloat16
_HEAD_DIM = 128
_GROUP = 8
_TOPK_MAX = 256
_ROPE_THETA = 10000.0
_LN_EPS = 1e-5
_RMS_EPS = 1e-6
_LANES = 128
_SUBLANES = 8
_MXU_DIM = 256
_VMEM_LIMIT = 56 * 1024 * 1024
_VT_ROWS = _HEAD_DIM + 16
_INT_MIN = -2 ** 31
_MASKED = -1e30
_M_INIT = -1e29
_LOG2E = math.log2(math.e)

_NT = (((1,), (1,)), ((), ()))


def _params(*sem):
    return pltpu.CompilerParams(dimension_semantics=sem, vmem_limit_bytes=_VMEM_LIMIT)


def _rope_tile(x, cos, sin_signed):
    return x * cos + pltpu.roll(x, _HEAD_DIM // 2, axis=1) * sin_signed


class _Cols:
    def __init__(self, w, col0=0, n=None, layer=None, transposed=False):
        self.w, self.col0, self.layer, self.transposed = w, col0, layer, transposed
        self.n_total = w.shape[-2] if transposed else w.shape[-1]
        self.k = w.shape[-1] if transposed else w.shape[-2]
        self.n = self.n_total - col0 if n is None else n

    def block_aligned(self, tn):
        return self.col0 % tn == 0

    def aligned(self, tn):
        start_ok = self.col0 % _SUBLANES == 0 if self.transposed else self.block_aligned(tn)
        return start_ok and self.col0 + -(-self.n // tn) * tn <= self.n_total

    def materialize(self):
        w = self.w if self.layer is None else self.w[self.layer]
        if self.transposed:
            return _Cols(w[self.col0:self.col0 + self.n, :].astype(_CDT), transposed=True)
        return _Cols(w[:, self.col0:self.col0 + self.n].astype(_CDT))


def _proj_tiles(s, cols):
    tn = min(cols.n, 1024)
    while cols.k * tn * cols.w.dtype.itemsize > 8 * 1024 * 1024:
        tn //= 2
    return min(s, 1024 if tn <= 512 else 512), tn


def _proj(a, cols, epilogue, extras, extra_specs, out_shape, out_specs, tm, tn, name):
    m, k = a.shape
    assert cols.aligned(tn) and k == cols.k
    convert = cols.w.dtype != _CDT
    n_extra = len(extras)
    w_block = (tn, k) if cols.transposed else (k, tn)

    def body(a_ref, w_ref, *rest):
        if convert:
            wc_ref = rest[-1]
            rest = rest[:-1]

            @pl.when(pl.program_id(1) == 0)
            def _():
                wc_ref[...] = w_ref[...].reshape(w_block).astype(_CDT)

            w = wc_ref[...]
        else:
            w = w_ref[...].reshape(w_block)
        if cols.transposed:
            res = lax.dot_general(a_ref[...], w, _NT, preferred_element_type=jnp.float32)
        else:
            res = jnp.dot(a_ref[...], w, preferred_element_type=jnp.float32)
        epilogue(res, rest[:n_extra], rest[n_extra:])

    lead_index = () if cols.layer is None else (cols.layer,)
    if cols.block_aligned(tn):
        c0 = cols.col0 // tn
        lead = () if cols.layer is None else (None,)
        if cols.transposed:
            w_spec = pl.BlockSpec(lead + (tn, k), lambda j, i: lead_index + (c0 + j, 0))
        else:
            w_spec = pl.BlockSpec(lead + (k, tn), lambda j, i: lead_index + (0, c0 + j))
    else:
        assert cols.transposed
        lead = () if cols.layer is None else (pl.Element(1),)
        w_spec = pl.BlockSpec(lead + (pl.Element(tn), pl.Element(k)),
                              lambda j, i: lead_index + (pl.multiple_of(cols.col0 + j * tn, _SUBLANES), 0))
    return pl.pallas_call(
        body,
        grid=(-(-cols.n // tn), m // tm),
        in_specs=[pl.BlockSpec((tm, k), lambda j, i: (i, 0)), w_spec] + list(extra_specs),
        out_specs=out_specs,
        out_shape=out_shape,
        scratch_shapes=[pltpu.VMEM(w_block, _CDT)] if convert else [],
        compiler_params=_params("arbitrary", "arbitrary"),
        name=name,
    )(a, cols.w, *extras)


def _ep_rms(res, extras, outs):
    (g_ref,) = extras
    (o_ref,) = outs
    y = res * lax.rsqrt(jnp.mean(res * res, axis=-1, keepdims=True) + _RMS_EPS)
    o_ref[...] = (y * g_ref[...]).astype(o_ref.dtype)


def _ep_heads(res, extras, outs, *, rope, scale):
    (o_ref,) = outs
    nh = res.shape[1] // _HEAD_DIM
    if rope:
        cos = extras[0][...]
        sin = extras[1][...]
    for h in range(nh):
        xh = res[:, h * _HEAD_DIM:(h + 1) * _HEAD_DIM]
        if rope:
            xh = _rope_tile(xh, cos, sin)
        if scale != 1.0:
            xh = xh * scale
        o_ref[0, h] = xh.astype(o_ref.dtype)


def _store_vt(vt_ref, h, vh, t):
    for c in range(vh.shape[0] // t):
        vt_ref[0, h, c, :_HEAD_DIM, :] = vh[c * t:(c + 1) * t, :].T.astype(vt_ref.dtype)
        vt_ref[0, h, c, _HEAD_DIM:, :] = jnp.ones((_VT_ROWS - _HEAD_DIM, t), vt_ref.dtype)


def _ep_vt(res, extras, outs, *, t):
    (vt_ref,) = outs
    for h in range(res.shape[1] // _HEAD_DIM):
        _store_vt(vt_ref, h, res[:, h * _HEAD_DIM:(h + 1) * _HEAD_DIM], t)


def _ep_kidx(res, extras, outs, *, idx_heads, w_scale):
    cos_ref, sin_ref, g_ref, b_ref = extras
    kidx_ref, wt_ref = outs
    x = res[:, :_HEAD_DIM]
    mu = jnp.mean(x, axis=-1, keepdims=True)
    xc = x - mu
    var = jnp.mean(xc * xc, axis=-1, keepdims=True)
    y = xc * lax.rsqrt(var + _LN_EPS) * g_ref[...] + b_ref[...]
    kidx_ref[0] = _rope_tile(y, cos_ref[...], sin_ref[...]).astype(kidx_ref.dtype)
    wt = (res[:, _HEAD_DIM:] * w_scale).T
    wt_ref[0] = wt[:idx_heads, :]


def _ep_silu(res, extras, outs):
    (o_ref,) = outs
    o_ref[0, 0] = (res * jax.nn.sigmoid(res)).astype(o_ref.dtype)


def _ep_logf(res, extras, outs):
    (bias_ref,) = extras
    (o_ref,) = outs
    z = res + bias_ref[...]
    o_ref[0] = jnp.minimum(z, 0.0) - jnp.log(1.0 + jnp.exp(-jnp.abs(z)))


def _outproj(o, gate, w, tm, tn, name):
    b, g, s, c = o.shape
    d = w.shape[1]
    nsb = s // tm

    def body(o_ref, g_ref, w_ref, out_ref):
        acc = None
        for gi in range(g):
            a = (o_ref[0, gi].astype(jnp.float32) * g_ref[0, gi].astype(jnp.float32)).astype(_CDT)
            part = jnp.dot(a, w_ref[gi * c:(gi + 1) * c, :], preferred_element_type=jnp.float32)
            acc = part if acc is None else acc + part
        out_ref[...] = acc

    return pl.pallas_call(
        body,
        grid=(d // tn, b * nsb),
        in_specs=[pl.BlockSpec((1, g, tm, c), lambda j, i: (i // nsb, 0, i % nsb, 0)),
                  pl.BlockSpec((1, g, tm, c), lambda j, i: (i // nsb, 0, i % nsb, 0)),
                  pl.BlockSpec((g * c, tn), lambda j, i: (0, j))],
        out_specs=pl.BlockSpec((tm, tn), lambda j, i: (i, j)),
        out_shape=jax.ShapeDtypeStruct((b * s, d), jnp.float32),
        compiler_params=_params("arbitrary", "arbitrary"),
        name=name,
    )(o, gate, w)


def _resid_ln(h, x, g, b, alpha, tm, with_cdt, name):
    m, d = x.shape

    def body(h_ref, x_ref, g_ref, b_ref, *outs):
        y = alpha * x_ref[...] + h_ref[...]
        mu = jnp.mean(y, axis=-1, keepdims=True)
        yc = y - mu
        var = jnp.mean(yc * yc, axis=-1, keepdims=True)
        z = yc * lax.rsqrt(var + _LN_EPS) * g_ref[...] + b_ref[...]
        outs[0][...] = z
        if with_cdt:
            outs[1][...] = z.astype(_CDT)

    row = pl.BlockSpec((tm, d), lambda i: (i, 0))
    vec = pl.BlockSpec((1, d), lambda i: (0, 0))
    out_shape = [jax.ShapeDtypeStruct((m, d), jnp.float32)]
    out_specs = [row]
    if with_cdt:
        out_shape.append(jax.ShapeDtypeStruct((m, d), _CDT))
        out_specs.append(row)
    return pl.pallas_call(
        body,
        grid=(m // tm,),
        in_specs=[row, row, vec, vec],
        out_specs=out_specs,
        out_shape=out_shape,
        compiler_params=_params("arbitrary"),
        name=name,
    )(h, x, g.reshape(1, d), b.reshape(1, d))


def _col_reduce(x, op):
    rows, n = x.shape
    fold = 8 * _SUBLANES
    if rows > fold and rows % fold == 0:
        x = op(x.reshape(rows // fold, fold, n), axis=0)
    return op(x, axis=0, keepdims=True)


def _softmax_step(sc, m_ref, cols):
    m_old = m_ref[:, cols]
    m_new = jnp.maximum(m_old, _col_reduce(sc, jnp.max))
    m_ref[:, cols] = m_new
    return jnp.exp2(sc - m_new).astype(_CDT), jnp.exp2(m_old - m_new)


def _pv_step(vta, p, alpha, acc_ref, cols):
    pv = jnp.dot(vta, p, preferred_element_type=jnp.float32)
    acc_ref[:, cols] = pv if alpha is None else alpha * acc_ref[:, cols] + pv


def _float_sort_key(x):
    bits = pltpu.bitcast(x, jnp.int32)
    return bits ^ ((bits >> 31) & jnp.int32(0x7FFFFFFF))


def _dsa_core(q, qi, wt, kidx, k, vt, *, topk, tk1, tk, hchunk):
    b, n_heads, s, _ = q.shape
    idx_heads = qi.shape[1]
    n_kv = k.shape[1]
    tq = _LANES
    gq = _GROUP * tq
    log_s = int(math.log2(s))
    assert 1 << log_s == s and s % tk == 0 and tk % tk1 == 0 and tk1 % tq == 0
    assert idx_heads % hchunk == 0 and gq % _MXU_DIM == 0

    def body(q_ref, qi_ref, wt_ref, kidx_ref, k_ref, vt_ref, o_ref,
             key_ref, mask_ref, sc_ref, p_ref, al_ref, m_ref, acc_ref):
        i = pl.program_id(1)
        n1 = (i * tq + tq + tk1 - 1) // tk1
        n3 = (i * tq + tq + tk - 1) // tk

        q_pos1 = i * tq + lax.broadcasted_iota(jnp.int32, (tk1, tq), 1)
        s_iota1 = lax.broadcasted_iota(jnp.int32, (tk1, tq), 0)

        def score_tile(kt, carry):
            ks = pl.multiple_of(kt * tk1, tk1)
            kx = kidx_ref[0, pl.ds(ks, tk1), :]
            acc = jnp.zeros((tk1, tq), jnp.float32)
            for hc in range(idx_heads // hchunk):
                qq = qi_ref[0, hc * hchunk:(hc + 1) * hchunk].reshape(hchunk * tq, _HEAD_DIM)
                d = lax.dot_general(kx, qq, _NT, preferred_element_type=jnp.float32)
                for hh in range(hchunk):
                    h = hc * hchunk + hh
                    acc = acc + jnp.maximum(d[:, hh * tq:(hh + 1) * tq], 0.0) * wt_ref[0, h:h + 1, :]
            key = _float_sort_key(acc + 0.0)
            key_ref[pl.ds(ks, tk1), :] = jnp.where(ks + s_iota1 <= q_pos1, key, _INT_MIN)
            return carry

        lax.fori_loop(0, n1, score_tile, 0)

        def pad_tile(kt, carry):
            key_ref[pl.ds(pl.multiple_of(kt * tk1, tk1), tk1), :] = jnp.full((tk1, tq), _INT_MIN, jnp.int32)
            return carry

        lax.fori_loop(n1, n3 * (tk // tk1), pad_tile, 0)

        q_pos = i * tq + lax.broadcasted_iota(jnp.int32, (tk, tq), 1)
        s_iota = lax.broadcasted_iota(jnp.int32, (tk, tq), 0)

        fold = 8 * _SUBLANES

        def count(pred_fn):
            def tile(kt, cnt):
                ks = pl.multiple_of(kt * tk, tk)
                c = jnp.where(pred_fn(key_ref[pl.ds(ks, tk), :], ks), 1.0, 0.0)
                return cnt + jnp.sum(c.reshape(tk // fold, fold, tq), axis=0)
            cnt = lax.fori_loop(0, n3, tile, jnp.zeros((fold, tq), jnp.float32))
            return jnp.sum(cnt, axis=0, keepdims=True)

        def value_step(t, prefix):
            cand = prefix | jnp.left_shift(jnp.int32(1), 31 - t)
            cand_s = cand ^ _INT_MIN
            cnt = count(lambda key, ks: key >= cand_s)
            return jnp.where(cnt >= topk, cand, prefix)

        prefix = lax.fori_loop(0, 32, value_step, jnp.zeros((1, tq), jnp.int32))
        thr = prefix ^ _INT_MIN

        def tie_break():
            need = topk - count(lambda key, ks: key > thr)

            def index_step(t, j):
                cand = j | jnp.left_shift(jnp.int32(1), log_s - 1 - t)
                cnt = count(lambda key, ks: (key == thr) & (ks + s_iota < cand))
                return jnp.where(cnt < need, cand, j)

            return lax.fori_loop(0, log_s, index_step, jnp.zeros((1, tq), jnp.int32))

        n_ge = count(lambda key, ks: key >= thr)
        j_last = lax.cond(jnp.max(n_ge) > topk, tie_break,
                          lambda: jnp.full((1, tq), s, jnp.int32))

        def mask_tile(kt, carry):
            ks = pl.multiple_of(kt * tk, tk)
            key = key_ref[pl.ds(ks, tk), :]
            s_pos = ks + s_iota
            sel = (key > thr) | ((key == thr) & (s_pos <= j_last))
            mask_ref[pl.ds(ks, tk), :] = jnp.where(sel & (s_pos <= q_pos), 0.0, _MASKED).astype(_CDT)
            return carry

        lax.fori_loop(0, n3, mask_tile, 0)

        r = lax.broadcasted_iota(jnp.int32, (gq, tq), 0)
        c = lax.broadcasted_iota(jnp.int32, (gq, tq), 1)
        one_hot = jnp.where((r & (tq - 1)) == c, 1.0, 0.0).astype(_CDT)

        chunks = [slice(cc * _MXU_DIM, (cc + 1) * _MXU_DIM) for cc in range(gq // _MXU_DIM)]

        def kv_group(g, carry):
            qg = q_ref[0, pl.ds(g * _GROUP, _GROUP)].reshape(gq, _HEAD_DIM)
            qaug = jnp.concatenate([qg, one_hot], axis=1)
            m_ref[...] = jnp.full((1, gq), _M_INIT, jnp.float32)

            def logits(kt):
                kaug = jnp.concatenate([k_ref[0, g, pl.ds(kt * tk, tk), :],
                                        mask_ref[pl.ds(kt * tk, tk), :]], axis=1)
                for cols in chunks:
                    sc_ref[kt & 1, :, cols] = lax.dot_general(kaug, qaug[cols], _NT,
                                                              preferred_element_type=jnp.float32)

            def softmax(kt):
                for cols in chunks:
                    p_ref[kt & 1, :, cols], al_ref[kt & 1, :, cols] = _softmax_step(
                        sc_ref[kt & 1, :, cols], m_ref, cols)

            def pv(kt):
                vta = vt_ref[0, g, kt]
                for cols in chunks:
                    _pv_step(vta, p_ref[kt & 1, :, cols], None if kt == 0 else al_ref[kt & 1, :, cols],
                             acc_ref, cols)

            for n_tiles in range(1, s // tk + 1):
                @pl.when(n3 == n_tiles)
                def _():
                    for step in range(n_tiles + 2):
                        if step < n_tiles:
                            logits(step)
                        if 1 <= step <= n_tiles:
                            softmax(step - 1)
                        if step >= 2:
                            pv(step - 2)

            acc = acc_ref[...]
            ot = acc[:_HEAD_DIM] * (1.0 / acc[_HEAD_DIM:_HEAD_DIM + 1])
            for hh in range(_GROUP):
                o_ref[0, g, :, hh * _HEAD_DIM:(hh + 1) * _HEAD_DIM] = (
                    ot[:, hh * tq:(hh + 1) * tq].T.astype(o_ref.dtype))
            return carry

        lax.fori_loop(0, n_kv, kv_group, 0)

    nq = s // tq
    return pl.pallas_call(
        body,
        grid=(b, nq),
        in_specs=[pl.BlockSpec((1, n_heads, tq, _HEAD_DIM), lambda bi, i: (bi, 0, i, 0)),
                  pl.BlockSpec((1, idx_heads, tq, _HEAD_DIM), lambda bi, i: (bi, 0, i, 0)),
                  pl.BlockSpec((1, idx_heads, tq), lambda bi, i: (bi, 0, i)),
                  pl.BlockSpec((1, s, _HEAD_DIM), lambda bi, i: (bi, 0, 0)),
                  pl.BlockSpec((1, n_kv, s, _HEAD_DIM), lambda bi, i: (bi, 0, 0, 0)),
                  pl.BlockSpec((1, n_kv, s // tk, _VT_ROWS, tk), lambda bi, i: (bi, 0, 0, 0, 0))],
        out_specs=pl.BlockSpec((1, n_kv, tq, _GROUP * _HEAD_DIM), lambda bi, i: (bi, 0, i, 0)),
        out_shape=jax.ShapeDtypeStruct((b, n_kv, s, _GROUP * _HEAD_DIM), _CDT),
        scratch_shapes=[pltpu.VMEM((s, tq), jnp.int32),
                        pltpu.VMEM((s, tq), _CDT),
                        pltpu.VMEM((2, tk, gq), jnp.float32),
                        pltpu.VMEM((2, tk, gq), _CDT),
                        pltpu.VMEM((2, 1, gq), jnp.float32),
                        pltpu.VMEM((1, gq), jnp.float32),
                        pltpu.VMEM((_VT_ROWS, gq), jnp.float32)],
        compiler_params=_params("arbitrary", "arbitrary"),
        name="dsa_core",
    )(q, qi, wt, kidx, k, vt)


def _cumsum_aug(lf, n_heads):
    b, s, w = lf.shape
    blk = _LANES

    def body(lf_ref, out_ref, carry_ref):
        @pl.when(pl.program_id(1) == 0)
        def _():
            carry_ref[...] = jnp.zeros_like(carry_ref)
        r = lax.broadcasted_iota(jnp.int32, (blk, blk), 0)
        c = lax.broadcasted_iota(jnp.int32, (blk, blk), 1)
        tri = jnp.where(r >= c, 1.0, 0.0).astype(jnp.float32)
        cs = jnp.dot(tri, lf_ref[0], preferred_element_type=jnp.float32,
                     precision=lax.Precision.HIGHEST) + carry_ref[...]
        carry_ref[...] = cs[blk - 1:blk, :]
        c2 = cs * _LOG2E
        for h in range(n_heads):
            col = c2[:, h:h + 1]
            hi = col.astype(_CDT).astype(jnp.float32)
            mid = (col - hi).astype(_CDT).astype(jnp.float32)
            lo = (col - hi - mid).astype(_CDT).astype(jnp.float32)
            tile = jnp.where(c == 0, hi, jnp.where(c == 1, mid, jnp.where(c == 2, lo, 0.0)))
            out_ref[0, h] = tile.astype(out_ref.dtype)

    return pl.pallas_call(
        body,
        grid=(b, s // blk),
        in_specs=[pl.BlockSpec((1, blk, w), lambda bi, i: (bi, i, 0))],
        out_specs=pl.BlockSpec((1, n_heads, blk, _LANES), lambda bi, i: (bi, 0, i, 0)),
        out_shape=jax.ShapeDtypeStruct((b, n_heads, s, _LANES), _CDT),
        scratch_shapes=[pltpu.VMEM((1, w), jnp.float32)],
        compiler_params=_params("arbitrary", "arbitrary"),
        name="fox_cumsum",
    )(lf)


def _fox_core(q, k, caug, vt, *, t):
    b, n_heads, s, _ = q.shape
    assert t % _MXU_DIM == 0 and s % t == 0

    chunks = [slice(cc * _MXU_DIM, (cc + 1) * _MXU_DIM) for cc in range(t // _MXU_DIM)]

    tiles = [(i, j) for i in range(s // t) for j in range(i + 1)]

    def body(q_ref, k_ref, c_ref, vt_ref, o_ref, kaug_ref, sc_ref, p_ref, al_ref, m_ref, acc_ref):
        kaug_ref[:, :_HEAD_DIM] = k_ref[0, 0]
        kaug_ref[:, _HEAD_DIM:] = c_ref[0, 0]
        lane = lax.broadcasted_iota(jnp.int32, (t, _HEAD_DIM), 1)
        minus_ones = jnp.where(lane < 3, -1.0, 0.0).astype(_CDT)

        def logits(n):
            i, j = tiles[n]
            qaug = jnp.concatenate([q_ref[0, 0, i * t:(i + 1) * t, :], minus_ones], axis=1)
            ka = kaug_ref[j * t:(j + 1) * t, :]
            for cols in chunks:
                sc_ref[n & 1, :, cols] = lax.dot_general(ka, qaug[cols], _NT,
                                                         preferred_element_type=jnp.float32)

        def softmax(n):
            i, j = tiles[n]
            if j == 0:
                m_ref[...] = jnp.full((1, t), _M_INIT, jnp.float32)
            for cols in chunks:
                sc = sc_ref[n & 1, :, cols]
                if i == j:
                    key_i = lax.broadcasted_iota(jnp.int32, (t, _MXU_DIM), 0)
                    qry_i = cols.start + lax.broadcasted_iota(jnp.int32, (t, _MXU_DIM), 1)
                    sc = jnp.where(key_i <= qry_i, sc, -jnp.inf)
                p_ref[n & 1, :, cols], al_ref[n & 1, :, cols] = _softmax_step(sc, m_ref, cols)

        def pv(n):
            i, j = tiles[n]
            vta = vt_ref[0, 0, j]
            for cols in chunks:
                _pv_step(vta, p_ref[n & 1, :, cols], None if j == 0 else al_ref[n & 1, :, cols],
                         acc_ref, cols)
            if j == i:
                acc = acc_ref[...]
                ot = acc[:_HEAD_DIM] * (1.0 / acc[_HEAD_DIM:_HEAD_DIM + 1])
                for cq in range(t // _LANES):
                    o_ref[0, i * t + cq * _LANES:i * t + (cq + 1) * _LANES, :] = (
                        ot[:, cq * _LANES:(cq + 1) * _LANES].T.astype(o_ref.dtype))

        for step in range(len(tiles) + 2):
            if step < len(tiles):
                logits(step)
            if 1 <= step <= len(tiles):
                softmax(step - 1)
            if step >= 2:
                pv(step - 2)

    head = pl.BlockSpec((1, 1, s, _HEAD_DIM), lambda bi, h: (bi, h, 0, 0))
    return pl.pallas_call(
        body,
        grid=(b, n_heads),
        in_specs=[head, head, head,
                  pl.BlockSpec((1, 1, s // t, _VT_ROWS, t), lambda bi, h: (bi, h, 0, 0, 0))],
        out_specs=pl.BlockSpec((1, s, _HEAD_DIM), lambda bi, h: (bi, 0, h)),
        out_shape=jax.ShapeDtypeStruct((b, s, n_heads * _HEAD_DIM), _CDT),
        scratch_shapes=[pltpu.VMEM((s, 2 * _HEAD_DIM), _CDT),
                        pltpu.VMEM((2, t, t), jnp.float32),
                        pltpu.VMEM((2, t, t), _CDT),
                        pltpu.VMEM((2, 1, t), jnp.float32),
                        pltpu.VMEM((1, t), jnp.float32),
                        pltpu.VMEM((_VT_ROWS, t), jnp.float32)],
        compiler_params=_params("arbitrary", "arbitrary"),
        name="fox_core",
    )(q, k, caug, vt)


def _rope_tables(s):
    half = _HEAD_DIM // 2
    inv = _ROPE_THETA ** (-jnp.arange(half, dtype=jnp.float32) / half)
    ang = jnp.arange(s, dtype=jnp.float32)[:, None] * inv[None, :]
    cos, sin = jnp.cos(ang), jnp.sin(ang)
    return jnp.concatenate([cos, cos], axis=1), jnp.concatenate([-sin, sin], axis=1)


def _placed(cols, s, tn_cap=None):
    tm, tn = _proj_tiles(s, cols)
    if tn_cap is not None:
        tn = min(tn, tn_cap)
    if not cols.aligned(tn):
        cols = cols.materialize()
        tm, tn = _proj_tiles(s, cols)
        if tn_cap is not None:
            tn = min(tn, tn_cap)
    return cols, tm, tn


def _heads_call(a, cols, b, s, *, rope, scale, cos, sin, name):
    cols, tm, tn = _placed(cols, s)
    nsb = s // tm
    extras, extra_specs = (), ()
    if rope:
        extras = (cos, sin)
        extra_specs = (pl.BlockSpec((tm, _HEAD_DIM), lambda j, i: (i % nsb, 0)),) * 2
    nh_t = tn // _HEAD_DIM
    return _proj(
        a, cols, functools.partial(_ep_heads, rope=rope, scale=scale), extras, extra_specs,
        jax.ShapeDtypeStruct((b, cols.n // _HEAD_DIM, s, _HEAD_DIM), _CDT),
        pl.BlockSpec((1, nh_t, tm, _HEAD_DIM), lambda j, i: (i // nsb, j, i % nsb, 0)),
        tm, tn, name)


def _vt_call(a, cols, b, s, t, name):
    cols, tm, tn = _placed(cols, s)
    nsb = s // tm
    nh_t = tn // _HEAD_DIM
    return _proj(
        a, cols, functools.partial(_ep_vt, t=t), (), (),
        jax.ShapeDtypeStruct((b, cols.n // _HEAD_DIM, s // t, _VT_ROWS, t), _CDT),
        pl.BlockSpec((1, nh_t, tm // t, _VT_ROWS, t), lambda j, i: (i // nsb, j, i % nsb, 0, 0)),
        tm, tn, name)


def _gate_call(a, cols, b, s, groups, name):
    c = cols.n // groups
    cols, tm, tn = _placed(cols, s, tn_cap=c)
    nsb = s // tm
    per = c // tn
    return _proj(
        a, cols, _ep_silu, (), (),
        jax.ShapeDtypeStruct((b, groups, s, c), _CDT),
        pl.BlockSpec((1, 1, tm, tn), lambda j, i: (i // nsb, j // per, i % nsb, j % per)),
        tm, tn, name)


def _dsa_layer(xc, w_in_t, q_norm_g, w_uq, kidx_g, kidx_b, w_out, layer, b, s):
    d = xc.shape[1]
    q_lora = w_uq.shape[1]
    branch = w_out.shape[1]
    idx_heads = (w_uq.shape[2] - branch) // _HEAD_DIM
    kv = (w_in_t.shape[1] - q_lora - _HEAD_DIM - idx_heads - branch) // 2
    w_in = functools.partial(_Cols, w_in_t, layer=layer, transposed=True)
    n_kv = kv // _HEAD_DIM
    n_heads = branch // _HEAD_DIM
    assert n_heads == n_kv * _GROUP and idx_heads <= _LANES
    topk = min(_TOPK_MAX, s // 4)
    o0 = q_lora
    o1 = o0 + 2 * kv
    o2 = o1 + _HEAD_DIM
    o3 = o2 + idx_heads
    cos, sin = _rope_tables(s)
    tk1 = min(256, s)
    tk = min(512, s)

    c_cq = w_in(0, q_lora).materialize()
    tm = min(512, s)
    cq = _proj(xc, c_cq, _ep_rms, (q_norm_g[layer].reshape(1, q_lora),),
               (pl.BlockSpec((1, q_lora), lambda j, i: (0, 0)),),
               jax.ShapeDtypeStruct((b * s, q_lora), _CDT),
               pl.BlockSpec((tm, q_lora), lambda j, i: (i, 0)), tm, q_lora, "dsa_cq")
    k = _heads_call(xc, w_in(o0, kv), b, s, rope=True, scale=1.0, cos=cos, sin=sin, name="dsa_k")
    vt = _vt_call(xc, w_in(o0 + kv, kv), b, s, tk, "dsa_v")

    c_ki = w_in(o1, 2 * _LANES)
    if not c_ki.aligned(2 * _LANES):
        w_ki = jnp.pad(w_in_t[layer][o1:o3, :], ((0, _LANES - idx_heads), (0, 0)))
        c_ki = _Cols(w_ki.astype(_CDT), transposed=True)
    nsb = s // tm
    rope_specs = (pl.BlockSpec((tm, _HEAD_DIM), lambda j, i: (i % nsb, 0)),) * 2
    w_scale = idx_heads ** -0.5 * _HEAD_DIM ** -0.5
    vec = pl.BlockSpec((1, _HEAD_DIM), lambda j, i: (0, 0))
    kidx, wt = _proj(xc, c_ki, functools.partial(_ep_kidx, idx_heads=idx_heads, w_scale=w_scale),
                     (cos, sin, kidx_g[layer].reshape(1, _HEAD_DIM), kidx_b[layer].reshape(1, _HEAD_DIM)),
                     rope_specs + (vec, vec),
                     [jax.ShapeDtypeStruct((b, s, _HEAD_DIM), _CDT),
                      jax.ShapeDtypeStruct((b, idx_heads, s), jnp.float32)],
                     [pl.BlockSpec((1, tm, _HEAD_DIM), lambda j, i: (i // nsb, i % nsb, 0)),
                      pl.BlockSpec((1, idx_heads, tm), lambda j, i: (i // nsb, 0, i % nsb))],
                     tm, 2 * _LANES, "dsa_kidx")
    gate = _gate_call(xc, w_in(o3, branch), b, s, n_kv, "dsa_gate")
    q = _heads_call(cq, _Cols(w_uq, 0, branch, layer), b, s, rope=True,
                    scale=_HEAD_DIM ** -0.5 * _LOG2E, cos=cos, sin=sin, name="dsa_q")
    qi = _heads_call(cq, _Cols(w_uq, branch, None, layer), b, s, rope=True, scale=1.0, cos=cos, sin=sin,
                     name="dsa_qi")
    o = _dsa_core(q, qi, wt, kidx, k, vt, topk=topk, tk1=tk1, tk=tk, hchunk=min(8, idx_heads))
    return _outproj(o, gate, w_out[layer].astype(_CDT), min(512, s), min(1024, d), "dsa_out")


def _fox_layer(xc, w_in_t, forget_bias, w_out, layer, b, s):
    d = xc.shape[1]
    branch = w_out.shape[1]
    n_heads = forget_bias.shape[1]
    assert branch == n_heads * _HEAD_DIM and n_heads <= _LANES
    w_in = functools.partial(_Cols, w_in_t, layer=layer, transposed=True)
    w_f = jnp.pad(w_in_t[layer][4 * branch:, :], ((0, _LANES - n_heads), (0, 0)))
    fb = jnp.pad(forget_bias[layer], (0, _LANES - n_heads)).reshape(1, _LANES)
    tm = min(512, s)
    nsb = s // tm
    t = min(512, s)
    q = _heads_call(xc, w_in(0, branch), b, s, rope=False,
                    scale=_HEAD_DIM ** -0.5 * _LOG2E, cos=None, sin=None, name="fox_q")
    k = _heads_call(xc, w_in(branch, branch), b, s, rope=False, scale=1.0,
                    cos=None, sin=None, name="fox_k")
    vt = _vt_call(xc, w_in(2 * branch, branch), b, s, t, "fox_v")
    gate = _gate_call(xc, w_in(3 * branch, branch), b, s, 1, "fox_gate")
    lf = _proj(xc, _Cols(w_f, transposed=True), _ep_logf, (fb,), (pl.BlockSpec((1, _LANES), lambda j, i: (0, 0)),),
               jax.ShapeDtypeStruct((b, s, _LANES), jnp.float32),
               pl.BlockSpec((1, tm, _LANES), lambda j, i: (i // nsb, i % nsb, 0)),
               tm, _LANES, "fox_logf")
    caug = _cumsum_aug(lf, n_heads)
    o = _fox_core(q, k, caug, vt, t=t)
    return _outproj(o.reshape(b, 1, s, branch), gate, w_out[layer].astype(_CDT), tm, min(1024, d),
                    "fox_out")


def kernel(x, a_w_in, a_q_norm_g, a_w_uq, a_kidx_norm_g, a_kidx_norm_b, a_w_out,
           b_w_in, b_forget_bias, b_w_out, ln_g, ln_b):
    b, s, d = x.shape
    depth = ln_g.shape[0]
    alpha = (2 * depth) ** 0.25
    xf = x.reshape(b * s, d)
    xc = xf.astype(_CDT)
    a_w_in_t = jnp.swapaxes(a_w_in, 1, 2)
    b_w_in_t = jnp.swapaxes(b_w_in, 1, 2)
    tm_ln = min(256, s)
    for i in range(depth):
        j = i // 2
        if i % 2 == 0:
            h = _dsa_layer(xc, a_w_in_t, a_q_norm_g, a_w_uq, a_kidx_norm_g, a_kidx_norm_b, a_w_out, j, b, s)
        else:
            h = _fox_layer(xc, b_w_in_t, b_forget_bias, b_w_out, j, b, s)
        last = i == depth - 1
        outs = _resid_ln(h, xf, ln_g[i], ln_b[i], alpha, tm_ln, not last, "resid_ln_%d" % i)
        xf = outs[0]
        if not last:
            xc = outs[1]
    return xf.reshape(b, s, d)
```

```python
import functools
import math

import jax
import jax.numpy as jnp
from jax import lax
from jax.experimental import pallas as pl
from jax.experimental.pallas import tpu as pltpu

_CDT = jnp.bfloat16
_HEAD_DIM = 128
_GROUP = 8
_TOPK_MAX = 256
_ROPE_THETA = 10000.0
_LN_EPS = 1e-5
_RMS_EPS = 1e-6
_LANES = 128
_SUBLANES = 8
_MXU_DIM = 256
_VMEM_LIMIT = 56 * 1024 * 1024
_VT_ROWS = _HEAD_DIM + 16
_INT_MIN = -2 ** 31
_MASKED = -1e30
_M_INIT = -1e29
_LOG2E = math.log2(math.e)
_ROW_CHUNK = 256

_NT = (((1,), (1,)), ((), ()))


def _params(*sem):
    return pltpu.CompilerParams(dimension_semantics=sem, vmem_limit_bytes=_VMEM_LIMIT)


def _rope_tile(x, cos, sin_signed):
    return x * cos + pltpu.roll(x, _HEAD_DIM // 2, axis=1) * sin_signed


class _Cols:
    def __init__(self, w, col0=0, n=None, layer=None, transposed=False):
        self.w, self.col0, self.layer, self.transposed = w, col0, layer, transposed
        self.n_total = w.shape[-2] if transposed else w.shape[-1]
        self.k = w.shape[-1] if transposed else w.shape[-2]
        self.n = self.n_total - col0 if n is None else n

    def block_aligned(self, tn):
        return self.col0 % tn == 0

    def aligned(self, tn):
        start_ok = self.col0 % _SUBLANES == 0 if self.transposed else self.block_aligned(tn)
        return start_ok and self.col0 + -(-self.n // tn) * tn <= self.n_total

    def materialize(self):
        w = self.w if self.layer is None else self.w[self.layer]
        if self.transposed:
            return _Cols(w[self.col0:self.col0 + self.n, :].astype(_CDT), transposed=True)
        return _Cols(w[:, self.col0:self.col0 + self.n].astype(_CDT))


def _proj_tiles(s, cols):
    tn = min(cols.n, 1024)
    while cols.k * tn * cols.w.dtype.itemsize > 8 * 1024 * 1024:
        tn //= 2
    return min(s, 1024 if cols.k * 1024 * 2 * 2 <= 16 * 1024 * 1024 else 512), tn


def _proj(a, cols, epilogue, extras, extra_specs, out_shape, out_specs, tm, tn, name, row_chunk=None):
    m, k = a.shape
    assert cols.aligned(tn) and k == cols.k
    convert = cols.w.dtype != _CDT
    n_extra = len(extras)
    w_block = (tn, k) if cols.transposed else (k, tn)

    def body(a_ref, w_ref, *rest):
        if convert:
            wc_ref = rest[-1]
            rest = rest[:-1]

            @pl.when(pl.program_id(1) == 0)
            def _():
                wc_ref[...] = w_ref[...].reshape(w_block).astype(_CDT)

            w = wc_ref[...]
        else:
            w = w_ref[...].reshape(w_block)
        rc = tm if row_chunk is None else row_chunk
        for r in range(tm // rc):
            rows = slice(r * rc, (r + 1) * rc)
            if cols.transposed:
                res = lax.dot_general(a_ref[rows, :], w, _NT, preferred_element_type=jnp.float32)
            else:
                res = jnp.dot(a_ref[rows, :], w, preferred_element_type=jnp.float32)
            epilogue(res, rest[:n_extra], rest[n_extra:], rows)

    lead_index = () if cols.layer is None else (cols.layer,)
    if cols.block_aligned(tn):
        c0 = cols.col0 // tn
        lead = () if cols.layer is None else (None,)
        if cols.transposed:
            w_spec = pl.BlockSpec(lead + (tn, k), lambda j, i: lead_index + (c0 + j, 0))
        else:
            w_spec = pl.BlockSpec(lead + (k, tn), lambda j, i: lead_index + (0, c0 + j))
    else:
        assert cols.transposed
        lead = () if cols.layer is None else (pl.Element(1),)
        w_spec = pl.BlockSpec(lead + (pl.Element(tn), pl.Element(k)),
                              lambda j, i: lead_index + (pl.multiple_of(cols.col0 + j * tn, _SUBLANES), 0))
    return pl.pallas_call(
        body,
        grid=(-(-cols.n // tn), m // tm),
        in_specs=[pl.BlockSpec((tm, k), lambda j, i: (i, 0)), w_spec] + list(extra_specs),
        out_specs=out_specs,
        out_shape=out_shape,
        scratch_shapes=[pltpu.VMEM(w_block, _CDT)] if convert else [],
        compiler_params=_params("arbitrary", "arbitrary"),
        name=name,
    )(a, cols.w, *extras)


def _ep_rms(res, extras, outs, rows):
    (g_ref,) = extras
    (o_ref,) = outs
    y = res * lax.rsqrt(jnp.mean(res * res, axis=-1, keepdims=True) + _RMS_EPS)
    o_ref[rows, :] = (y * g_ref[...]).astype(o_ref.dtype)


def _ep_heads(res, extras, outs, rows, *, rope, scale):
    (o_ref,) = outs
    nh = res.shape[1] // _HEAD_DIM
    if rope:
        cos = extras[0][rows, :]
        sin = extras[1][rows, :]
    for h in range(nh):
        xh = res[:, h * _HEAD_DIM:(h + 1) * _HEAD_DIM]
        if rope:
            xh = _rope_tile(xh, cos, sin)
        if scale != 1.0:
            xh = xh * scale
        o_ref[0, h, rows, :] = xh.astype(o_ref.dtype)


def _ep_vt(res, extras, outs, rows, *, t):
    (vt_ref,) = outs
    assert rows.start % t == 0 and res.shape[0] % t == 0
    for h in range(res.shape[1] // _HEAD_DIM):
        vh = res[:, h * _HEAD_DIM:(h + 1) * _HEAD_DIM]
        for c in range(res.shape[0] // t):
            dst = rows.start // t + c
            vt_ref[0, h, dst, :_HEAD_DIM, :] = vh[c * t:(c + 1) * t, :].T.astype(vt_ref.dtype)
            vt_ref[0, h, dst, _HEAD_DIM:, :] = jnp.ones((_VT_ROWS - _HEAD_DIM, t), vt_ref.dtype)


def _ep_kidx(res, extras, outs, rows, *, idx_heads, w_scale):
    cos_ref, sin_ref, g_ref, b_ref = extras
    kidx_ref, wt_ref = outs
    x = res[:, :_HEAD_DIM]
    mu = jnp.mean(x, axis=-1, keepdims=True)
    xc = x - mu
    var = jnp.mean(xc * xc, axis=-1, keepdims=True)
    y = xc * lax.rsqrt(var + _LN_EPS) * g_ref[...] + b_ref[...]
    kidx_ref[0] = _rope_tile(y, cos_ref[...], sin_ref[...]).astype(kidx_ref.dtype)
    wt = (res[:, _HEAD_DIM:] * w_scale).T
    wt_ref[0] = wt[:idx_heads, :]


def _ep_silu(res, extras, outs, rows):
    (o_ref,) = outs
    o_ref[0, 0, rows, :] = (res * jax.nn.sigmoid(res)).astype(o_ref.dtype)


def _ep_logf(res, extras, outs, rows):
    (bias_ref,) = extras
    (o_ref,) = outs
    z = res + bias_ref[...]
    o_ref[0] = jnp.minimum(z, 0.0) - jnp.log(1.0 + jnp.exp(-jnp.abs(z)))


def _outproj(o, gate, w, tm, tn, name):
    b, g, s, c = o.shape
    d = w.shape[1]
    nsb = s // tm

    def body(o_ref, g_ref, w_ref, out_ref):
        acc = None
        for gi in range(g):
            a = (o_ref[0, gi].astype(jnp.float32) * g_ref[0, gi].astype(jnp.float32)).astype(_CDT)
            part = jnp.dot(a, w_ref[gi * c:(gi + 1) * c, :], preferred_element_type=jnp.float32)
            acc = part if acc is None else acc + part
        out_ref[...] = acc

    return pl.pallas_call(
        body,
        grid=(d // tn, b * nsb),
        in_specs=[pl.BlockSpec((1, g, tm, c), lambda j, i: (i // nsb, 0, i % nsb, 0)),
                  pl.BlockSpec((1, g, tm, c), lambda j, i: (i // nsb, 0, i % nsb, 0)),
                  pl.BlockSpec((g * c, tn), lambda j, i: (0, j))],
        out_specs=pl.BlockSpec((tm, tn), lambda j, i: (i, j)),
        out_shape=jax.ShapeDtypeStruct((b * s, d), jnp.float32),
        compiler_params=_params("arbitrary", "arbitrary"),
        name=name,
    )(o, gate, w)


def _resid_ln(h, x, g, b, alpha, tm, with_cdt, name):
    m, d = x.shape

    def body(h_ref, x_ref, g_ref, b_ref, *outs):
        y = alpha * x_ref[...] + h_ref[...]
        mu = jnp.mean(y, axis=-1, keepdims=True)
        yc = y - mu
        var = jnp.mean(yc * yc, axis=-1, keepdims=True)
        z = yc * lax.rsqrt(var + _LN_EPS) * g_ref[...] + b_ref[...]
        outs[0][...] = z
        if with_cdt:
            outs[1][...] = z.astype(_CDT)

    row = pl.BlockSpec((tm, d), lambda i: (i, 0))
    vec = pl.BlockSpec((1, d), lambda i: (0, 0))
    out_shape = [jax.ShapeDtypeStruct((m, d), jnp.float32)]
    out_specs = [row]
    if with_cdt:
        out_shape.append(jax.ShapeDtypeStruct((m, d), _CDT))
        out_specs.append(row)
    return pl.pallas_call(
        body,
        grid=(m // tm,),
        in_specs=[row, row, vec, vec],
        out_specs=out_specs,
        out_shape=out_shape,
        compiler_params=_params("arbitrary"),
        name=name,
    )(h, x, g.reshape(1, d), b.reshape(1, d))


def _col_reduce(x, op):
    rows, n = x.shape
    fold = 8 * _SUBLANES
    if rows > fold and rows % fold == 0:
        x = op(x.reshape(rows // fold, fold, n), axis=0)
    return op(x, axis=0, keepdims=True)


def _softmax_step(sc, m_ref, cols):
    m_old = m_ref[:, cols]
    m_new = jnp.maximum(m_old, _col_reduce(sc, jnp.max))
    m_ref[:, cols] = m_new
    return jnp.exp2(sc - m_new).astype(_CDT), jnp.exp2(m_old - m_new)


def _pv_step(vta, p, alpha, acc_ref, cols):
    pv = jnp.dot(vta, p, preferred_element_type=jnp.float32)
    acc_ref[:, cols] = pv if alpha is None else alpha * acc_ref[:, cols] + pv


def _float_sort_key(x):
    bits = pltpu.bitcast(x, jnp.int32)
    return bits ^ ((bits >> 31) & jnp.int32(0x7FFFFFFF))


def _dsa_core(q, qi, wt, kidx, k, vt, *, topk, tk1, tk, hchunk):
    b, n_heads, s, _ = q.shape
    idx_heads = qi.shape[1]
    n_kv = k.shape[1]
    tq = _LANES
    gq = _GROUP * tq
    log_s = int(math.log2(s))
    assert 1 << log_s == s and s % tk == 0 and tk % tk1 == 0 and tk1 % tq == 0
    assert idx_heads % hchunk == 0 and gq % _MXU_DIM == 0

    def body(q_ref, qi_ref, wt_ref, kidx_ref, k_ref, vt_ref, o_ref,
             key_ref, mask_ref, sc_ref, p_ref, al_ref, m_ref, acc_ref):
        i = pl.program_id(1)
        n1 = (i * tq + tq + tk1 - 1) // tk1
        n3 = (i * tq + tq + tk - 1) // tk

        q_pos1 = i * tq + lax.broadcasted_iota(jnp.int32, (tk1, tq), 1)
        s_iota1 = lax.broadcasted_iota(jnp.int32, (tk1, tq), 0)

        def score_tile(kt, carry):
            ks = pl.multiple_of(kt * tk1, tk1)
            kx = kidx_ref[0, pl.ds(ks, tk1), :]
            acc = jnp.zeros((tk1, tq), jnp.float32)
            for hc in range(idx_heads // hchunk):
                qq = qi_ref[0, hc * hchunk:(hc + 1) * hchunk].reshape(hchunk * tq, _HEAD_DIM)
                d = lax.dot_general(kx, qq, _NT, preferred_element_type=jnp.float32)
                for hh in range(hchunk):
                    h = hc * hchunk + hh
                    acc = acc + jnp.maximum(d[:, hh * tq:(hh + 1) * tq], 0.0) * wt_ref[0, h:h + 1, :]
            key = _float_sort_key(acc + 0.0)
            key_ref[pl.ds(ks, tk1), :] = jnp.where(ks + s_iota1 <= q_pos1, key, _INT_MIN)
            return carry

        lax.fori_loop(0, n1, score_tile, 0)

        def pad_tile(kt, carry):
            key_ref[pl.ds(pl.multiple_of(kt * tk1, tk1), tk1), :] = jnp.full((tk1, tq), _INT_MIN, jnp.int32)
            return carry

        lax.fori_loop(n1, n3 * (tk // tk1), pad_tile, 0)

        q_pos = i * tq + lax.broadcasted_iota(jnp.int32, (tk, tq), 1)
        s_iota = lax.broadcasted_iota(jnp.int32, (tk, tq), 0)

        fold = 8 * _SUBLANES

        def count(pred_fn):
            def tile(kt, cnt):
                ks = pl.multiple_of(kt * tk, tk)
                c = jnp.where(pred_fn(key_ref[pl.ds(ks, tk), :], ks), 1.0, 0.0)
                return cnt + jnp.sum(c.reshape(tk // fold, fold, tq), axis=0)
            cnt = lax.fori_loop(0, n3, tile, jnp.zeros((fold, tq), jnp.float32))
            return jnp.sum(cnt, axis=0, keepdims=True)

        def value_step(t, prefix):
            cand = prefix | jnp.left_shift(jnp.int32(1), 31 - t)
            cand_s = cand ^ _INT_MIN
            cnt = count(lambda key, ks: key >= cand_s)
            return jnp.where(cnt >= topk, cand, prefix)

        prefix = lax.fori_loop(0, 32, value_step, jnp.zeros((1, tq), jnp.int32))
        thr = prefix ^ _INT_MIN

        def tie_break():
            need = topk - count(lambda key, ks: key > thr)

            def index_step(t, j):
                cand = j | jnp.left_shift(jnp.int32(1), log_s - 1 - t)
                cnt = count(lambda key, ks: (key == thr) & (ks + s_iota < cand))
                return jnp.where(cnt < need, cand, j)

            return lax.fori_loop(0, log_s, index_step, jnp.zeros((1, tq), jnp.int32))

        n_ge = count(lambda key, ks: key >= thr)
        j_last = lax.cond(jnp.max(n_ge) > topk, tie_break,
                          lambda: jnp.full((1, tq), s, jnp.int32))

        def mask_tile(kt, carry):
            ks = pl.multiple_of(kt * tk, tk)
            key = key_ref[pl.ds(ks, tk), :]
            s_pos = ks + s_iota
            sel = (key > thr) | ((key == thr) & (s_pos <= j_last))
            mask_ref[pl.ds(ks, tk), :] = jnp.where(sel & (s_pos <= q_pos), 0.0, _MASKED).astype(_CDT)
            return carry

        lax.fori_loop(0, n3, mask_tile, 0)

        r = lax.broadcasted_iota(jnp.int32, (gq, tq), 0)
        c = lax.broadcasted_iota(jnp.int32, (gq, tq), 1)
        one_hot = jnp.where((r & (tq - 1)) == c, 1.0, 0.0).astype(_CDT)

        chunks = [slice(cc * _MXU_DIM, (cc + 1) * _MXU_DIM) for cc in range(gq // _MXU_DIM)]

        def kv_group(g, carry):
            qg = q_ref[0, pl.ds(g * _GROUP, _GROUP)].reshape(gq, _HEAD_DIM)
            qaug = jnp.concatenate([qg, one_hot], axis=1)
            m_ref[...] = jnp.full((1, gq), _M_INIT, jnp.float32)

            def logits(kt):
                kaug = jnp.concatenate([k_ref[0, g, pl.ds(kt * tk, tk), :],
                                        mask_ref[pl.ds(kt * tk, tk), :]], axis=1)
                for cols in chunks:
                    sc_ref[kt & 1, :, cols] = lax.dot_general(kaug, qaug[cols], _NT,
                                                              preferred_element_type=jnp.float32)

            def softmax(kt):
                for cols in chunks:
                    p_ref[kt & 1, :, cols], al_ref[kt & 1, :, cols] = _softmax_step(
                        sc_ref[kt & 1, :, cols], m_ref, cols)

            def pv(kt):
                vta = vt_ref[0, g, kt]
                for cols in chunks:
                    _pv_step(vta, p_ref[kt & 1, :, cols], None if kt == 0 else al_ref[kt & 1, :, cols],
                             acc_ref, cols)

            for n_tiles in range(1, s // tk + 1):
                @pl.when(n3 == n_tiles)
                def _():
                    for step in range(n_tiles + 2):
                        if step < n_tiles:
                            logits(step)
                        if 1 <= step <= n_tiles:
                            softmax(step - 1)
                        if step >= 2:
                            pv(step - 2)

            acc = acc_ref[...]
            ot = acc[:_HEAD_DIM] * (1.0 / acc[_HEAD_DIM:_HEAD_DIM + 1])
            for hh in range(_GROUP):
                o_ref[0, g, :, hh * _HEAD_DIM:(hh + 1) * _HEAD_DIM] = (
                    ot[:, hh * tq:(hh + 1) * tq].T.astype(o_ref.dtype))
            return carry

        lax.fori_loop(0, n_kv, kv_group, 0)

    nq = s // tq
    return pl.pallas_call(
        body,
        grid=(b, nq),
        in_specs=[pl.BlockSpec((1, n_heads, tq, _HEAD_DIM), lambda bi, i: (bi, 0, i, 0)),
                  pl.BlockSpec((1, idx_heads, tq, _HEAD_DIM), lambda bi, i: (bi, 0, i, 0)),
                  pl.BlockSpec((1, idx_heads, tq), lambda bi, i: (bi, 0, i)),
                  pl.BlockSpec((1, s, _HEAD_DIM), lambda bi, i: (bi, 0, 0)),
                  pl.BlockSpec((1, n_kv, s, _HEAD_DIM), lambda bi, i: (bi, 0, 0, 0)),
                  pl.BlockSpec((1, n_kv, s // tk, _VT_ROWS, tk), lambda bi, i: (bi, 0, 0, 0, 0))],
        out_specs=pl.BlockSpec((1, n_kv, tq, _GROUP * _HEAD_DIM), lambda bi, i: (bi, 0, i, 0)),
        out_shape=jax.ShapeDtypeStruct((b, n_kv, s, _GROUP * _HEAD_DIM), _CDT),
        scratch_shapes=[pltpu.VMEM((s, tq), jnp.int32),
                        pltpu.VMEM((s, tq), _CDT),
                        pltpu.VMEM((2, tk, gq), jnp.float32),
                        pltpu.VMEM((2, tk, gq), _CDT),
                        pltpu.VMEM((2, 1, gq), jnp.float32),
                        pltpu.VMEM((1, gq), jnp.float32),
                        pltpu.VMEM((_VT_ROWS, gq), jnp.float32)],
        compiler_params=_params("arbitrary", "arbitrary"),
        name="dsa_core",
    )(q, qi, wt, kidx, k, vt)


def _cumsum_aug(lf, n_heads):
    b, s, w = lf.shape
    blk = _LANES

    def body(lf_ref, out_ref, carry_ref):
        @pl.when(pl.program_id(1) == 0)
        def _():
            carry_ref[...] = jnp.zeros_like(carry_ref)
        r = lax.broadcasted_iota(jnp.int32, (blk, blk), 0)
        c = lax.broadcasted_iota(jnp.int32, (blk, blk), 1)
        tri = jnp.where(r >= c, 1.0, 0.0).astype(jnp.float32)
        cs = jnp.dot(tri, lf_ref[0], preferred_element_type=jnp.float32,
                     precision=lax.Precision.HIGHEST) + carry_ref[...]
        carry_ref[...] = cs[blk - 1:blk, :]
        c2 = cs * _LOG2E
        for h in range(n_heads):
            col = c2[:, h:h + 1]
            hi = col.astype(_CDT).astype(jnp.float32)
            mid = (col - hi).astype(_CDT).astype(jnp.float32)
            lo = (col - hi - mid).astype(_CDT).astype(jnp.float32)
            tile = jnp.where(c == 0, hi, jnp.where(c == 1, mid, jnp.where(c == 2, lo, 0.0)))
            out_ref[0, h] = tile.astype(out_ref.dtype)

    return pl.pallas_call(
        body,
        grid=(b, s // blk),
        in_specs=[pl.BlockSpec((1, blk, w), lambda bi, i: (bi, i, 0))],
        out_specs=pl.BlockSpec((1, n_heads, blk, _LANES), lambda bi, i: (bi, 0, i, 0)),
        out_shape=jax.ShapeDtypeStruct((b, n_heads, s, _LANES), _CDT),
        scratch_shapes=[pltpu.VMEM((1, w), jnp.float32)],
        compiler_params=_params("arbitrary", "arbitrary"),
        name="fox_cumsum",
    )(lf)


def _fox_core(q, k, caug, vt, *, t):
    b, n_heads, s, _ = q.shape
    assert t % _MXU_DIM == 0 and s % t == 0

    chunks = [slice(cc * _MXU_DIM, (cc + 1) * _MXU_DIM) for cc in range(t // _MXU_DIM)]

    tiles = [(i, j) for i in range(s // t) for j in range(i + 1)]

    def body(q_ref, k_ref, c_ref, vt_ref, o_ref, kaug_ref, sc_ref, p_ref, al_ref, m_ref, acc_ref):
        kaug_ref[:, :_HEAD_DIM] = k_ref[0, 0]
        kaug_ref[:, _HEAD_DIM:] = c_ref[0, 0]
        lane = lax.broadcasted_iota(jnp.int32, (t, _HEAD_DIM), 1)
        minus_ones = jnp.where(lane < 3, -1.0, 0.0).astype(_CDT)

        def logits(n):
            i, j = tiles[n]
            qaug = jnp.concatenate([q_ref[0, 0, i * t:(i + 1) * t, :], minus_ones], axis=1)
            ka = kaug_ref[j * t:(j + 1) * t, :]
            for cols in chunks:
                sc_ref[n & 1, :, cols] = lax.dot_general(ka, qaug[cols], _NT,
                                                         preferred_element_type=jnp.float32)

        def softmax(n):
            i, j = tiles[n]
            if j == 0:
                m_ref[...] = jnp.full((1, t), _M_INIT, jnp.float32)
            for cols in chunks:
                sc = sc_ref[n & 1, :, cols]
                if i == j:
                    key_i = lax.broadcasted_iota(jnp.int32, (t, _MXU_DIM), 0)
                    qry_i = cols.start + lax.broadcasted_iota(jnp.int32, (t, _MXU_DIM), 1)
                    sc = jnp.where(key_i <= qry_i, sc, -jnp.inf)
                p_ref[n & 1, :, cols], al_ref[n & 1, :, cols] = _softmax_step(sc, m_ref, cols)

        def pv(n):
            i, j = tiles[n]
            vta = vt_ref[0, 0, j]
            for cols in chunks:
                _pv_step(vta, p_ref[n & 1, :, cols], None if j == 0 else al_ref[n & 1, :, cols],
                         acc_ref, cols)
            if j == i:
                acc = acc_ref[...]
                ot = acc[:_HEAD_DIM] * (1.0 / acc[_HEAD_DIM:_HEAD_DIM + 1])
                for cq in range(t // _LANES):
                    o_ref[0, i * t + cq * _LANES:i * t + (cq + 1) * _LANES, :] = (
                        ot[:, cq * _LANES:(cq + 1) * _LANES].T.astype(o_ref.dtype))

        for step in range(len(tiles) + 2):
            if step < len(tiles):
                logits(step)
            if 1 <= step <= len(tiles):
                softmax(step - 1)
            if step >= 2:
                pv(step - 2)

    head = pl.BlockSpec((1, 1, s, _HEAD_DIM), lambda bi, h: (bi, h, 0, 0))
    return pl.pallas_call(
        body,
        grid=(b, n_heads),
        in_specs=[head, head, head,
                  pl.BlockSpec((1, 1, s // t, _VT_ROWS, t), lambda bi, h: (bi, h, 0, 0, 0))],
        out_specs=pl.BlockSpec((1, s, _HEAD_DIM), lambda bi, h: (bi, 0, h)),
        out_shape=jax.ShapeDtypeStruct((b, s, n_heads * _HEAD_DIM), _CDT),
        scratch_shapes=[pltpu.VMEM((s, 2 * _HEAD_DIM), _CDT),
                        pltpu.VMEM((2, t, t), jnp.float32),
                        pltpu.VMEM((2, t, t), _CDT),
                        pltpu.VMEM((2, 1, t), jnp.float32),
                        pltpu.VMEM((1, t), jnp.float32),
                        pltpu.VMEM((_VT_ROWS, t), jnp.float32)],
        compiler_params=_params("arbitrary", "arbitrary"),
        name="fox_core",
    )(q, k, caug, vt)


def _rope_tables(s):
    half = _HEAD_DIM // 2
    inv = _ROPE_THETA ** (-jnp.arange(half, dtype=jnp.float32) / half)
    ang = jnp.arange(s, dtype=jnp.float32)[:, None] * inv[None, :]
    cos, sin = jnp.cos(ang), jnp.sin(ang)
    return jnp.concatenate([cos, cos], axis=1), jnp.concatenate([-sin, sin], axis=1)


def _placed(cols, s, tn_cap=None):
    tm, tn = _proj_tiles(s, cols)
    if tn_cap is not None:
        tn = min(tn, tn_cap)
    if not cols.aligned(tn):
        cols = cols.materialize()
        tm, tn = _proj_tiles(s, cols)
        if tn_cap is not None:
            tn = min(tn, tn_cap)
    return cols, tm, tn


def _heads_call(a, cols, b, s, *, rope, scale, cos, sin, name):
    cols, tm, tn = _placed(cols, s)
    nsb = s // tm
    extras, extra_specs = (), ()
    if rope:
        extras = (cos, sin)
        extra_specs = (pl.BlockSpec((tm, _HEAD_DIM), lambda j, i: (i % nsb, 0)),) * 2
    nh_t = tn // _HEAD_DIM
    return _proj(
        a, cols, functools.partial(_ep_heads, rope=rope, scale=scale), extras, extra_specs,
        jax.ShapeDtypeStruct((b, cols.n // _HEAD_DIM, s, _HEAD_DIM), _CDT),
        pl.BlockSpec((1, nh_t, tm, _HEAD_DIM), lambda j, i: (i // nsb, j, i % nsb, 0)),
        tm, tn, name, row_chunk=min(_ROW_CHUNK, tm))


def _vt_call(a, cols, b, s, t, name):
    cols, tm, tn = _placed(cols, s)
    nsb = s // tm
    nh_t = tn // _HEAD_DIM
    return _proj(
        a, cols, functools.partial(_ep_vt, t=t), (), (),
        jax.ShapeDtypeStruct((b, cols.n // _HEAD_DIM, s // t, _VT_ROWS, t), _CDT),
        pl.BlockSpec((1, nh_t, tm // t, _VT_ROWS, t), lambda j, i: (i // nsb, j, i % nsb, 0, 0)),
        tm, tn, name, row_chunk=t)


def _gate_call(a, cols, b, s, groups, name):
    c = cols.n // groups
    cols, tm, tn = _placed(cols, s, tn_cap=c)
    nsb = s // tm
    per = c // tn
    return _proj(
        a, cols, _ep_silu, (), (),
        jax.ShapeDtypeStruct((b, groups, s, c), _CDT),
        pl.BlockSpec((1, 1, tm, tn), lambda j, i: (i // nsb, j // per, i % nsb, j % per)),
        tm, tn, name, row_chunk=min(_ROW_CHUNK, tm))


def _dsa_layer(xc, w_in_t, q_norm_g, w_uq, kidx_g, kidx_b, w_out, layer, b, s):
    d = xc.shape[1]
    q_lora = w_uq.shape[1]
    branch = w_out.shape[1]
    idx_heads = (w_uq.shape[2] - branch) // _HEAD_DIM
    kv = (w_in_t.shape[1] - q_lora - _HEAD_DIM - idx_heads - branch) // 2
    w_in = functools.partial(_Cols, w_in_t, layer=layer, transposed=True)
    n_kv = kv // _HEAD_DIM
    n_heads = branch // _HEAD_DIM
    assert n_heads == n_kv * _GROUP and idx_heads <= _LANES
    topk = min(_TOPK_MAX, s // 4)
    o0 = q_lora
    o1 = o0 + 2 * kv
    o2 = o1 + _HEAD_DIM
    o3 = o2 + idx_heads
    cos, sin = _rope_tables(s)
    tk1 = min(256, s)
    tk = min(512, s)

    c_cq = w_in(0, q_lora).materialize()
    tm = min(512, s)
    cq = _proj(xc, c_cq, _ep_rms, (q_norm_g[layer].reshape(1, q_lora),),
               (pl.BlockSpec((1, q_lora), lambda j, i: (0, 0)),),
               jax.ShapeDtypeStruct((b * s, q_lora), _CDT),
               pl.BlockSpec((tm, q_lora), lambda j, i: (i, 0)), tm, q_lora, "dsa_cq")
    k = _heads_call(xc, w_in(o0, kv), b, s, rope=True, scale=1.0, cos=cos, sin=sin, name="dsa_k")
    vt = _vt_call(xc, w_in(o0 + kv, kv), b, s, tk, "dsa_v")

    c_ki = w_in(o1, 2 * _LANES)
    if not c_ki.aligned(2 * _LANES):
        w_ki = jnp.pad(w_in_t[layer][o1:o3, :], ((0, _LANES - idx_heads), (0, 0)))
        c_ki = _Cols(w_ki.astype(_CDT), transposed=True)
    nsb = s // tm
    rope_specs = (pl.BlockSpec((tm, _HEAD_DIM), lambda j, i: (i % nsb, 0)),) * 2
    w_scale = idx_heads ** -0.5 * _HEAD_DIM ** -0.5
    vec = pl.BlockSpec((1, _HEAD_DIM), lambda j, i: (0, 0))
    kidx, wt = _proj(xc, c_ki, functools.partial(_ep_kidx, idx_heads=idx_heads, w_scale=w_scale),
                     (cos, sin, kidx_g[layer].reshape(1, _HEAD_DIM), kidx_b[layer].reshape(1, _HEAD_DIM)),
                     rope_specs + (vec, vec),
                     [jax.ShapeDtypeStruct((b, s, _HEAD_DIM), _CDT),
                      jax.ShapeDtypeStruct((b, idx_heads, s), jnp.float32)],
                     [pl.BlockSpec((1, tm, _HEAD_DIM), lambda j, i: (i // nsb, i % nsb, 0)),
                      pl.BlockSpec((1, idx_heads, tm), lambda j, i: (i // nsb, 0, i % nsb))],
                     tm, 2 * _LANES, "dsa_kidx")
    gate = _gate_call(xc, w_in(o3, branch), b, s, n_kv, "dsa_gate")
    q = _heads_call(cq, _Cols(w_uq, 0, branch, layer), b, s, rope=True,
                    scale=_HEAD_DIM ** -0.5 * _LOG2E, cos=cos, sin=sin, name="dsa_q")
    qi = _heads_call(cq, _Cols(w_uq, branch, None, layer), b, s, rope=True, scale=1.0, cos=cos, sin=sin,
                     name="dsa_qi")
    o = _dsa_core(q, qi, wt, kidx, k, vt, topk=topk, tk1=tk1, tk=tk, hchunk=min(8, idx_heads))
    return _outproj(o, gate, w_out[layer].astype(_CDT), min(512, s), min(1024, d), "dsa_out")


def _fox_layer(xc, w_in_t, forget_bias, w_out, layer, b, s):
    d = xc.shape[1]
    branch = w_out.shape[1]
    n_heads = forget_bias.shape[1]
    assert branch == n_heads * _HEAD_DIM and n_heads <= _LANES
    w_in = functools.partial(_Cols, w_in_t, layer=layer, transposed=True)
    w_f = jnp.pad(w_in_t[layer][4 * branch:, :], ((0, _LANES - n_heads), (0, 0)))
    fb = jnp.pad(forget_bias[layer], (0, _LANES - n_heads)).reshape(1, _LANES)
    tm = min(512, s)
    nsb = s // tm
    t = min(512, s)
    q = _heads_call(xc, w_in(0, branch), b, s, rope=False,
                    scale=_HEAD_DIM ** -0.5 * _LOG2E, cos=None, sin=None, name="fox_q")
    k = _heads_call(xc, w_in(branch, branch), b, s, rope=False, scale=1.0,
                    cos=None, sin=None, name="fox_k")
    vt = _vt_call(xc, w_in(2 * branch, branch), b, s, t, "fox_v")
    gate = _gate_call(xc, w_in(3 * branch, branch), b, s, 1, "fox_gate")
    lf = _proj(xc, _Cols(w_f, transposed=True), _ep_logf, (fb,), (pl.BlockSpec((1, _LANES), lambda j, i: (0, 0)),),
               jax.ShapeDtypeStruct((b, s, _LANES), jnp.float32),
               pl.BlockSpec((1, tm, _LANES), lambda j, i: (i // nsb, i % nsb, 0)),
               tm, _LANES, "fox_logf")
    caug = _cumsum_aug(lf, n_heads)
    o = _fox_core(q, k, caug, vt, t=t)
    return _outproj(o.reshape(b, 1, s, branch), gate, w_out[layer].astype(_CDT), tm, min(1024, d),
                    "fox_out")


def kernel(x, a_w_in, a_q_norm_g, a_w_uq, a_kidx_norm_g, a_kidx_norm_b, a_w_out,
           b_w_in, b_forget_bias, b_w_out, ln_g, ln_b):
    b, s, d = x.shape
    depth = ln_g.shape[0]
    alpha = (2 * depth) ** 0.25
    xf = x.reshape(b * s, d)
    xc = xf.astype(_CDT)
    a_w_in_t = jnp.swapaxes(a_w_in, 1, 2)
    b_w_in_t = jnp.swapaxes(b_w_in, 1, 2)
    tm_ln = min(256, s)
    for i in range(depth):
        j = i // 2
        if i % 2 == 0:
            h = _dsa_layer(xc, a_w_in_t, a_q_norm_g, a_w_uq, a_kidx_norm_g, a_kidx_norm_b, a_w_out, j, b, s)
        else:
            h = _fox_layer(xc, b_w_in_t, b_forget_bias, b_w_out, j, b, s)
        last = i == depth - 1
        outs = _resid_ln(h, xf, ln_g[i], ln_b[i], alpha, tm_ln, not last, "resid_ln_%d" % i)
        xf = outs[0]
        if not last:
            xc = outs[1]
    return xf.reshape(b, s, d)
```

```python
import functools
import math

import jax
import jax.numpy as jnp
from jax import lax
from jax.experimental import pallas as pl
from jax.experimental.pallas import tpu as pltpu

_CDT = jnp.bfloat16
_HEAD_DIM = 128
_GROUP = 8
_TOPK_MAX = 256
_ROPE_THETA = 10000.0
_LN_EPS = 1e-5
_RMS_EPS = 1e-6
_LANES = 128
_SUBLANES = 8
_MXU_DIM = 256
_VMEM_LIMIT = 56 * 1024 * 1024
_PROJ_TILE_BUDGET = 40 * 1024 * 1024
_VT_ROWS = _HEAD_DIM + 16
_INT_MIN = -2 ** 31
_MASKED = -1e30
_M_INIT = -1e29
_LOG2E = math.log2(math.e)
_ROW_CHUNK = 256

_NT = (((1,), (1,)), ((), ()))


def _params(*sem):
    return pltpu.CompilerParams(dimension_semantics=sem, vmem_limit_bytes=_VMEM_LIMIT)


def _rope_tile(x, cos, sin_signed):
    return x * cos + pltpu.roll(x, _HEAD_DIM // 2, axis=1) * sin_signed


class _Cols:
    def __init__(self, w, col0=0, n=None, layer=None, transposed=False):
        self.w, self.col0, self.layer, self.transposed = w, col0, layer, transposed
        self.n_total = w.shape[-2] if transposed else w.shape[-1]
        self.k = w.shape[-1] if transposed else w.shape[-2]
        self.n = self.n_total - col0 if n is None else n

    def block_aligned(self, tn):
        return self.col0 % tn == 0

    def aligned(self, tn):
        start_ok = self.col0 % _SUBLANES == 0 if self.transposed else self.block_aligned(tn)
        return start_ok and self.col0 + -(-self.n // tn) * tn <= self.n_total

    def materialize(self):
        w = self.w if self.layer is None else self.w[self.layer]
        if self.transposed:
            return _Cols(w[self.col0:self.col0 + self.n, :].astype(_CDT), transposed=True)
        return _Cols(w[:, self.col0:self.col0 + self.n].astype(_CDT))


def _proj_tiles(s, cols):
    itemsize = cols.w.dtype.itemsize
    act = jnp.dtype(_CDT).itemsize
    for tn in (1024, 512, 256, 128):
        if tn > cols.n:
            continue
        w_bytes = cols.k * tn * (2 * itemsize + (act if cols.w.dtype != _CDT else 0))
        for tm in (1024, 512, 256):
            if tm <= s and w_bytes + 2 * tm * cols.k * act <= _PROJ_TILE_BUDGET:
                return tm, tn
    raise ValueError("projection tiles do not fit VMEM")


def _proj(a, cols, epilogue, extras, extra_specs, out_shape, out_specs, tm, tn, name, row_chunk=None):
    m, k = a.shape
    assert cols.aligned(tn) and k == cols.k
    convert = cols.w.dtype != _CDT
    n_extra = len(extras)
    w_block = (tn, k) if cols.transposed else (k, tn)

    def body(a_ref, w_ref, *rest):
        if convert:
            wc_ref = rest[-1]
            rest = rest[:-1]

            @pl.when(pl.program_id(1) == 0)
            def _():
                wc_ref[...] = w_ref[...].reshape(w_block).astype(_CDT)

            w = wc_ref[...]
        else:
            w = w_ref[...].reshape(w_block)
        rc = tm if row_chunk is None else row_chunk
        for r in range(tm // rc):
            rows = slice(r * rc, (r + 1) * rc)
            if cols.transposed:
                res = lax.dot_general(a_ref[rows, :], w, _NT, preferred_element_type=jnp.float32)
            else:
                res = jnp.dot(a_ref[rows, :], w, preferred_element_type=jnp.float32)
            epilogue(res, rest[:n_extra], rest[n_extra:], rows)

    lead_index = () if cols.layer is None else (cols.layer,)
    if cols.block_aligned(tn):
        c0 = cols.col0 // tn
        lead = () if cols.layer is None else (None,)
        if cols.transposed:
            w_spec = pl.BlockSpec(lead + (tn, k), lambda j, i: lead_index + (c0 + j, 0))
        else:
            w_spec = pl.BlockSpec(lead + (k, tn), lambda j, i: lead_index + (0, c0 + j))
    else:
        assert cols.transposed
        lead = () if cols.layer is None else (pl.Element(1),)
        w_spec = pl.BlockSpec(lead + (pl.Element(tn), pl.Element(k)),
                              lambda j, i: lead_index + (pl.multiple_of(cols.col0 + j * tn, _SUBLANES), 0))
    return pl.pallas_call(
        body,
        grid=(-(-cols.n // tn), m // tm),
        in_specs=[pl.BlockSpec((tm, k), lambda j, i: (i, 0)), w_spec] + list(extra_specs),
        out_specs=out_specs,
        out_shape=out_shape,
        scratch_shapes=[pltpu.VMEM(w_block, _CDT)] if convert else [],
        compiler_params=_params("arbitrary", "arbitrary"),
        name=name,
    )(a, cols.w, *extras)


def _ep_rms(res, extras, outs, rows):
    (g_ref,) = extras
    (o_ref,) = outs
    y = res * lax.rsqrt(jnp.mean(res * res, axis=-1, keepdims=True) + _RMS_EPS)
    o_ref[rows, :] = (y * g_ref[...]).astype(o_ref.dtype)


def _ep_heads(res, extras, outs, rows, *, rope, scale):
    (o_ref,) = outs
    nh = res.shape[1] // _HEAD_DIM
    if rope:
        cos = extras[0][rows, :]
        sin = extras[1][rows, :]
    for h in range(nh):
        xh = res[:, h * _HEAD_DIM:(h + 1) * _HEAD_DIM]
        if rope:
            xh = _rope_tile(xh, cos, sin)
        if scale != 1.0:
            xh = xh * scale
        o_ref[0, h, rows, :] = xh.astype(o_ref.dtype)


def _ep_vt(res, extras, outs, rows, *, t):
    (vt_ref,) = outs
    assert rows.start % t == 0 and res.shape[0] % t == 0
    for h in range(res.shape[1] // _HEAD_DIM):
        vh = res[:, h * _HEAD_DIM:(h + 1) * _HEAD_DIM]
        for c in range(res.shape[0] // t):
            dst = rows.start // t + c
            vt_ref[0, h, dst, :_HEAD_DIM, :] = vh[c * t:(c + 1) * t, :].T.astype(vt_ref.dtype)
            vt_ref[0, h, dst, _HEAD_DIM:, :] = jnp.ones((_VT_ROWS - _HEAD_DIM, t), vt_ref.dtype)


def _ep_kidx(res, extras, outs, rows, *, idx_heads, w_scale):
    cos_ref, sin_ref, g_ref, b_ref = extras
    kidx_ref, wt_ref = outs
    x = res[:, :_HEAD_DIM]
    mu = jnp.mean(x, axis=-1, keepdims=True)
    xc = x - mu
    var = jnp.mean(xc * xc, axis=-1, keepdims=True)
    y = xc * lax.rsqrt(var + _LN_EPS) * g_ref[...] + b_ref[...]
    kidx_ref[0] = _rope_tile(y, cos_ref[...], sin_ref[...]).astype(kidx_ref.dtype)
    wt = (res[:, _HEAD_DIM:] * w_scale).T
    wt_ref[0] = wt[:idx_heads, :]


def _ep_silu(res, extras, outs, rows):
    (o_ref,) = outs
    o_ref[0, 0, rows, :] = (res * jax.nn.sigmoid(res)).astype(o_ref.dtype)


def _ep_logf(res, extras, outs, rows):
    (bias_ref,) = extras
    (o_ref,) = outs
    z = res + bias_ref[...]
    o_ref[0] = jnp.minimum(z, 0.0) - jnp.log(1.0 + jnp.exp(-jnp.abs(z)))


def _outproj(o, gate, w, tm, tn, name):
    b, g, s, c = o.shape
    d = w.shape[1]
    nsb = s // tm

    def body(o_ref, g_ref, w_ref, out_ref):
        acc = None
        for gi in range(g):
            a = (o_ref[0, gi].astype(jnp.float32) * g_ref[0, gi].astype(jnp.float32)).astype(_CDT)
            part = jnp.dot(a, w_ref[gi * c:(gi + 1) * c, :], preferred_element_type=jnp.float32)
            acc = part if acc is None else acc + part
        out_ref[...] = acc

    return pl.pallas_call(
        body,
        grid=(d // tn, b * nsb),
        in_specs=[pl.BlockSpec((1, g, tm, c), lambda j, i: (i // nsb, 0, i % nsb, 0)),
                  pl.BlockSpec((1, g, tm, c), lambda j, i: (i // nsb, 0, i % nsb, 0)),
                  pl.BlockSpec((g * c, tn), lambda j, i: (0, j))],
        out_specs=pl.BlockSpec((tm, tn), lambda j, i: (i, j)),
        out_shape=jax.ShapeDtypeStruct((b * s, d), jnp.float32),
        compiler_params=_params("arbitrary", "arbitrary"),
        name=name,
    )(o, gate, w)


def _resid_ln(h, x, g, b, alpha, tm, with_cdt, name):
    m, d = x.shape

    def body(h_ref, x_ref, g_ref, b_ref, *outs):
        y = alpha * x_ref[...] + h_ref[...]
        mu = jnp.mean(y, axis=-1, keepdims=True)
        yc = y - mu
        var = jnp.mean(yc * yc, axis=-1, keepdims=True)
        z = yc * lax.rsqrt(var + _LN_EPS) * g_ref[...] + b_ref[...]
        outs[0][...] = z
        if with_cdt:
            outs[1][...] = z.astype(_CDT)

    row = pl.BlockSpec((tm, d), lambda i: (i, 0))
    vec = pl.BlockSpec((1, d), lambda i: (0, 0))
    out_shape = [jax.ShapeDtypeStruct((m, d), jnp.float32)]
    out_specs = [row]
    if with_cdt:
        out_shape.append(jax.ShapeDtypeStruct((m, d), _CDT))
        out_specs.append(row)
    return pl.pallas_call(
        body,
        grid=(m // tm,),
        in_specs=[row, row, vec, vec],
        out_specs=out_specs,
        out_shape=out_shape,
        compiler_params=_params("arbitrary"),
        name=name,
    )(h, x, g.reshape(1, d), b.reshape(1, d))


def _col_reduce(x, op):
    rows, n = x.shape
    fold = 8 * _SUBLANES
    if rows > fold and rows % fold == 0:
        x = op(x.reshape(rows // fold, fold, n), axis=0)
    return op(x, axis=0, keepdims=True)


def _softmax_step(load_sc, m_ref, cols):
    m_old = m_ref[:, cols]
    m_new = jnp.maximum(m_old, _col_reduce(load_sc(), jnp.max))
    m_ref[:, cols] = m_new
    return jnp.exp2(load_sc() - m_new).astype(_CDT), jnp.exp2(m_old - m_new)


def _pv_step(vta, p, alpha, acc_ref, cols):
    pv = jnp.dot(vta, p, preferred_element_type=jnp.float32)
    acc_ref[:, cols] = pv if alpha is None else alpha * acc_ref[:, cols] + pv


def _float_sort_key(x):
    bits = pltpu.bitcast(x, jnp.int32)
    return bits ^ ((bits >> 31) & jnp.int32(0x7FFFFFFF))


def _dsa_core(q, qi, wt, kidx, k, vt, *, topk, tk1, tk, hchunk):
    b, n_heads, s, _ = q.shape
    idx_heads = qi.shape[1]
    n_kv = k.shape[1]
    tq = _LANES
    gq = _GROUP * tq
    log_s = int(math.log2(s))
    assert 1 << log_s == s and s % tk == 0 and tk % tk1 == 0 and tk1 % tq == 0
    assert idx_heads % hchunk == 0 and gq % _MXU_DIM == 0

    def body(q_ref, qi_ref, wt_ref, kidx_ref, k_ref, vt_ref, o_ref,
             key_ref, mask_ref, sc_ref, p_ref, al_ref, m_ref, acc_ref):
        i = pl.program_id(1)
        n1 = (i * tq + tq + tk1 - 1) // tk1
        n3 = (i * tq + tq + tk - 1) // tk

        q_pos1 = i * tq + lax.broadcasted_iota(jnp.int32, (tk1, tq), 1)
        s_iota1 = lax.broadcasted_iota(jnp.int32, (tk1, tq), 0)

        def score_tile(kt, carry):
            ks = pl.multiple_of(kt * tk1, tk1)
            kx = kidx_ref[0, pl.ds(ks, tk1), :]
            acc = jnp.zeros((tk1, tq), jnp.float32)
            for hc in range(idx_heads // hchunk):
                qq = qi_ref[0, hc * hchunk:(hc + 1) * hchunk].reshape(hchunk * tq, _HEAD_DIM)
                d = lax.dot_general(kx, qq, _NT, preferred_element_type=jnp.float32)
                for hh in range(hchunk):
                    h = hc * hchunk + hh
                    acc = acc + jnp.maximum(d[:, hh * tq:(hh + 1) * tq], 0.0) * wt_ref[0, h:h + 1, :]
            key = _float_sort_key(acc + 0.0)
            key_ref[pl.ds(ks, tk1), :] = jnp.where(ks + s_iota1 <= q_pos1, key, _INT_MIN)
            return carry

        lax.fori_loop(0, n1, score_tile, 0)

        def pad_tile(kt, carry):
            key_ref[pl.ds(pl.multiple_of(kt * tk1, tk1), tk1), :] = jnp.full((tk1, tq), _INT_MIN, jnp.int32)
            return carry

        lax.fori_loop(n1, n3 * (tk // tk1), pad_tile, 0)

        q_pos = i * tq + lax.broadcasted_iota(jnp.int32, (tk, tq), 1)
        s_iota = lax.broadcasted_iota(jnp.int32, (tk, tq), 0)

        fold = 8 * _SUBLANES

        def count(pred_fn):
            def tile(kt, cnt):
                ks = pl.multiple_of(kt * tk, tk)
                c = jnp.where(pred_fn(key_ref[pl.ds(ks, tk), :], ks), 1.0, 0.0)
                return cnt + jnp.sum(c.reshape(tk // fold, fold, tq), axis=0)
            cnt = lax.fori_loop(0, n3, tile, jnp.zeros((fold, tq), jnp.float32))
            return jnp.sum(cnt, axis=0, keepdims=True)

        def value_step(t, prefix):
            cand = prefix | jnp.left_shift(jnp.int32(1), 31 - t)
            cand_s = cand ^ _INT_MIN
            cnt = count(lambda key, ks: key >= cand_s)
            return jnp.where(cnt >= topk, cand, prefix)

        prefix = lax.fori_loop(0, 32, value_step, jnp.zeros((1, tq), jnp.int32))
        thr = prefix ^ _INT_MIN

        def tie_break():
            need = topk - count(lambda key, ks: key > thr)

            def index_step(t, j):
                cand = j | jnp.left_shift(jnp.int32(1), log_s - 1 - t)
                cnt = count(lambda key, ks: (key == thr) & (ks + s_iota < cand))
                return jnp.where(cnt < need, cand, j)

            return lax.fori_loop(0, log_s, index_step, jnp.zeros((1, tq), jnp.int32))

        n_ge = count(lambda key, ks: key >= thr)
        j_last = lax.cond(jnp.max(n_ge) > topk, tie_break,
                          lambda: jnp.full((1, tq), s, jnp.int32))

        def mask_tile(kt, carry):
            ks = pl.multiple_of(kt * tk, tk)
            key = key_ref[pl.ds(ks, tk), :]
            s_pos = ks + s_iota
            sel = (key > thr) | ((key == thr) & (s_pos <= j_last))
            mask_ref[pl.ds(ks, tk), :] = jnp.where(sel & (s_pos <= q_pos), 0.0, _MASKED).astype(_CDT)
            return carry

        lax.fori_loop(0, n3, mask_tile, 0)

        r = lax.broadcasted_iota(jnp.int32, (gq, tq), 0)
        c = lax.broadcasted_iota(jnp.int32, (gq, tq), 1)
        one_hot = jnp.where((r & (tq - 1)) == c, 1.0, 0.0).astype(_CDT)

        chunks = [slice(cc * _MXU_DIM, (cc + 1) * _MXU_DIM) for cc in range(gq // _MXU_DIM)]

        def kv_group(g, carry):
            qg = q_ref[0, pl.ds(g * _GROUP, _GROUP)].reshape(gq, _HEAD_DIM)
            qaug = jnp.concatenate([qg, one_hot], axis=1)
            m_ref[...] = jnp.full((1, gq), _M_INIT, jnp.float32)

            def logits(kt):
                kaug = jnp.concatenate([k_ref[0, g, pl.ds(kt * tk, tk), :],
                                        mask_ref[pl.ds(kt * tk, tk), :]], axis=1)
                for cols in chunks:
                    sc_ref[kt & 1, :, cols] = lax.dot_general(kaug, qaug[cols], _NT,
                                                              preferred_element_type=jnp.float32)

            def softmax(kt):
                for cols in chunks:
                    p_ref[kt & 1, :, cols], al_ref[kt & 1, :, cols] = _softmax_step(
                        lambda: sc_ref[kt & 1, :, cols], m_ref, cols)

            def pv(kt):
                vta = vt_ref[0, g, kt]
                for cols in chunks:
                    _pv_step(vta, p_ref[kt & 1, :, cols], None if kt == 0 else al_ref[kt & 1, :, cols],
                             acc_ref, cols)

            for n_tiles in range(1, s // tk + 1):
                @pl.when(n3 == n_tiles)
                def _():
                    for step in range(n_tiles + 2):
                        if step < n_tiles:
                            logits(step)
                        if 1 <= step <= n_tiles:
                            softmax(step - 1)
                        if step >= 2:
                            pv(step - 2)

            acc = acc_ref[...]
            ot = acc[:_HEAD_DIM] * (1.0 / acc[_HEAD_DIM:_HEAD_DIM + 1])
            for hh in range(_GROUP):
                o_ref[0, g, :, hh * _HEAD_DIM:(hh + 1) * _HEAD_DIM] = (
                    ot[:, hh * tq:(hh + 1) * tq].T.astype(o_ref.dtype))
            return carry

        lax.fori_loop(0, n_kv, kv_group, 0)

    nq = s // tq
    return pl.pallas_call(
        body,
        grid=(b, nq),
        in_specs=[pl.BlockSpec((1, n_heads, tq, _HEAD_DIM), lambda bi, i: (bi, 0, i, 0)),
                  pl.BlockSpec((1, idx_heads, tq, _HEAD_DIM), lambda bi, i: (bi, 0, i, 0)),
                  pl.BlockSpec((1, idx_heads, tq), lambda bi, i: (bi, 0, i)),
                  pl.BlockSpec((1, s, _HEAD_DIM), lambda bi, i: (bi, 0, 0)),
                  pl.BlockSpec((1, n_kv, s, _HEAD_DIM), lambda bi, i: (bi, 0, 0, 0)),
                  pl.BlockSpec((1, n_kv, s // tk, _VT_ROWS, tk), lambda bi, i: (bi, 0, 0, 0, 0))],
        out_specs=pl.BlockSpec((1, n_kv, tq, _GROUP * _HEAD_DIM), lambda bi, i: (bi, 0, i, 0)),
        out_shape=jax.ShapeDtypeStruct((b, n_kv, s, _GROUP * _HEAD_DIM), _CDT),
        scratch_shapes=[pltpu.VMEM((s, tq), jnp.int32),
                        pltpu.VMEM((s, tq), _CDT),
                        pltpu.VMEM((2, tk, gq), jnp.float32),
                        pltpu.VMEM((2, tk, gq), _CDT),
                        pltpu.VMEM((2, 1, gq), jnp.float32),
                        pltpu.VMEM((1, gq), jnp.float32),
                        pltpu.VMEM((_VT_ROWS, gq), jnp.float32)],
        compiler_params=_params("arbitrary", "arbitrary"),
        name="dsa_core",
    )(q, qi, wt, kidx, k, vt)


def _cumsum_aug(lf, n_heads):
    b, s, w = lf.shape
    blk = _LANES

    def body(lf_ref, out_ref, carry_ref):
        @pl.when(pl.program_id(1) == 0)
        def _():
            carry_ref[...] = jnp.zeros_like(carry_ref)
        r = lax.broadcasted_iota(jnp.int32, (blk, blk), 0)
        c = lax.broadcasted_iota(jnp.int32, (blk, blk), 1)
        tri = jnp.where(r >= c, 1.0, 0.0).astype(jnp.float32)
        cs = jnp.dot(tri, lf_ref[0], preferred_element_type=jnp.float32,
                     precision=lax.Precision.HIGHEST) + carry_ref[...]
        carry_ref[...] = cs[blk - 1:blk, :]
        c2 = cs * _LOG2E
        for h in range(n_heads):
            col = c2[:, h:h + 1]
            hi = col.astype(_CDT).astype(jnp.float32)
            mid = (col - hi).astype(_CDT).astype(jnp.float32)
            lo = (col - hi - mid).astype(_CDT).astype(jnp.float32)
            tile = jnp.where(c == 0, hi, jnp.where(c == 1, mid, jnp.where(c == 2, lo, 0.0)))
            out_ref[0, h] = tile.astype(out_ref.dtype)

    return pl.pallas_call(
        body,
        grid=(b, s // blk),
        in_specs=[pl.BlockSpec((1, blk, w), lambda bi, i: (bi, i, 0))],
        out_specs=pl.BlockSpec((1, n_heads, blk, _LANES), lambda bi, i: (bi, 0, i, 0)),
        out_shape=jax.ShapeDtypeStruct((b, n_heads, s, _LANES), _CDT),
        scratch_shapes=[pltpu.VMEM((1, w), jnp.float32)],
        compiler_params=_params("arbitrary", "arbitrary"),
        name="fox_cumsum",
    )(lf)


def _fox_core(q, k, caug, vt, *, t):
    b, n_heads, s, _ = q.shape
    assert t % _MXU_DIM == 0 and s % t == 0

    chunks = [slice(cc * _MXU_DIM, (cc + 1) * _MXU_DIM) for cc in range(t // _MXU_DIM)]

    tiles = [(i, j) for i in range(s // t) for j in range(i + 1)]

    def body(q_ref, k_ref, c_ref, vt_ref, o_ref, kaug_ref, sc_ref, p_ref, al_ref, m_ref, acc_ref):
        kaug_ref[:, :_HEAD_DIM] = k_ref[0, 0]
        kaug_ref[:, _HEAD_DIM:] = c_ref[0, 0]
        lane = lax.broadcasted_iota(jnp.int32, (t, _HEAD_DIM), 1)
        minus_ones = jnp.where(lane < 3, -1.0, 0.0).astype(_CDT)

        def logits(n):
            i, j = tiles[n]
            qaug = jnp.concatenate([q_ref[0, 0, i * t:(i + 1) * t, :], minus_ones], axis=1)
            ka = kaug_ref[j * t:(j + 1) * t, :]
            for cols in chunks:
                sc_ref[n & 1, :, cols] = lax.dot_general(ka, qaug[cols], _NT,
                                                         preferred_element_type=jnp.float32)

        def softmax(n):
            i, j = tiles[n]
            if j == 0:
                m_ref[...] = jnp.full((1, t), _M_INIT, jnp.float32)
            for cols in chunks:
                def load_sc(cols=cols):
                    sc = sc_ref[n & 1, :, cols]
                    if i == j:
                        key_i = lax.broadcasted_iota(jnp.int32, (t, _MXU_DIM), 0)
                        qry_i = cols.start + lax.broadcasted_iota(jnp.int32, (t, _MXU_DIM), 1)
                        sc = jnp.where(key_i <= qry_i, sc, -jnp.inf)
                    return sc

                p_ref[n & 1, :, cols], al_ref[n & 1, :, cols] = _softmax_step(load_sc, m_ref, cols)

        def pv(n):
            i, j = tiles[n]
            vta = vt_ref[0, 0, j]
            for cols in chunks:
                _pv_step(vta, p_ref[n & 1, :, cols], None if j == 0 else al_ref[n & 1, :, cols],
                         acc_ref, cols)
            if j == i:
                acc = acc_ref[...]
                ot = acc[:_HEAD_DIM] * (1.0 / acc[_HEAD_DIM:_HEAD_DIM + 1])
                for cq in range(t // _LANES):
                    o_ref[0, i * t + cq * _LANES:i * t + (cq + 1) * _LANES, :] = (
                        ot[:, cq * _LANES:(cq + 1) * _LANES].T.astype(o_ref.dtype))

        for step in range(len(tiles) + 2):
            if step < len(tiles):
                logits(step)
            if 1 <= step <= len(tiles):
                softmax(step - 1)
            if step >= 2:
                pv(step - 2)

    head = pl.BlockSpec((1, 1, s, _HEAD_DIM), lambda bi, h: (bi, h, 0, 0))
    return pl.pallas_call(
        body,
        grid=(b, n_heads),
        in_specs=[head, head, head,
                  pl.BlockSpec((1, 1, s // t, _VT_ROWS, t), lambda bi, h: (bi, h, 0, 0, 0))],
        out_specs=pl.BlockSpec((1, s, _HEAD_DIM), lambda bi, h: (bi, 0, h)),
        out_shape=jax.ShapeDtypeStruct((b, s, n_heads * _HEAD_DIM), _CDT),
        scratch_shapes=[pltpu.VMEM((s, 2 * _HEAD_DIM), _CDT),
                        pltpu.VMEM((2, t, t), jnp.float32),
                        pltpu.VMEM((2, t, t), _CDT),
                        pltpu.VMEM((2, 1, t), jnp.float32),
                        pltpu.VMEM((1, t), jnp.float32),
                        pltpu.VMEM((_VT_ROWS, t), jnp.float32)],
        compiler_params=_params("arbitrary", "arbitrary"),
        name="fox_core",
    )(q, k, caug, vt)


def _rope_tables(s):
    half = _HEAD_DIM // 2
    inv = _ROPE_THETA ** (-jnp.arange(half, dtype=jnp.float32) / half)
    ang = jnp.arange(s, dtype=jnp.float32)[:, None] * inv[None, :]
    cos, sin = jnp.cos(ang), jnp.sin(ang)
    return jnp.concatenate([cos, cos], axis=1), jnp.concatenate([-sin, sin], axis=1)


def _placed(cols, s, tn_cap=None):
    tm, tn = _proj_tiles(s, cols)
    if tn_cap is not None:
        tn = min(tn, tn_cap)
    if not cols.aligned(tn):
        cols = cols.materialize()
        tm, tn = _proj_tiles(s, cols)
        if tn_cap is not None:
            tn = min(tn, tn_cap)
    return cols, tm, tn


def _heads_call(a, cols, b, s, *, rope, scale, cos, sin, name):
    cols, tm, tn = _placed(cols, s)
    nsb = s // tm
    extras, extra_specs = (), ()
    if rope:
        extras = (cos, sin)
        extra_specs = (pl.BlockSpec((tm, _HEAD_DIM), lambda j, i: (i % nsb, 0)),) * 2
    nh_t = tn // _HEAD_DIM
    return _proj(
        a, cols, functools.partial(_ep_heads, rope=rope, scale=scale), extras, extra_specs,
        jax.ShapeDtypeStruct((b, cols.n // _HEAD_DIM, s, _HEAD_DIM), _CDT),
        pl.BlockSpec((1, nh_t, tm, _HEAD_DIM), lambda j, i: (i // nsb, j, i % nsb, 0)),
        tm, tn, name, row_chunk=min(_ROW_CHUNK, tm))


def _vt_call(a, cols, b, s, t, name):
    cols, tm, tn = _placed(cols, s)
    nsb = s // tm
    nh_t = tn // _HEAD_DIM
    return _proj(
        a, cols, functools.partial(_ep_vt, t=t), (), (),
        jax.ShapeDtypeStruct((b, cols.n // _HEAD_DIM, s // t, _VT_ROWS, t), _CDT),
        pl.BlockSpec((1, nh_t, tm // t, _VT_ROWS, t), lambda j, i: (i // nsb, j, i % nsb, 0, 0)),
        tm, tn, name, row_chunk=t)


def _gate_call(a, cols, b, s, groups, name):
    c = cols.n // groups
    cols, tm, tn = _placed(cols, s, tn_cap=c)
    nsb = s // tm
    per = c // tn
    return _proj(
        a, cols, _ep_silu, (), (),
        jax.ShapeDtypeStruct((b, groups, s, c), _CDT),
        pl.BlockSpec((1, 1, tm, tn), lambda j, i: (i // nsb, j // per, i % nsb, j % per)),
        tm, tn, name, row_chunk=min(_ROW_CHUNK, tm))


def _dsa_layer(xc, w_in_t, q_norm_g, w_uq, kidx_g, kidx_b, w_out, layer, b, s):
    d = xc.shape[1]
    q_lora = w_uq.shape[1]
    branch = w_out.shape[1]
    idx_heads = (w_uq.shape[2] - branch) // _HEAD_DIM
    kv = (w_in_t.shape[1] - q_lora - _HEAD_DIM - idx_heads - branch) // 2
    w_in = functools.partial(_Cols, w_in_t, layer=layer, transposed=True)
    n_kv = kv // _HEAD_DIM
    n_heads = branch // _HEAD_DIM
    assert n_heads == n_kv * _GROUP and idx_heads <= _LANES
    topk = min(_TOPK_MAX, s // 4)
    o0 = q_lora
    o1 = o0 + 2 * kv
    o2 = o1 + _HEAD_DIM
    o3 = o2 + idx_heads
    cos, sin = _rope_tables(s)
    tk1 = min(256, s)
    tk = min(512, s)

    c_cq = w_in(0, q_lora).materialize()
    tm = min(512, s)
    cq = _proj(xc, c_cq, _ep_rms, (q_norm_g[layer].reshape(1, q_lora),),
               (pl.BlockSpec((1, q_lora), lambda j, i: (0, 0)),),
               jax.ShapeDtypeStruct((b * s, q_lora), _CDT),
               pl.BlockSpec((tm, q_lora), lambda j, i: (i, 0)), tm, q_lora, "dsa_cq")
    k = _heads_call(xc, w_in(o0, kv), b, s, rope=True, scale=1.0, cos=cos, sin=sin, name="dsa_k")
    vt = _vt_call(xc, w_in(o0 + kv, kv), b, s, tk, "dsa_v")

    c_ki = w_in(o1, 2 * _LANES)
    if not c_ki.aligned(2 * _LANES):
        w_ki = jnp.pad(w_in_t[layer][o1:o3, :], ((0, _LANES - idx_heads), (0, 0)))
        c_ki = _Cols(w_ki.astype(_CDT), transposed=True)
    nsb = s // tm
    rope_specs = (pl.BlockSpec((tm, _HEAD_DIM), lambda j, i: (i % nsb, 0)),) * 2
    w_scale = idx_heads ** -0.5 * _HEAD_DIM ** -0.5
    vec = pl.BlockSpec((1, _HEAD_DIM), lambda j, i: (0, 0))
    kidx, wt = _proj(xc, c_ki, functools.partial(_ep_kidx, idx_heads=idx_heads, w_scale=w_scale),
                     (cos, sin, kidx_g[layer].reshape(1, _HEAD_DIM), kidx_b[layer].reshape(1, _HEAD_DIM)),
                     rope_specs + (vec, vec),
                     [jax.ShapeDtypeStruct((b, s, _HEAD_DIM), _CDT),
                      jax.ShapeDtypeStruct((b, idx_heads, s), jnp.float32)],
                     [pl.BlockSpec((1, tm, _HEAD_DIM), lambda j, i: (i // nsb, i % nsb, 0)),
                      pl.BlockSpec((1, idx_heads, tm), lambda j, i: (i // nsb, 0, i % nsb))],
                     tm, 2 * _LANES, "dsa_kidx")
    gate = _gate_call(xc, w_in(o3, branch), b, s, n_kv, "dsa_gate")
    q = _heads_call(cq, _Cols(w_uq, 0, branch, layer), b, s, rope=True,
                    scale=_HEAD_DIM ** -0.5 * _LOG2E, cos=cos, sin=sin, name="dsa_q")
    qi = _heads_call(cq, _Cols(w_uq, branch, None, layer), b, s, rope=True, scale=1.0, cos=cos, sin=sin,
                     name="dsa_qi")
    o = _dsa_core(q, qi, wt, kidx, k, vt, topk=topk, tk1=tk1, tk=tk, hchunk=min(8, idx_heads))
    return _outproj(o, gate, w_out[layer].astype(_CDT), min(512, s), min(1024, d), "dsa_out")


def _fox_layer(xc, w_in_t, forget_bias, w_out, layer, b, s):
    d = xc.shape[1]
    branch = w_out.shape[1]
    n_heads = forget_bias.shape[1]
    assert branch == n_heads * _HEAD_DIM and n_heads <= _LANES
    w_in = functools.partial(_Cols, w_in_t, layer=layer, transposed=True)
    w_f = jnp.pad(w_in_t[layer][4 * branch:, :], ((0, _LANES - n_heads), (0, 0)))
    fb = jnp.pad(forget_bias[layer], (0, _LANES - n_heads)).reshape(1, _LANES)
    tm = min(512, s)
    nsb = s // tm
    t = min(512, s)
    q = _heads_call(xc, w_in(0, branch), b, s, rope=False,
                    scale=_HEAD_DIM ** -0.5 * _LOG2E, cos=None, sin=None, name="fox_q")
    k = _heads_call(xc, w_in(branch, branch), b, s, rope=False, scale=1.0,
                    cos=None, sin=None, name="fox_k")
    vt = _vt_call(xc, w_in(2 * branch, branch), b, s, t, "fox_v")
    gate = _gate_call(xc, w_in(3 * branch, branch), b, s, 1, "fox_gate")
    lf = _proj(xc, _Cols(w_f, transposed=True), _ep_logf, (fb,), (pl.BlockSpec((1, _LANES), lambda j, i: (0, 0)),),
               jax.ShapeDtypeStruct((b, s, _LANES), jnp.float32),
               pl.BlockSpec((1, tm, _LANES), lambda j, i: (i // nsb, i % nsb, 0)),
               tm, _LANES, "fox_logf")
    caug = _cumsum_aug(lf, n_heads)
    o = _fox_core(q, k, caug, vt, t=t)
    return _outproj(o.reshape(b, 1, s, branch), gate, w_out[layer].astype(_CDT), tm, min(1024, d),
                    "fox_out")


def kernel(x, a_w_in, a_q_norm_g, a_w_uq, a_kidx_norm_g, a_kidx_norm_b, a_w_out,
           b_w_in, b_forget_bias, b_w_out, ln_g, ln_b):
    b, s, d = x.shape
    depth = ln_g.shape[0]
    alpha = (2 * depth) ** 0.25
    xf = x.reshape(b * s, d)
    xc = xf.astype(_CDT)
    a_w_in_t = jnp.swapaxes(a_w_in, 1, 2)
    b_w_in_t = jnp.swapaxes(b_w_in, 1, 2)
    tm_ln = min(256, s)
    for i in range(depth):
        j = i // 2
        if i % 2 == 0:
            h = _dsa_layer(xc, a_w_in_t, a_q_norm_g, a_w_uq, a_kidx_norm_g, a_kidx_norm_b, a_w_out, j, b, s)
        else:
            h = _fox_layer(xc, b_w_in_t, b_forget_bias, b_w_out, j, b, s)
        last = i == depth - 1
        outs = _resid_ln(h, xf, ln_g[i], ln_b[i], alpha, tm_ln, not last, "resid_ln_%d" % i)
        xf = outs[0]
        if not last:
            xc = outs[1]
    return xf.reshape(b, s, d)
```

```python
import functools
import math

import jax
import jax.numpy as jnp
from jax import lax
from jax.experimental import pallas as pl
from jax.experimental.pallas import tpu as pltpu

_CDT = jnp.bfloat16
_HEAD_DIM = 128
_GROUP = 8
_TOPK_MAX = 256
_ROPE_THETA = 10000.0
_LN_EPS = 1e-5
_RMS_EPS = 1e-6
_LANES = 128
_SUBLANES = 8
_MXU_DIM = 256
_VMEM_LIMIT = 56 * 1024 * 1024
_PROJ_TILE_BUDGET = 40 * 1024 * 1024
_VT_ROWS = _HEAD_DIM + 16
_INT_MIN = -2 ** 31
_HALF_BIAS = 1 << 15
_MASKED = -1e30
_M_INIT = -1e29
_LOG2E = math.log2(math.e)
_ROW_CHUNK = 256

_NT = (((1,), (1,)), ((), ()))


def _params(*sem):
    return pltpu.CompilerParams(dimension_semantics=sem, vmem_limit_bytes=_VMEM_LIMIT)


def _rope_tile(x, cos, sin_signed):
    return x * cos + pltpu.roll(x, _HEAD_DIM // 2, axis=1) * sin_signed


class _Cols:
    def __init__(self, w, col0=0, n=None, layer=None, transposed=False):
        self.w, self.col0, self.layer, self.transposed = w, col0, layer, transposed
        self.n_total = w.shape[-2] if transposed else w.shape[-1]
        self.k = w.shape[-1] if transposed else w.shape[-2]
        self.n = self.n_total - col0 if n is None else n

    def block_aligned(self, tn):
        return self.col0 % tn == 0

    def aligned(self, tn):
        start_ok = self.col0 % _SUBLANES == 0 if self.transposed else self.block_aligned(tn)
        return start_ok and self.col0 + -(-self.n // tn) * tn <= self.n_total

    def materialize(self):
        w = self.w if self.layer is None else self.w[self.layer]
        if self.transposed:
            return _Cols(w[self.col0:self.col0 + self.n, :].astype(_CDT), transposed=True)
        return _Cols(w[:, self.col0:self.col0 + self.n].astype(_CDT))


def _proj_tiles(s, cols):
    itemsize = cols.w.dtype.itemsize
    act = jnp.dtype(_CDT).itemsize
    for tn in (1024, 512, 256, 128):
        if tn > cols.n:
            continue
        w_bytes = cols.k * tn * (2 * itemsize + (act if cols.w.dtype != _CDT else 0))
        for tm in (1024, 512, 256):
            if tm <= s and w_bytes + 2 * tm * cols.k * act <= _PROJ_TILE_BUDGET:
                return tm, tn
    raise ValueError("projection tiles do not fit VMEM")


def _proj(a, cols, epilogue, extras, extra_specs, out_shape, out_specs, tm, tn, name, row_chunk=None):
    m, k = a.shape
    assert cols.aligned(tn) and k == cols.k
    convert = cols.w.dtype != _CDT
    n_extra = len(extras)
    w_block = (tn, k) if cols.transposed else (k, tn)

    def body(a_ref, w_ref, *rest):
        if convert:
            wc_ref = rest[-1]
            rest = rest[:-1]

            @pl.when(pl.program_id(1) == 0)
            def _():
                wc_ref[...] = w_ref[...].reshape(w_block).astype(_CDT)

            w = wc_ref[...]
        else:
            w = w_ref[...].reshape(w_block)
        rc = tm if row_chunk is None else row_chunk
        for r in range(tm // rc):
            rows = slice(r * rc, (r + 1) * rc)
            if cols.transposed:
                res = lax.dot_general(a_ref[rows, :], w, _NT, preferred_element_type=jnp.float32)
            else:
                res = jnp.dot(a_ref[rows, :], w, preferred_element_type=jnp.float32)
            epilogue(res, rest[:n_extra], rest[n_extra:], rows)

    lead_index = () if cols.layer is None else (cols.layer,)
    if cols.block_aligned(tn):
        c0 = cols.col0 // tn
        lead = () if cols.layer is None else (None,)
        if cols.transposed:
            w_spec = pl.BlockSpec(lead + (tn, k), lambda j, i: lead_index + (c0 + j, 0))
        else:
            w_spec = pl.BlockSpec(lead + (k, tn), lambda j, i: lead_index + (0, c0 + j))
    else:
        assert cols.transposed
        lead = () if cols.layer is None else (pl.Element(1),)
        w_spec = pl.BlockSpec(lead + (pl.Element(tn), pl.Element(k)),
                              lambda j, i: lead_index + (pl.multiple_of(cols.col0 + j * tn, _SUBLANES), 0))
    return pl.pallas_call(
        body,
        grid=(-(-cols.n // tn), m // tm),
        in_specs=[pl.BlockSpec((tm, k), lambda j, i: (i, 0)), w_spec] + list(extra_specs),
        out_specs=out_specs,
        out_shape=out_shape,
        scratch_shapes=[pltpu.VMEM(w_block, _CDT)] if convert else [],
        compiler_params=_params("arbitrary", "arbitrary"),
        name=name,
    )(a, cols.w, *extras)


def _ep_rms(res, extras, outs, rows):
    (g_ref,) = extras
    (o_ref,) = outs
    y = res * lax.rsqrt(jnp.mean(res * res, axis=-1, keepdims=True) + _RMS_EPS)
    o_ref[rows, :] = (y * g_ref[...]).astype(o_ref.dtype)


def _ep_heads(res, extras, outs, rows, *, rope, scale):
    (o_ref,) = outs
    nh = res.shape[1] // _HEAD_DIM
    if rope:
        cos = extras[0][rows, :]
        sin = extras[1][rows, :]
    for h in range(nh):
        xh = res[:, h * _HEAD_DIM:(h + 1) * _HEAD_DIM]
        if rope:
            xh = _rope_tile(xh, cos, sin)
        if scale != 1.0:
            xh = xh * scale
        o_ref[0, h, rows, :] = xh.astype(o_ref.dtype)


def _ep_vt(res, extras, outs, rows, *, t):
    (vt_ref,) = outs
    assert rows.start % t == 0 and res.shape[0] % t == 0
    for h in range(res.shape[1] // _HEAD_DIM):
        vh = res[:, h * _HEAD_DIM:(h + 1) * _HEAD_DIM]
        for c in range(res.shape[0] // t):
            dst = rows.start // t + c
            vt_ref[0, h, dst, :_HEAD_DIM, :] = vh[c * t:(c + 1) * t, :].T.astype(vt_ref.dtype)
            vt_ref[0, h, dst, _HEAD_DIM:, :] = jnp.ones((_VT_ROWS - _HEAD_DIM, t), vt_ref.dtype)


def _ep_kidx(res, extras, outs, rows, *, idx_heads, w_scale):
    cos_ref, sin_ref, g_ref, b_ref = extras
    kidx_ref, wt_ref = outs
    x = res[:, :_HEAD_DIM]
    mu = jnp.mean(x, axis=-1, keepdims=True)
    xc = x - mu
    var = jnp.mean(xc * xc, axis=-1, keepdims=True)
    y = xc * lax.rsqrt(var + _LN_EPS) * g_ref[...] + b_ref[...]
    kidx_ref[0] = _rope_tile(y, cos_ref[...], sin_ref[...]).astype(kidx_ref.dtype)
    wt = (res[:, _HEAD_DIM:] * w_scale).T
    wt_ref[0] = wt[:idx_heads, :]


def _ep_silu(res, extras, outs, rows):
    (o_ref,) = outs
    o_ref[0, 0, rows, :] = (res * jax.nn.sigmoid(res)).astype(o_ref.dtype)


def _ep_logf(res, extras, outs, rows):
    (bias_ref,) = extras
    (o_ref,) = outs
    z = res + bias_ref[...]
    o_ref[0] = jnp.minimum(z, 0.0) - jnp.log(1.0 + jnp.exp(-jnp.abs(z)))


def _outproj(o, gate, w, tm, tn, name):
    b, g, s, c = o.shape
    d = w.shape[1]
    nsb = s // tm

    def body(o_ref, g_ref, w_ref, out_ref):
        acc = None
        for gi in range(g):
            a = (o_ref[0, gi].astype(jnp.float32) * g_ref[0, gi].astype(jnp.float32)).astype(_CDT)
            part = jnp.dot(a, w_ref[gi * c:(gi + 1) * c, :], preferred_element_type=jnp.float32)
            acc = part if acc is None else acc + part
        out_ref[...] = acc

    return pl.pallas_call(
        body,
        grid=(d // tn, b * nsb),
        in_specs=[pl.BlockSpec((1, g, tm, c), lambda j, i: (i // nsb, 0, i % nsb, 0)),
                  pl.BlockSpec((1, g, tm, c), lambda j, i: (i // nsb, 0, i % nsb, 0)),
                  pl.BlockSpec((g * c, tn), lambda j, i: (0, j))],
        out_specs=pl.BlockSpec((tm, tn), lambda j, i: (i, j)),
        out_shape=jax.ShapeDtypeStruct((b * s, d), jnp.float32),
        compiler_params=_params("arbitrary", "arbitrary"),
        name=name,
    )(o, gate, w)


def _resid_ln(h, x, g, b, alpha, tm, with_cdt, name):
    m, d = x.shape

    def body(h_ref, x_ref, g_ref, b_ref, *outs):
        y = alpha * x_ref[...] + h_ref[...]
        mu = jnp.mean(y, axis=-1, keepdims=True)
        yc = y - mu
        var = jnp.mean(yc * yc, axis=-1, keepdims=True)
        z = yc * lax.rsqrt(var + _LN_EPS) * g_ref[...] + b_ref[...]
        outs[0][...] = z
        if with_cdt:
            outs[1][...] = z.astype(_CDT)

    row = pl.BlockSpec((tm, d), lambda i: (i, 0))
    vec = pl.BlockSpec((1, d), lambda i: (0, 0))
    out_shape = [jax.ShapeDtypeStruct((m, d), jnp.float32)]
    out_specs = [row]
    if with_cdt:
        out_shape.append(jax.ShapeDtypeStruct((m, d), _CDT))
        out_specs.append(row)
    return pl.pallas_call(
        body,
        grid=(m // tm,),
        in_specs=[row, row, vec, vec],
        out_specs=out_specs,
        out_shape=out_shape,
        compiler_params=_params("arbitrary"),
        name=name,
    )(h, x, g.reshape(1, d), b.reshape(1, d))


def _col_reduce(x, op):
    rows, n = x.shape
    fold = 8 * _SUBLANES
    if rows > fold and rows % fold == 0:
        x = op(x.reshape(rows // fold, fold, n), axis=0)
    return op(x, axis=0, keepdims=True)


def _softmax_step(load_sc, m_ref, cols):
    m_old = m_ref[:, cols]
    m_new = jnp.maximum(m_old, _col_reduce(load_sc(), jnp.max))
    m_ref[:, cols] = m_new
    return jnp.exp2(load_sc() - m_new).astype(_CDT), jnp.exp2(m_old - m_new)


def _pv_step(vta, p, alpha, acc_ref, cols):
    pv = jnp.dot(vta, p, preferred_element_type=jnp.float32)
    acc_ref[:, cols] = pv if alpha is None else alpha * acc_ref[:, cols] + pv


def _float_sort_key(x):
    bits = pltpu.bitcast(x, jnp.int32)
    return bits ^ ((bits >> 31) & jnp.int32(0x7FFFFFFF))


def _dsa_core(q, qi, wt, kidx, k, vt, *, topk, tk1, tk, hchunk):
    b, n_heads, s, _ = q.shape
    idx_heads = qi.shape[1]
    n_kv = k.shape[1]
    tq = _LANES
    gq = _GROUP * tq
    log_s = int(math.log2(s))
    assert 1 << log_s == s and s % tk == 0 and tk % tk1 == 0 and tk1 % tq == 0
    assert idx_heads % hchunk == 0 and gq % _MXU_DIM == 0

    def body(q_ref, qi_ref, wt_ref, kidx_ref, k_ref, vt_ref, o_ref,
             key_ref, hi_ref, lo_ref, lom_ref, mask_ref, sc_ref, p_ref, al_ref, m_ref, acc_ref):
        i = pl.program_id(1)
        n1 = (i * tq + tq + tk1 - 1) // tk1
        n3 = (i * tq + tq + tk - 1) // tk

        q_pos1 = i * tq + lax.broadcasted_iota(jnp.int32, (tk1, tq), 1)
        s_iota1 = lax.broadcasted_iota(jnp.int32, (tk1, tq), 0)

        def score_tile(kt, carry):
            ks = pl.multiple_of(kt * tk1, tk1)
            kx = kidx_ref[0, pl.ds(ks, tk1), :]
            acc = jnp.zeros((tk1, tq), jnp.float32)
            for hc in range(idx_heads // hchunk):
                qq = qi_ref[0, hc * hchunk:(hc + 1) * hchunk].reshape(hchunk * tq, _HEAD_DIM)
                d = lax.dot_general(kx, qq, _NT, preferred_element_type=jnp.float32)
                for hh in range(hchunk):
                    h = hc * hchunk + hh
                    acc = acc + jnp.maximum(d[:, hh * tq:(hh + 1) * tq], 0.0) * wt_ref[0, h:h + 1, :]
            key = _float_sort_key(acc + 0.0)
            store_keys(ks, jnp.where(ks + s_iota1 <= q_pos1, key, _INT_MIN))
            return carry

        def store_keys(ks, key):
            key_ref[pl.ds(ks, tk1), :] = key
            hi_ref[pl.ds(ks, tk1), :] = (key >> 16).astype(jnp.int16)
            lo_ref[pl.ds(ks, tk1), :] = ((key & 0xFFFF) - _HALF_BIAS).astype(jnp.int16)

        lax.fori_loop(0, n1, score_tile, 0)

        def pad_tile(kt, carry):
            store_keys(pl.multiple_of(kt * tk1, tk1), jnp.full((tk1, tq), _INT_MIN, jnp.int32))
            return carry

        lax.fori_loop(n1, n3 * (tk // tk1), pad_tile, 0)

        q_pos = i * tq + lax.broadcasted_iota(jnp.int32, (tk, tq), 1)
        s_iota = lax.broadcasted_iota(jnp.int32, (tk, tq), 0)

        fold = 8 * _SUBLANES

        def count(pred_fn):
            def tile(kt, cnt):
                ks = pl.multiple_of(kt * tk, tk)
                c = jnp.where(pred_fn(key_ref[pl.ds(ks, tk), :], ks), 1.0, 0.0)
                return cnt + jnp.sum(c.reshape(tk // fold, fold, tq), axis=0)
            cnt = lax.fori_loop(0, n3, tile, jnp.zeros((fold, tq), jnp.float32))
            return jnp.sum(cnt, axis=0, keepdims=True)

        pack = 2 * _SUBLANES

        def count16(ref, pred_fn):
            def tile(kt, cnt):
                ks = pl.multiple_of(kt * tk, tk)
                x = ref[pl.ds(ks, tk), :].reshape(tk // pack, pack, tq)
                c = jnp.where(pred_fn(x), jnp.int16(1), jnp.int16(0))
                c = c.reshape(tk // (4 * pack), 4, pack, tq)
                for r in range(tk // (4 * pack)):
                    cnt = cnt + c[r]
                return cnt
            cnt = lax.fori_loop(0, n3, tile, jnp.zeros((4, pack, tq), jnp.int16))
            cnt = cnt.astype(jnp.int32).astype(jnp.float32).reshape(4 * pack, tq)
            return jnp.sum(cnt, axis=0, keepdims=True)

        def as_half(v):
            return jnp.broadcast_to(v, (pack, tq)).astype(jnp.int16)

        def half_select(ref, n_above):
            def step(t, prefix):
                cand = prefix | jnp.left_shift(jnp.int32(1), 15 - t)
                c16 = as_half(cand - _HALF_BIAS)
                cnt = n_above + count16(ref, lambda x: x >= c16)
                return jnp.where(cnt >= topk, cand, prefix)
            return lax.fori_loop(0, 16, step, jnp.zeros((1, tq), jnp.int32))

        thr_hi = half_select(hi_ref, 0.0) - _HALF_BIAS
        t16 = as_half(thr_hi)
        n_gt_hi = count16(hi_ref, lambda x: x > t16)

        def masked_low_tile(kt, carry):
            ks = pl.multiple_of(kt * tk, tk)
            h = hi_ref[pl.ds(ks, tk), :].reshape(tk // pack, pack, tq)
            l = lo_ref[pl.ds(ks, tk), :].reshape(tk // pack, pack, tq)
            lom_ref[pl.ds(ks, tk), :] = jnp.where(h == t16, l, jnp.int16(-_HALF_BIAS)).reshape(tk, tq)
            return carry

        lax.fori_loop(0, n3, masked_low_tile, 0)
        thr = thr_hi * (2 * _HALF_BIAS) + half_select(lom_ref, n_gt_hi)

        def tie_break():
            need = topk - count(lambda key, ks: key > thr)

            def index_step(t, j):
                cand = j | jnp.left_shift(jnp.int32(1), log_s - 1 - t)
                cnt = count(lambda key, ks: (key == thr) & (ks + s_iota < cand))
                return jnp.where(cnt < need, cand, j)

            return lax.fori_loop(0, log_s, index_step, jnp.zeros((1, tq), jnp.int32))

        n_ge = count(lambda key, ks: key >= thr)
        j_last = lax.cond(jnp.max(n_ge) > topk, tie_break,
                          lambda: jnp.full((1, tq), s, jnp.int32))

        def mask_tile(kt, carry):
            ks = pl.multiple_of(kt * tk, tk)
            key = key_ref[pl.ds(ks, tk), :]
            s_pos = ks + s_iota
            sel = (key > thr) | ((key == thr) & (s_pos <= j_last))
            mask_ref[pl.ds(ks, tk), :] = jnp.where(sel & (s_pos <= q_pos), 0.0, _MASKED).astype(_CDT)
            return carry

        lax.fori_loop(0, n3, mask_tile, 0)

        r = lax.broadcasted_iota(jnp.int32, (gq, tq), 0)
        c = lax.broadcasted_iota(jnp.int32, (gq, tq), 1)
        one_hot = jnp.where((r & (tq - 1)) == c, 1.0, 0.0).astype(_CDT)

        chunks = [slice(cc * _MXU_DIM, (cc + 1) * _MXU_DIM) for cc in range(gq // _MXU_DIM)]

        def kv_group(g, carry):
            qg = q_ref[0, pl.ds(g * _GROUP, _GROUP)].reshape(gq, _HEAD_DIM)
            qaug = jnp.concatenate([qg, one_hot], axis=1)
            m_ref[...] = jnp.full((1, gq), _M_INIT, jnp.float32)

            def logits(kt):
                kaug = jnp.concatenate([k_ref[0, g, pl.ds(kt * tk, tk), :],
                                        mask_ref[pl.ds(kt * tk, tk), :]], axis=1)
                for cols in chunks:
                    sc_ref[kt & 1, :, cols] = lax.dot_general(kaug, qaug[cols], _NT,
                                                              preferred_element_type=jnp.float32)

            def softmax(kt):
                for cols in chunks:
                    p_ref[kt & 1, :, cols], al_ref[kt & 1, :, cols] = _softmax_step(
                        lambda: sc_ref[kt & 1, :, cols], m_ref, cols)

            def pv(kt):
                vta = vt_ref[0, g, kt]
                for cols in chunks:
                    _pv_step(vta, p_ref[kt & 1, :, cols], None if kt == 0 else al_ref[kt & 1, :, cols],
                             acc_ref, cols)

            for n_tiles in range(1, s // tk + 1):
                @pl.when(n3 == n_tiles)
                def _():
                    for step in range(n_tiles + 2):
                        if step < n_tiles:
                            logits(step)
                        if 1 <= step <= n_tiles:
                            softmax(step - 1)
                        if step >= 2:
                            pv(step - 2)

            acc = acc_ref[...]
            ot = acc[:_HEAD_DIM] * (1.0 / acc[_HEAD_DIM:_HEAD_DIM + 1])
            for hh in range(_GROUP):
                o_ref[0, g, :, hh * _HEAD_DIM:(hh + 1) * _HEAD_DIM] = (
                    ot[:, hh * tq:(hh + 1) * tq].T.astype(o_ref.dtype))
            return carry

        lax.fori_loop(0, n_kv, kv_group, 0)

    nq = s // tq
    return pl.pallas_call(
        body,
        grid=(b, nq),
        in_specs=[pl.BlockSpec((1, n_heads, tq, _HEAD_DIM), lambda bi, i: (bi, 0, i, 0)),
                  pl.BlockSpec((1, idx_heads, tq, _HEAD_DIM), lambda bi, i: (bi, 0, i, 0)),
                  pl.BlockSpec((1, idx_heads, tq), lambda bi, i: (bi, 0, i)),
                  pl.BlockSpec((1, s, _HEAD_DIM), lambda bi, i: (bi, 0, 0)),
                  pl.BlockSpec((1, n_kv, s, _HEAD_DIM), lambda bi, i: (bi, 0, 0, 0)),
                  pl.BlockSpec((1, n_kv, s // tk, _VT_ROWS, tk), lambda bi, i: (bi, 0, 0, 0, 0))],
        out_specs=pl.BlockSpec((1, n_kv, tq, _GROUP * _HEAD_DIM), lambda bi, i: (bi, 0, i, 0)),
        out_shape=jax.ShapeDtypeStruct((b, n_kv, s, _GROUP * _HEAD_DIM), _CDT),
        scratch_shapes=[pltpu.VMEM((s, tq), jnp.int32),
                        pltpu.VMEM((s, tq), jnp.int16),
                        pltpu.VMEM((s, tq), jnp.int16),
                        pltpu.VMEM((s, tq), jnp.int16),
                        pltpu.VMEM((s, tq), _CDT),
                        pltpu.VMEM((2, tk, gq), jnp.float32),
                        pltpu.VMEM((2, tk, gq), _CDT),
                        pltpu.VMEM((2, 1, gq), jnp.float32),
                        pltpu.VMEM((1, gq), jnp.float32),
                        pltpu.VMEM((_VT_ROWS, gq), jnp.float32)],
        compiler_params=_params("arbitrary", "arbitrary"),
        name="dsa_core",
    )(q, qi, wt, kidx, k, vt)


def _cumsum_aug(lf, n_heads):
    b, s, w = lf.shape
    blk = _LANES

    def body(lf_ref, out_ref, carry_ref):
        @pl.when(pl.program_id(1) == 0)
        def _():
            carry_ref[...] = jnp.zeros_like(carry_ref)
        r = lax.broadcasted_iota(jnp.int32, (blk, blk), 0)
        c = lax.broadcasted_iota(jnp.int32, (blk, blk), 1)
        tri = jnp.where(r >= c, 1.0, 0.0).astype(jnp.float32)
        cs = jnp.dot(tri, lf_ref[0], preferred_element_type=jnp.float32,
                     precision=lax.Precision.HIGHEST) + carry_ref[...]
        carry_ref[...] = cs[blk - 1:blk, :]
        c2 = cs * _LOG2E
        for h in range(n_heads):
            col = c2[:, h:h + 1]
            hi = col.astype(_CDT).astype(jnp.float32)
            mid = (col - hi).astype(_CDT).astype(jnp.float32)
            lo = (col - hi - mid).astype(_CDT).astype(jnp.float32)
            tile = jnp.where(c == 0, hi, jnp.where(c == 1, mid, jnp.where(c == 2, lo, 0.0)))
            out_ref[0, h] = tile.astype(out_ref.dtype)

    return pl.pallas_call(
        body,
        grid=(b, s // blk),
        in_specs=[pl.BlockSpec((1, blk, w), lambda bi, i: (bi, i, 0))],
        out_specs=pl.BlockSpec((1, n_heads, blk, _LANES), lambda bi, i: (bi, 0, i, 0)),
        out_shape=jax.ShapeDtypeStruct((b, n_heads, s, _LANES), _CDT),
        scratch_shapes=[pltpu.VMEM((1, w), jnp.float32)],
        compiler_params=_params("arbitrary", "arbitrary"),
        name="fox_cumsum",
    )(lf)


def _fox_core(q, k, caug, vt, *, t):
    b, n_heads, s, _ = q.shape
    assert t % _MXU_DIM == 0 and s % t == 0

    chunks = [slice(cc * _MXU_DIM, (cc + 1) * _MXU_DIM) for cc in range(t // _MXU_DIM)]

    tiles = [(i, j) for i in range(s // t) for j in range(i + 1)]

    def visible(i, j, cols):
        return min(t, cols.stop) if i == j else t

    def body(q_ref, k_ref, c_ref, vt_ref, o_ref, kaug_ref, sc_ref, p_ref, al_ref, m_ref, acc_ref):
        kaug_ref[:, :_HEAD_DIM] = k_ref[0, 0]
        kaug_ref[:, _HEAD_DIM:] = c_ref[0, 0]
        lane = lax.broadcasted_iota(jnp.int32, (t, _HEAD_DIM), 1)
        minus_ones = jnp.where(lane < 3, -1.0, 0.0).astype(_CDT)

        def logits(n):
            i, j = tiles[n]
            qaug = jnp.concatenate([q_ref[0, 0, i * t:(i + 1) * t, :], minus_ones], axis=1)
            ka = kaug_ref[j * t:(j + 1) * t, :]
            for cols in chunks:
                kr = visible(i, j, cols)
                sc_ref[n & 1, :kr, cols] = lax.dot_general(ka[:kr], qaug[cols], _NT,
                                                           preferred_element_type=jnp.float32)

        def softmax(n):
            i, j = tiles[n]
            if j == 0:
                m_ref[...] = jnp.full((1, t), _M_INIT, jnp.float32)
            for cols in chunks:
                kr = visible(i, j, cols)

                def load_sc(cols=cols, kr=kr):
                    sc = sc_ref[n & 1, :kr, cols]
                    if i == j:
                        key_i = lax.broadcasted_iota(jnp.int32, (kr, _MXU_DIM), 0)
                        qry_i = cols.start + lax.broadcasted_iota(jnp.int32, (kr, _MXU_DIM), 1)
                        sc = jnp.where(key_i <= qry_i, sc, -jnp.inf)
                    return sc

                p_ref[n & 1, :kr, cols], al_ref[n & 1, :, cols] = _softmax_step(load_sc, m_ref, cols)

        def pv(n):
            i, j = tiles[n]
            vta = vt_ref[0, 0, j]
            for cols in chunks:
                kr = visible(i, j, cols)
                _pv_step(vta[:, :kr], p_ref[n & 1, :kr, cols],
                         None if j == 0 else al_ref[n & 1, :, cols], acc_ref, cols)
            if j == i:
                acc = acc_ref[...]
                ot = acc[:_HEAD_DIM] * (1.0 / acc[_HEAD_DIM:_HEAD_DIM + 1])
                for cq in range(t // _LANES):
                    o_ref[0, i * t + cq * _LANES:i * t + (cq + 1) * _LANES, :] = (
                        ot[:, cq * _LANES:(cq + 1) * _LANES].T.astype(o_ref.dtype))

        for step in range(len(tiles) + 2):
            if step < len(tiles):
                logits(step)
            if 1 <= step <= len(tiles):
                softmax(step - 1)
            if step >= 2:
                pv(step - 2)

    head = pl.BlockSpec((1, 1, s, _HEAD_DIM), lambda bi, h: (bi, h, 0, 0))
    return pl.pallas_call(
        body,
        grid=(b, n_heads),
        in_specs=[head, head, head,
                  pl.BlockSpec((1, 1, s // t, _VT_ROWS, t), lambda bi, h: (bi, h, 0, 0, 0))],
        out_specs=pl.BlockSpec((1, s, _HEAD_DIM), lambda bi, h: (bi, 0, h)),
        out_shape=jax.ShapeDtypeStruct((b, s, n_heads * _HEAD_DIM), _CDT),
        scratch_shapes=[pltpu.VMEM((s, 2 * _HEAD_DIM), _CDT),
                        pltpu.VMEM((2, t, t), jnp.float32),
                        pltpu.VMEM((2, t, t), _CDT),
                        pltpu.VMEM((2, 1, t), jnp.float32),
                        pltpu.VMEM((1, t), jnp.float32),
                        pltpu.VMEM((_VT_ROWS, t), jnp.float32)],
        compiler_params=_params("arbitrary", "arbitrary"),
        name="fox_core",
    )(q, k, caug, vt)


def _rope_tables(s):
    half = _HEAD_DIM // 2
    inv = _ROPE_THETA ** (-jnp.arange(half, dtype=jnp.float32) / half)
    ang = jnp.arange(s, dtype=jnp.float32)[:, None] * inv[None, :]
    cos, sin = jnp.cos(ang), jnp.sin(ang)
    return jnp.concatenate([cos, cos], axis=1), jnp.concatenate([-sin, sin], axis=1)


def _placed(cols, s, tn_cap=None):
    tm, tn = _proj_tiles(s, cols)
    if tn_cap is not None:
        tn = min(tn, tn_cap)
    if not cols.aligned(tn):
        cols = cols.materialize()
        tm, tn = _proj_tiles(s, cols)
        if tn_cap is not None:
            tn = min(tn, tn_cap)
    return cols, tm, tn


def _heads_call(a, cols, b, s, *, rope, scale, cos, sin, name):
    cols, tm, tn = _placed(cols, s)
    nsb = s // tm
    extras, extra_specs = (), ()
    if rope:
        extras = (cos, sin)
        extra_specs = (pl.BlockSpec((tm, _HEAD_DIM), lambda j, i: (i % nsb, 0)),) * 2
    nh_t = tn // _HEAD_DIM
    return _proj(
        a, cols, functools.partial(_ep_heads, rope=rope, scale=scale), extras, extra_specs,
        jax.ShapeDtypeStruct((b, cols.n // _HEAD_DIM, s, _HEAD_DIM), _CDT),
        pl.BlockSpec((1, nh_t, tm, _HEAD_DIM), lambda j, i: (i // nsb, j, i % nsb, 0)),
        tm, tn, name, row_chunk=min(_ROW_CHUNK, tm))


def _vt_call(a, cols, b, s, t, name):
    cols, tm, tn = _placed(cols, s)
    nsb = s // tm
    nh_t = tn // _HEAD_DIM
    return _proj(
        a, cols, functools.partial(_ep_vt, t=t), (), (),
        jax.ShapeDtypeStruct((b, cols.n // _HEAD_DIM, s // t, _VT_ROWS, t), _CDT),
        pl.BlockSpec((1, nh_t, tm // t, _VT_ROWS, t), lambda j, i: (i // nsb, j, i % nsb, 0, 0)),
        tm, tn, name, row_chunk=t)


def _gate_call(a, cols, b, s, groups, name):
    c = cols.n // groups
    cols, tm, tn = _placed(cols, s, tn_cap=c)
    nsb = s // tm
    per = c // tn
    return _proj(
        a, cols, _ep_silu, (), (),
        jax.ShapeDtypeStruct((b, groups, s, c), _CDT),
        pl.BlockSpec((1, 1, tm, tn), lambda j, i: (i // nsb, j // per, i % nsb, j % per)),
        tm, tn, name, row_chunk=min(_ROW_CHUNK, tm))


def _dsa_layer(xc, w_in_t, q_norm_g, w_uq, kidx_g, kidx_b, w_out, layer, b, s):
    d = xc.shape[1]
    q_lora = w_uq.shape[1]
    branch = w_out.shape[1]
    idx_heads = (w_uq.shape[2] - branch) // _HEAD_DIM
    kv = (w_in_t.shape[1] - q_lora - _HEAD_DIM - idx_heads - branch) // 2
    w_in = functools.partial(_Cols, w_in_t, layer=layer, transposed=True)
    n_kv = kv // _HEAD_DIM
    n_heads = branch // _HEAD_DIM
    assert n_heads == n_kv * _GROUP and idx_heads <= _LANES
    topk = min(_TOPK_MAX, s // 4)
    o0 = q_lora
    o1 = o0 + 2 * kv
    o2 = o1 + _HEAD_DIM
    o3 = o2 + idx_heads
    cos, sin = _rope_tables(s)
    tk1 = min(256, s)
    tk = min(512, s)

    c_cq = w_in(0, q_lora).materialize()
    tm = min(512, s)
    cq = _proj(xc, c_cq, _ep_rms, (q_norm_g[layer].reshape(1, q_lora),),
               (pl.BlockSpec((1, q_lora), lambda j, i: (0, 0)),),
               jax.ShapeDtypeStruct((b * s, q_lora), _CDT),
               pl.BlockSpec((tm, q_lora), lambda j, i: (i, 0)), tm, q_lora, "dsa_cq")
    k = _heads_call(xc, w_in(o0, kv), b, s, rope=True, scale=1.0, cos=cos, sin=sin, name="dsa_k")
    vt = _vt_call(xc, w_in(o0 + kv, kv), b, s, tk, "dsa_v")

    c_ki = w_in(o1, 2 * _LANES)
    if not c_ki.aligned(2 * _LANES):
        w_ki = jnp.pad(w_in_t[layer][o1:o3, :], ((0, _LANES - idx_heads), (0, 0)))
        c_ki = _Cols(w_ki.astype(_CDT), transposed=True)
    nsb = s // tm
    rope_specs = (pl.BlockSpec((tm, _HEAD_DIM), lambda j, i: (i % nsb, 0)),) * 2
    w_scale = idx_heads ** -0.5 * _HEAD_DIM ** -0.5
    vec = pl.BlockSpec((1, _HEAD_DIM), lambda j, i: (0, 0))
    kidx, wt = _proj(xc, c_ki, functools.partial(_ep_kidx, idx_heads=idx_heads, w_scale=w_scale),
                     (cos, sin, kidx_g[layer].reshape(1, _HEAD_DIM), kidx_b[layer].reshape(1, _HEAD_DIM)),
                     rope_specs + (vec, vec),
                     [jax.ShapeDtypeStruct((b, s, _HEAD_DIM), _CDT),
                      jax.ShapeDtypeStruct((b, idx_heads, s), jnp.float32)],
                     [pl.BlockSpec((1, tm, _HEAD_DIM), lambda j, i: (i // nsb, i % nsb, 0)),
                      pl.BlockSpec((1, idx_heads, tm), lambda j, i: (i // nsb, 0, i % nsb))],
                     tm, 2 * _LANES, "dsa_kidx")
    gate = _gate_call(xc, w_in(o3, branch), b, s, n_kv, "dsa_gate")
    q = _heads_call(cq, _Cols(w_uq, 0, branch, layer), b, s, rope=True,
                    scale=_HEAD_DIM ** -0.5 * _LOG2E, cos=cos, sin=sin, name="dsa_q")
    qi = _heads_call(cq, _Cols(w_uq, branch, None, layer), b, s, rope=True, scale=1.0, cos=cos, sin=sin,
                     name="dsa_qi")
    o = _dsa_core(q, qi, wt, kidx, k, vt, topk=topk, tk1=tk1, tk=tk, hchunk=min(8, idx_heads))
    return _outproj(o, gate, w_out[layer].astype(_CDT), min(512, s), min(1024, d), "dsa_out")


def _fox_layer(xc, w_in_t, forget_bias, w_out, layer, b, s):
    d = xc.shape[1]
    branch = w_out.shape[1]
    n_heads = forget_bias.shape[1]
    assert branch == n_heads * _HEAD_DIM and n_heads <= _LANES
    w_in = functools.partial(_Cols, w_in_t, layer=layer, transposed=True)
    w_f = jnp.pad(w_in_t[layer][4 * branch:, :], ((0, _LANES - n_heads), (0, 0)))
    fb = jnp.pad(forget_bias[layer], (0, _LANES - n_heads)).reshape(1, _LANES)
    tm = min(512, s)
    nsb = s // tm
    t = min(512, s)
    q = _heads_call(xc, w_in(0, branch), b, s, rope=False,
                    scale=_HEAD_DIM ** -0.5 * _LOG2E, cos=None, sin=None, name="fox_q")
    k = _heads_call(xc, w_in(branch, branch), b, s, rope=False, scale=1.0,
                    cos=None, sin=None, name="fox_k")
    vt = _vt_call(xc, w_in(2 * branch, branch), b, s, t, "fox_v")
    gate = _gate_call(xc, w_in(3 * branch, branch), b, s, 1, "fox_gate")
    lf = _proj(xc, _Cols(w_f, transposed=True), _ep_logf, (fb,), (pl.BlockSpec((1, _LANES), lambda j, i: (0, 0)),),
               jax.ShapeDtypeStruct((b, s, _LANES), jnp.float32),
               pl.BlockSpec((1, tm, _LANES), lambda j, i: (i // nsb, i % nsb, 0)),
               tm, _LANES, "fox_logf")
    caug = _cumsum_aug(lf, n_heads)
    o = _fox_core(q, k, caug, vt, t=t)
    return _outproj(o.reshape(b, 1, s, branch), gate, w_out[layer].astype(_CDT), tm, min(1024, d),
                    "fox_out")


def kernel(x, a_w_in, a_q_norm_g, a_w_uq, a_kidx_norm_g, a_kidx_norm_b, a_w_out,
           b_w_in, b_forget_bias, b_w_out, ln_g, ln_b):
    b, s, d = x.shape
    depth = ln_g.shape[0]
    alpha = (2 * depth) ** 0.25
    xf = x.reshape(b * s, d)
    xc = xf.astype(_CDT)
    a_w_in_t = jnp.swapaxes(a_w_in, 1, 2)
    b_w_in_t = jnp.swapaxes(b_w_in, 1, 2)
    tm_ln = min(256, s)
    for i in range(depth):
        j = i // 2
        if i % 2 == 0:
            h = _dsa_layer(xc, a_w_in_t, a_q_norm_g, a_w_uq, a_kidx_norm_g, a_kidx_norm_b, a_w_out, j, b, s)
        else:
            h = _fox_layer(xc, b_w_in_t, b_forget_bias, b_w_out, j, b, s)
        last = i == depth - 1
        outs = _resid_ln(h, xf, ln_g[i], ln_b[i], alpha, tm_ln, not last, "resid_ln_%d" % i)
        xf = outs[0]
        if not last:
            xc = outs[1]
    return xf.reshape(b, s, d)
```

```python
import functools
import math

import jax
import jax.numpy as jnp
from jax import lax
from jax.experimental import pallas as pl
from jax.experimental.pallas import tpu as pltpu

_CDT = jnp.bfloat16
_HEAD_DIM = 128
_GROUP = 8
_TOPK_MAX = 256
_ROPE_THETA = 10000.0
_LN_EPS = 1e-5
_RMS_EPS = 1e-6
_LANES = 128
_SUBLANES = 8
_MXU_DIM = 256
_VMEM_LIMIT = 56 * 1024 * 1024
_PROJ_TILE_BUDGET = 40 * 1024 * 1024
_VT_ROWS = _HEAD_DIM + 16
_INT_MIN = -2 ** 31
_MASKED = -1e30
_M_INIT = -1e29
_LOG2E = math.log2(math.e)
_ROW_CHUNK = 256

_NT = (((1,), (1,)), ((), ()))


def _params(*sem):
    return pltpu.CompilerParams(dimension_semantics=sem, vmem_limit_bytes=_VMEM_LIMIT)


def _rope_tile(x, cos, sin_signed):
    return x * cos + pltpu.roll(x, _HEAD_DIM // 2, axis=1) * sin_signed


class _Cols:
    def __init__(self, w, col0=0, n=None, layer=None, transposed=False):
        self.w, self.col0, self.layer, self.transposed = w, col0, layer, transposed
        self.n_total = w.shape[-2] if transposed else w.shape[-1]
        self.k = w.shape[-1] if transposed else w.shape[-2]
        self.n = self.n_total - col0 if n is None else n

    def block_aligned(self, tn):
        return self.col0 % tn == 0

    def aligned(self, tn):
        start_ok = self.col0 % _SUBLANES == 0 if self.transposed else self.block_aligned(tn)
        return start_ok and self.col0 + -(-self.n // tn) * tn <= self.n_total

    def materialize(self):
        w = self.w if self.layer is None else self.w[self.layer]
        if self.transposed:
            return _Cols(w[self.col0:self.col0 + self.n, :].astype(_CDT), transposed=True)
        return _Cols(w[:, self.col0:self.col0 + self.n].astype(_CDT))


def _proj_tiles(s, cols):
    itemsize = cols.w.dtype.itemsize
    act = jnp.dtype(_CDT).itemsize
    for tn in (1024, 512, 256, 128):
        if tn > cols.n:
            continue
        w_bytes = cols.k * tn * (2 * itemsize + (act if cols.w.dtype != _CDT else 0))
        for tm in (1024, 512, 256):
            if tm <= s and w_bytes + 2 * tm * cols.k * act <= _PROJ_TILE_BUDGET:
                return tm, tn
    raise ValueError("projection tiles do not fit VMEM")


def _proj(a, cols, epilogue, extras, extra_specs, out_shape, out_specs, tm, tn, name, row_chunk=None):
    m, k = a.shape
    assert cols.aligned(tn) and k == cols.k
    convert = cols.w.dtype != _CDT
    n_extra = len(extras)
    w_block = (tn, k) if cols.transposed else (k, tn)

    def body(a_ref, w_ref, *rest):
        if convert:
            wc_ref = rest[-1]
            rest = rest[:-1]

            @pl.when(pl.program_id(1) == 0)
            def _():
                wc_ref[...] = w_ref[...].reshape(w_block).astype(_CDT)

            w = wc_ref[...]
        else:
            w = w_ref[...].reshape(w_block)
        rc = tm if row_chunk is None else row_chunk
        for r in range(tm // rc):
            rows = slice(r * rc, (r + 1) * rc)
            if cols.transposed:
                res = lax.dot_general(a_ref[rows, :], w, _NT, preferred_element_type=jnp.float32)
            else:
                res = jnp.dot(a_ref[rows, :], w, preferred_element_type=jnp.float32)
            epilogue(res, rest[:n_extra], rest[n_extra:], rows)

    lead_index = () if cols.layer is None else (cols.layer,)
    if cols.block_aligned(tn):
        c0 = cols.col0 // tn
        lead = () if cols.layer is None else (None,)
        if cols.transposed:
            w_spec = pl.BlockSpec(lead + (tn, k), lambda j, i: lead_index + (c0 + j, 0))
        else:
            w_spec = pl.BlockSpec(lead + (k, tn), lambda j, i: lead_index + (0, c0 + j))
    else:
        assert cols.transposed
        lead = () if cols.layer is None else (pl.Element(1),)
        w_spec = pl.BlockSpec(lead + (pl.Element(tn), pl.Element(k)),
                              lambda j, i: lead_index + (pl.multiple_of(cols.col0 + j * tn, _SUBLANES), 0))
    return pl.pallas_call(
        body,
        grid=(-(-cols.n // tn), m // tm),
        in_specs=[pl.BlockSpec((tm, k), lambda j, i: (i, 0)), w_spec] + list(extra_specs),
        out_specs=out_specs,
        out_shape=out_shape,
        scratch_shapes=[pltpu.VMEM(w_block, _CDT)] if convert else [],
        compiler_params=_params("arbitrary", "arbitrary"),
        name=name,
    )(a, cols.w, *extras)


def _ep_rms(res, extras, outs, rows):
    (g_ref,) = extras
    (o_ref,) = outs
    y = res * lax.rsqrt(jnp.mean(res * res, axis=-1, keepdims=True) + _RMS_EPS)
    o_ref[rows, :] = (y * g_ref[...]).astype(o_ref.dtype)


def _ep_heads(res, extras, outs, rows, *, rope, scale):
    (o_ref,) = outs
    nh = res.shape[1] // _HEAD_DIM
    if rope:
        cos = extras[0][rows, :]
        sin = extras[1][rows, :]
    for h in range(nh):
        xh = res[:, h * _HEAD_DIM:(h + 1) * _HEAD_DIM]
        if rope:
            xh = _rope_tile(xh, cos, sin)
        if scale != 1.0:
            xh = xh * scale
        o_ref[0, h, rows, :] = xh.astype(o_ref.dtype)


def _ep_vt(res, extras, outs, rows, *, t):
    (vt_ref,) = outs
    assert rows.start % t == 0 and res.shape[0] % t == 0
    for h in range(res.shape[1] // _HEAD_DIM):
        vh = res[:, h * _HEAD_DIM:(h + 1) * _HEAD_DIM]
        for c in range(res.shape[0] // t):
            dst = rows.start // t + c
            vt_ref[0, h, dst, :_HEAD_DIM, :] = vh[c * t:(c + 1) * t, :].T.astype(vt_ref.dtype)
            vt_ref[0, h, dst, _HEAD_DIM:, :] = jnp.ones((_VT_ROWS - _HEAD_DIM, t), vt_ref.dtype)


def _ep_kidx(res, extras, outs, rows, *, idx_heads, w_scale):
    cos_ref, sin_ref, g_ref, b_ref = extras
    kidx_ref, wt_ref = outs
    x = res[:, :_HEAD_DIM]
    mu = jnp.mean(x, axis=-1, keepdims=True)
    xc = x - mu
    var = jnp.mean(xc * xc, axis=-1, keepdims=True)
    y = xc * lax.rsqrt(var + _LN_EPS) * g_ref[...] + b_ref[...]
    kidx_ref[0] = _rope_tile(y, cos_ref[...], sin_ref[...]).astype(kidx_ref.dtype)
    wt = (res[:, _HEAD_DIM:] * w_scale).T
    wt_ref[0] = wt[:idx_heads, :]


def _ep_silu(res, extras, outs, rows):
    (o_ref,) = outs
    o_ref[0, 0, rows, :] = (res * jax.nn.sigmoid(res)).astype(o_ref.dtype)


def _ep_logf(res, extras, outs, rows):
    (bias_ref,) = extras
    (o_ref,) = outs
    z = res + bias_ref[...]
    o_ref[0] = jnp.minimum(z, 0.0) - jnp.log(1.0 + jnp.exp(-jnp.abs(z)))


def _outproj(o, gate, w, tm, tn, name):
    b, g, s, c = o.shape
    d = w.shape[1]
    nsb = s // tm

    def body(o_ref, g_ref, w_ref, out_ref):
        acc = None
        for gi in range(g):
            a = (o_ref[0, gi].astype(jnp.float32) * g_ref[0, gi].astype(jnp.float32)).astype(_CDT)
            part = jnp.dot(a, w_ref[gi * c:(gi + 1) * c, :], preferred_element_type=jnp.float32)
            acc = part if acc is None else acc + part
        out_ref[...] = acc

    return pl.pallas_call(
        body,
        grid=(d // tn, b * nsb),
        in_specs=[pl.BlockSpec((1, g, tm, c), lambda j, i: (i // nsb, 0, i % nsb, 0)),
                  pl.BlockSpec((1, g, tm, c), lambda j, i: (i // nsb, 0, i % nsb, 0)),
                  pl.BlockSpec((g * c, tn), lambda j, i: (0, j))],
        out_specs=pl.BlockSpec((tm, tn), lambda j, i: (i, j)),
        out_shape=jax.ShapeDtypeStruct((b * s, d), jnp.float32),
        compiler_params=_params("arbitrary", "arbitrary"),
        name=name,
    )(o, gate, w)


def _resid_ln(h, x, g, b, alpha, tm, with_cdt, name):
    m, d = x.shape

    def body(h_ref, x_ref, g_ref, b_ref, *outs):
        y = alpha * x_ref[...] + h_ref[...]
        mu = jnp.mean(y, axis=-1, keepdims=True)
        yc = y - mu
        var = jnp.mean(yc * yc, axis=-1, keepdims=True)
        z = yc * lax.rsqrt(var + _LN_EPS) * g_ref[...] + b_ref[...]
        outs[0][...] = z
        if with_cdt:
            outs[1][...] = z.astype(_CDT)

    row = pl.BlockSpec((tm, d), lambda i: (i, 0))
    vec = pl.BlockSpec((1, d), lambda i: (0, 0))
    out_shape = [jax.ShapeDtypeStruct((m, d), jnp.float32)]
    out_specs = [row]
    if with_cdt:
        out_shape.append(jax.ShapeDtypeStruct((m, d), _CDT))
        out_specs.append(row)
    return pl.pallas_call(
        body,
        grid=(m // tm,),
        in_specs=[row, row, vec, vec],
        out_specs=out_specs,
        out_shape=out_shape,
        compiler_params=_params("arbitrary"),
        name=name,
    )(h, x, g.reshape(1, d), b.reshape(1, d))


def _col_reduce(x, op):
    rows, n = x.shape
    fold = 8 * _SUBLANES
    if rows > fold and rows % fold == 0:
        x = op(x.reshape(rows // fold, fold, n), axis=0)
    return op(x, axis=0, keepdims=True)


def _softmax_step(load_sc, m_ref, cols):
    m_old = m_ref[:, cols]
    m_new = jnp.maximum(m_old, _col_reduce(load_sc(), jnp.max))
    m_ref[:, cols] = m_new
    return jnp.exp2(load_sc() - m_new).astype(_CDT), jnp.exp2(m_old - m_new)


def _pv_step(vta, p, alpha, acc_ref, cols):
    pv = jnp.dot(vta, p, preferred_element_type=jnp.float32)
    acc_ref[:, cols] = pv if alpha is None else alpha * acc_ref[:, cols] + pv


def _float_sort_key(x):
    bits = pltpu.bitcast(x, jnp.int32)
    return bits ^ ((bits >> 31) & jnp.int32(0x7FFFFFFF))


def _dsa_core(q, qi, wt, kidx, k, vt, *, topk, tk1, tk, hchunk):
    b, n_heads, s, _ = q.shape
    idx_heads = qi.shape[1]
    n_kv = k.shape[1]
    tq = _LANES
    gq = _GROUP * tq
    log_s = int(math.log2(s))
    assert 1 << log_s == s and s % tk == 0 and tk % tk1 == 0 and tk1 % tq == 0
    assert idx_heads % hchunk == 0 and gq % _MXU_DIM == 0

    def body(q_ref, qi_ref, wt_ref, kidx_ref, k_ref, vt_ref, o_ref,
             key_ref, mask_ref, sc_ref, p_ref, al_ref, m_ref, acc_ref):
        i = pl.program_id(1)
        n1 = (i * tq + tq + tk1 - 1) // tk1
        n3 = (i * tq + tq + tk - 1) // tk

        q_pos1 = i * tq + lax.broadcasted_iota(jnp.int32, (tk1, tq), 1)
        s_iota1 = lax.broadcasted_iota(jnp.int32, (tk1, tq), 0)

        def score_tile(kt, carry):
            ks = pl.multiple_of(kt * tk1, tk1)
            kx = kidx_ref[0, pl.ds(ks, tk1), :]
            acc = jnp.zeros((tk1, tq), jnp.float32)
            for hc in range(idx_heads // hchunk):
                qq = qi_ref[0, hc * hchunk:(hc + 1) * hchunk].reshape(hchunk * tq, _HEAD_DIM)
                d = lax.dot_general(kx, qq, _NT, preferred_element_type=jnp.float32)
                for hh in range(hchunk):
                    h = hc * hchunk + hh
                    acc = acc + jnp.maximum(d[:, hh * tq:(hh + 1) * tq], 0.0) * wt_ref[0, h:h + 1, :]
            key = _float_sort_key(acc + 0.0)
            key_ref[pl.ds(ks, tk1), :] = jnp.where(ks + s_iota1 <= q_pos1, key, _INT_MIN)
            return carry

        lax.fori_loop(0, n1, score_tile, 0)

        def pad_tile(kt, carry):
            key_ref[pl.ds(pl.multiple_of(kt * tk1, tk1), tk1), :] = jnp.full((tk1, tq), _INT_MIN, jnp.int32)
            return carry

        lax.fori_loop(n1, n3 * (tk // tk1), pad_tile, 0)

        q_pos = i * tq + lax.broadcasted_iota(jnp.int32, (tk, tq), 1)
        s_iota = lax.broadcasted_iota(jnp.int32, (tk, tq), 0)

        fold = 8 * _SUBLANES

        def count(pred_fn):
            def tile(kt, cnt):
                ks = pl.multiple_of(kt * tk, tk)
                c = jnp.where(pred_fn(key_ref[pl.ds(ks, tk), :], ks), 1.0, 0.0)
                return cnt + jnp.sum(c.reshape(tk // fold, fold, tq), axis=0)
            cnt = lax.fori_loop(0, n3, tile, jnp.zeros((fold, tq), jnp.float32))
            return jnp.sum(cnt, axis=0, keepdims=True)

        def value_step(t, prefix):
            cand = prefix | jnp.left_shift(jnp.int32(1), 31 - t)
            cand_s = cand ^ _INT_MIN
            cnt = count(lambda key, ks: key >= cand_s)
            return jnp.where(cnt >= topk, cand, prefix)

        prefix = lax.fori_loop(0, 32, value_step, jnp.zeros((1, tq), jnp.int32))
        thr = prefix ^ _INT_MIN

        def tie_break():
            need = topk - count(lambda key, ks: key > thr)

            def index_step(t, j):
                cand = j | jnp.left_shift(jnp.int32(1), log_s - 1 - t)
                cnt = count(lambda key, ks: (key == thr) & (ks + s_iota < cand))
                return jnp.where(cnt < need, cand, j)

            return lax.fori_loop(0, log_s, index_step, jnp.zeros((1, tq), jnp.int32))

        n_ge = count(lambda key, ks: key >= thr)
        j_last = lax.cond(jnp.max(n_ge) > topk, tie_break,
                          lambda: jnp.full((1, tq), s, jnp.int32))

        def mask_tile(kt, carry):
            ks = pl.multiple_of(kt * tk, tk)
            key = key_ref[pl.ds(ks, tk), :]
            s_pos = ks + s_iota
            sel = (key > thr) | ((key == thr) & (s_pos <= j_last))
            mask_ref[pl.ds(ks, tk), :] = jnp.where(sel & (s_pos <= q_pos), 0.0, _MASKED).astype(_CDT)
            return carry

        lax.fori_loop(0, n3, mask_tile, 0)

        r = lax.broadcasted_iota(jnp.int32, (gq, tq), 0)
        c = lax.broadcasted_iota(jnp.int32, (gq, tq), 1)
        one_hot = jnp.where((r & (tq - 1)) == c, 1.0, 0.0).astype(_CDT)

        chunks = [slice(cc * _MXU_DIM, (cc + 1) * _MXU_DIM) for cc in range(gq // _MXU_DIM)]

        def kv_group(g, carry):
            qg = q_ref[0, pl.ds(g * _GROUP, _GROUP)].reshape(gq, _HEAD_DIM)
            qaug = jnp.concatenate([qg, one_hot], axis=1)
            m_ref[...] = jnp.full((1, gq), _M_INIT, jnp.float32)

            def logits(kt):
                kaug = jnp.concatenate([k_ref[0, g, pl.ds(kt * tk, tk), :],
                                        mask_ref[pl.ds(kt * tk, tk), :]], axis=1)
                for cols in chunks:
                    sc_ref[kt & 1, :, cols] = lax.dot_general(kaug, qaug[cols], _NT,
                                                              preferred_element_type=jnp.float32)

            def softmax(kt):
                for cols in chunks:
                    p_ref[kt & 1, :, cols], al_ref[kt & 1, :, cols] = _softmax_step(
                        lambda: sc_ref[kt & 1, :, cols], m_ref, cols)

            def pv(kt):
                vta = vt_ref[0, g, kt]
                for cols in chunks:
                    _pv_step(vta, p_ref[kt & 1, :, cols], None if kt == 0 else al_ref[kt & 1, :, cols],
                             acc_ref, cols)

            for n_tiles in range(1, s // tk + 1):
                @pl.when(n3 == n_tiles)
                def _():
                    for step in range(n_tiles + 2):
                        if step < n_tiles:
                            logits(step)
                        if 1 <= step <= n_tiles:
                            softmax(step - 1)
                        if step >= 2:
                            pv(step - 2)

            acc = acc_ref[...]
            ot = acc[:_HEAD_DIM] * (1.0 / acc[_HEAD_DIM:_HEAD_DIM + 1])
            for hh in range(_GROUP):
                o_ref[0, g, :, hh * _HEAD_DIM:(hh + 1) * _HEAD_DIM] = (
                    ot[:, hh * tq:(hh + 1) * tq].T.astype(o_ref.dtype))
            return carry

        lax.fori_loop(0, n_kv, kv_group, 0)

    nq = s // tq
    return pl.pallas_call(
        body,
        grid=(b, nq),
        in_specs=[pl.BlockSpec((1, n_heads, tq, _HEAD_DIM), lambda bi, i: (bi, 0, i, 0)),
                  pl.BlockSpec((1, idx_heads, tq, _HEAD_DIM), lambda bi, i: (bi, 0, i, 0)),
                  pl.BlockSpec((1, idx_heads, tq), lambda bi, i: (bi, 0, i)),
                  pl.BlockSpec((1, s, _HEAD_DIM), lambda bi, i: (bi, 0, 0)),
                  pl.BlockSpec((1, n_kv, s, _HEAD_DIM), lambda bi, i: (bi, 0, 0, 0)),
                  pl.BlockSpec((1, n_kv, s // tk, _VT_ROWS, tk), lambda bi, i: (bi, 0, 0, 0, 0))],
        out_specs=pl.BlockSpec((1, n_kv, tq, _GROUP * _HEAD_DIM), lambda bi, i: (bi, 0, i, 0)),
        out_shape=jax.ShapeDtypeStruct((b, n_kv, s, _GROUP * _HEAD_DIM), _CDT),
        scratch_shapes=[pltpu.VMEM((s, tq), jnp.int32),
                        pltpu.VMEM((s, tq), _CDT),
                        pltpu.VMEM((2, tk, gq), jnp.float32),
                        pltpu.VMEM((2, tk, gq), _CDT),
                        pltpu.VMEM((2, 1, gq), jnp.float32),
                        pltpu.VMEM((1, gq), jnp.float32),
                        pltpu.VMEM((_VT_ROWS, gq), jnp.float32)],
        compiler_params=_params("arbitrary", "arbitrary"),
        name="dsa_core",
    )(q, qi, wt, kidx, k, vt)


def _cumsum_aug(lf, n_heads):
    b, s, w = lf.shape
    blk = _LANES

    def body(lf_ref, out_ref, carry_ref):
        @pl.when(pl.program_id(1) == 0)
        def _():
            carry_ref[...] = jnp.zeros_like(carry_ref)
        r = lax.broadcasted_iota(jnp.int32, (blk, blk), 0)
        c = lax.broadcasted_iota(jnp.int32, (blk, blk), 1)
        tri = jnp.where(r >= c, 1.0, 0.0).astype(jnp.float32)
        cs = jnp.dot(tri, lf_ref[0], preferred_element_type=jnp.float32,
                     precision=lax.Precision.HIGHEST) + carry_ref[...]
        carry_ref[...] = cs[blk - 1:blk, :]
        c2 = cs * _LOG2E
        for h in range(n_heads):
            col = c2[:, h:h + 1]
            hi = col.astype(_CDT).astype(jnp.float32)
            mid = (col - hi).astype(_CDT).astype(jnp.float32)
            lo = (col - hi - mid).astype(_CDT).astype(jnp.float32)
            tile = jnp.where(c == 0, hi, jnp.where(c == 1, mid, jnp.where(c == 2, lo, 0.0)))
            out_ref[0, h] = tile.astype(out_ref.dtype)

    return pl.pallas_call(
        body,
        grid=(b, s // blk),
        in_specs=[pl.BlockSpec((1, blk, w), lambda bi, i: (bi, i, 0))],
        out_specs=pl.BlockSpec((1, n_heads, blk, _LANES), lambda bi, i: (bi, 0, i, 0)),
        out_shape=jax.ShapeDtypeStruct((b, n_heads, s, _LANES), _CDT),
        scratch_shapes=[pltpu.VMEM((1, w), jnp.float32)],
        compiler_params=_params("arbitrary", "arbitrary"),
        name="fox_cumsum",
    )(lf)


def _fox_core(q, k, caug, vt, *, t):
    b, n_heads, s, _ = q.shape
    assert t % _MXU_DIM == 0 and s % t == 0

    chunks = [slice(cc * _MXU_DIM, (cc + 1) * _MXU_DIM) for cc in range(t // _MXU_DIM)]

    tiles = [(i, j) for i in range(s // t) for j in range(i + 1)]

    def visible(i, j, cols):
        return min(t, cols.stop) if i == j else t

    def body(q_ref, k_ref, c_ref, vt_ref, o_ref, kaug_ref, sc_ref, p_ref, al_ref, m_ref, acc_ref):
        kaug_ref[:, :_HEAD_DIM] = k_ref[0, 0]
        kaug_ref[:, _HEAD_DIM:] = c_ref[0, 0]
        lane = lax.broadcasted_iota(jnp.int32, (t, _HEAD_DIM), 1)
        minus_ones = jnp.where(lane < 3, -1.0, 0.0).astype(_CDT)

        def logits(n):
            i, j = tiles[n]
            qaug = jnp.concatenate([q_ref[0, 0, i * t:(i + 1) * t, :], minus_ones], axis=1)
            ka = kaug_ref[j * t:(j + 1) * t, :]
            for cols in chunks:
                kr = visible(i, j, cols)
                sc_ref[n & 1, :kr, cols] = lax.dot_general(ka[:kr], qaug[cols], _NT,
                                                           preferred_element_type=jnp.float32)

        def softmax(n):
            i, j = tiles[n]
            if j == 0:
                m_ref[...] = jnp.full((1, t), _M_INIT, jnp.float32)
            for cols in chunks:
                kr = visible(i, j, cols)

                def load_sc(cols=cols, kr=kr):
                    sc = sc_ref[n & 1, :kr, cols]
                    if i == j:
                        key_i = lax.broadcasted_iota(jnp.int32, (kr, _MXU_DIM), 0)
                        qry_i = cols.start + lax.broadcasted_iota(jnp.int32, (kr, _MXU_DIM), 1)
                        sc = jnp.where(key_i <= qry_i, sc, -jnp.inf)
                    return sc

                p_ref[n & 1, :kr, cols], al_ref[n & 1, :, cols] = _softmax_step(load_sc, m_ref, cols)

        def pv(n):
            i, j = tiles[n]
            vta = vt_ref[0, 0, j]
            for cols in chunks:
                kr = visible(i, j, cols)
                _pv_step(vta[:, :kr], p_ref[n & 1, :kr, cols],
                         None if j == 0 else al_ref[n & 1, :, cols], acc_ref, cols)
            if j == i:
                acc = acc_ref[...]
                ot = acc[:_HEAD_DIM] * (1.0 / acc[_HEAD_DIM:_HEAD_DIM + 1])
                for cq in range(t // _LANES):
                    o_ref[0, i * t + cq * _LANES:i * t + (cq + 1) * _LANES, :] = (
                        ot[:, cq * _LANES:(cq + 1) * _LANES].T.astype(o_ref.dtype))

        for step in range(len(tiles) + 2):
            if step < len(tiles):
                logits(step)
            if 1 <= step <= len(tiles):
                softmax(step - 1)
            if step >= 2:
                pv(step - 2)

    head = pl.BlockSpec((1, 1, s, _HEAD_DIM), lambda bi, h: (bi, h, 0, 0))
    return pl.pallas_call(
        body,
        grid=(b, n_heads),
        in_specs=[head, head, head,
                  pl.BlockSpec((1, 1, s // t, _VT_ROWS, t), lambda bi, h: (bi, h, 0, 0, 0))],
        out_specs=pl.BlockSpec((1, s, _HEAD_DIM), lambda bi, h: (bi, 0, h)),
        out_shape=jax.ShapeDtypeStruct((b, s, n_heads * _HEAD_DIM), _CDT),
        scratch_shapes=[pltpu.VMEM((s, 2 * _HEAD_DIM), _CDT),
                        pltpu.VMEM((2, t, t), jnp.float32),
                        pltpu.VMEM((2, t, t), _CDT),
                        pltpu.VMEM((2, 1, t), jnp.float32),
                        pltpu.VMEM((1, t), jnp.float32),
                        pltpu.VMEM((_VT_ROWS, t), jnp.float32)],
        compiler_params=_params("arbitrary", "arbitrary"),
        name="fox_core",
    )(q, k, caug, vt)


def _rope_tables(s):
    half = _HEAD_DIM // 2
    inv = _ROPE_THETA ** (-jnp.arange(half, dtype=jnp.float32) / half)
    ang = jnp.arange(s, dtype=jnp.float32)[:, None] * inv[None, :]
    cos, sin = jnp.cos(ang), jnp.sin(ang)
    return jnp.concatenate([cos, cos], axis=1), jnp.concatenate([-sin, sin], axis=1)


def _placed(cols, s, tn_cap=None):
    tm, tn = _proj_tiles(s, cols)
    if tn_cap is not None:
        tn = min(tn, tn_cap)
    if not cols.aligned(tn):
        cols = cols.materialize()
        tm, tn = _proj_tiles(s, cols)
        if tn_cap is not None:
            tn = min(tn, tn_cap)
    return cols, tm, tn


def _heads_call(a, cols, b, s, *, rope, scale, cos, sin, name):
    cols, tm, tn = _placed(cols, s)
    nsb = s // tm
    extras, extra_specs = (), ()
    if rope:
        extras = (cos, sin)
        extra_specs = (pl.BlockSpec((tm, _HEAD_DIM), lambda j, i: (i % nsb, 0)),) * 2
    nh_t = tn // _HEAD_DIM
    return _proj(
        a, cols, functools.partial(_ep_heads, rope=rope, scale=scale), extras, extra_specs,
        jax.ShapeDtypeStruct((b, cols.n // _HEAD_DIM, s, _HEAD_DIM), _CDT),
        pl.BlockSpec((1, nh_t, tm, _HEAD_DIM), lambda j, i: (i // nsb, j, i % nsb, 0)),
        tm, tn, name, row_chunk=min(_ROW_CHUNK, tm))


def _vt_call(a, cols, b, s, t, name):
    cols, tm, tn = _placed(cols, s)
    nsb = s // tm
    nh_t = tn // _HEAD_DIM
    return _proj(
        a, cols, functools.partial(_ep_vt, t=t), (), (),
        jax.ShapeDtypeStruct((b, cols.n // _HEAD_DIM, s // t, _VT_ROWS, t), _CDT),
        pl.BlockSpec((1, nh_t, tm // t, _VT_ROWS, t), lambda j, i: (i // nsb, j, i % nsb, 0, 0)),
        tm, tn, name, row_chunk=t)


def _gate_call(a, cols, b, s, groups, name):
    c = cols.n // groups
    cols, tm, tn = _placed(cols, s, tn_cap=c)
    nsb = s // tm
    per = c // tn
    return _proj(
        a, cols, _ep_silu, (), (),
        jax.ShapeDtypeStruct((b, groups, s, c), _CDT),
        pl.BlockSpec((1, 1, tm, tn), lambda j, i: (i // nsb, j // per, i % nsb, j % per)),
        tm, tn, name, row_chunk=min(_ROW_CHUNK, tm))


def _dsa_layer(xc, w_in_t, q_norm_g, w_uq, kidx_g, kidx_b, w_out, layer, b, s):
    d = xc.shape[1]
    q_lora = w_uq.shape[1]
    branch = w_out.shape[1]
    idx_heads = (w_uq.shape[2] - branch) // _HEAD_DIM
    kv = (w_in_t.shape[1] - q_lora - _HEAD_DIM - idx_heads - branch) // 2
    w_in = functools.partial(_Cols, w_in_t, layer=layer, transposed=True)
    n_kv = kv // _HEAD_DIM
    n_heads = branch // _HEAD_DIM
    assert n_heads == n_kv * _GROUP and idx_heads <= _LANES
    topk = min(_TOPK_MAX, s // 4)
    o0 = q_lora
    o1 = o0 + 2 * kv
    o2 = o1 + _HEAD_DIM
    o3 = o2 + idx_heads
    cos, sin = _rope_tables(s)
    tk1 = min(256, s)
    tk = min(512, s)

    c_cq = w_in(0, q_lora).materialize()
    tm = min(512, s)
    cq = _proj(xc, c_cq, _ep_rms, (q_norm_g[layer].reshape(1, q_lora),),
               (pl.BlockSpec((1, q_lora), lambda j, i: (0, 0)),),
               jax.ShapeDtypeStruct((b * s, q_lora), _CDT),
               pl.BlockSpec((tm, q_lora), lambda j, i: (i, 0)), tm, q_lora, "dsa_cq")
    k = _heads_call(xc, w_in(o0, kv), b, s, rope=True, scale=1.0, cos=cos, sin=sin, name="dsa_k")
    vt = _vt_call(xc, w_in(o0 + kv, kv), b, s, tk, "dsa_v")

    c_ki = w_in(o1, 2 * _LANES)
    if not c_ki.aligned(2 * _LANES):
        w_ki = jnp.pad(w_in_t[layer][o1:o3, :], ((0, _LANES - idx_heads), (0, 0)))
        c_ki = _Cols(w_ki.astype(_CDT), transposed=True)
    nsb = s // tm
    rope_specs = (pl.BlockSpec((tm, _HEAD_DIM), lambda j, i: (i % nsb, 0)),) * 2
    w_scale = idx_heads ** -0.5 * _HEAD_DIM ** -0.5
    vec = pl.BlockSpec((1, _HEAD_DIM), lambda j, i: (0, 0))
    kidx, wt = _proj(xc, c_ki, functools.partial(_ep_kidx, idx_heads=idx_heads, w_scale=w_scale),
                     (cos, sin, kidx_g[layer].reshape(1, _HEAD_DIM), kidx_b[layer].reshape(1, _HEAD_DIM)),
                     rope_specs + (vec, vec),
                     [jax.ShapeDtypeStruct((b, s, _HEAD_DIM), _CDT),
                      jax.ShapeDtypeStruct((b, idx_heads, s), jnp.float32)],
                     [pl.BlockSpec((1, tm, _HEAD_DIM), lambda j, i: (i // nsb, i % nsb, 0)),
                      pl.BlockSpec((1, idx_heads, tm), lambda j, i: (i // nsb, 0, i % nsb))],
                     tm, 2 * _LANES, "dsa_kidx")
    gate = _gate_call(xc, w_in(o3, branch), b, s, n_kv, "dsa_gate")
    q = _heads_call(cq, _Cols(w_uq, 0, branch, layer), b, s, rope=True,
                    scale=_HEAD_DIM ** -0.5 * _LOG2E, cos=cos, sin=sin, name="dsa_q")
    qi = _heads_call(cq, _Cols(w_uq, branch, None, layer), b, s, rope=True, scale=1.0, cos=cos, sin=sin,
                     name="dsa_qi")
    o = _dsa_core(q, qi, wt, kidx, k, vt, topk=topk, tk1=tk1, tk=tk, hchunk=min(8, idx_heads))
    return _outproj(o, gate, w_out[layer].astype(_CDT), min(512, s), min(1024, d), "dsa_out")


def _fox_layer(xc, w_in_t, forget_bias, w_out, layer, b, s):
    d = xc.shape[1]
    branch = w_out.shape[1]
    n_heads = forget_bias.shape[1]
    assert branch == n_heads * _HEAD_DIM and n_heads <= _LANES
    w_in = functools.partial(_Cols, w_in_t, layer=layer, transposed=True)
    w_f = jnp.pad(w_in_t[layer][4 * branch:, :], ((0, _LANES - n_heads), (0, 0)))
    fb = jnp.pad(forget_bias[layer], (0, _LANES - n_heads)).reshape(1, _LANES)
    tm = min(512, s)
    nsb = s // tm
    t = min(512, s)
    q = _heads_call(xc, w_in(0, branch), b, s, rope=False,
                    scale=_HEAD_DIM ** -0.5 * _LOG2E, cos=None, sin=None, name="fox_q")
    k = _heads_call(xc, w_in(branch, branch), b, s, rope=False, scale=1.0,
                    cos=None, sin=None, name="fox_k")
    vt = _vt_call(xc, w_in(2 * branch, branch), b, s, t, "fox_v")
    gate = _gate_call(xc, w_in(3 * branch, branch), b, s, 1, "fox_gate")
    lf = _proj(xc, _Cols(w_f, transposed=True), _ep_logf, (fb,), (pl.BlockSpec((1, _LANES), lambda j, i: (0, 0)),),
               jax.ShapeDtypeStruct((b, s, _LANES), jnp.float32),
               pl.BlockSpec((1, tm, _LANES), lambda j, i: (i // nsb, i % nsb, 0)),
               tm, _LANES, "fox_logf")
    caug = _cumsum_aug(lf, n_heads)
    o = _fox_core(q, k, caug, vt, t=t)
    return _outproj(o.reshape(b, 1, s, branch), gate, w_out[layer].astype(_CDT), tm, min(1024, d),
                    "fox_out")


def kernel(x, a_w_in, a_q_norm_g, a_w_uq, a_kidx_norm_g, a_kidx_norm_b, a_w_out,
           b_w_in, b_forget_bias, b_w_out, ln_g, ln_b):
    b, s, d = x.shape
    depth = ln_g.shape[0]
    alpha = (2 * depth) ** 0.25
    xf = x.reshape(b * s, d)
    xc = xf.astype(_CDT)
    a_w_in_t = jnp.swapaxes(a_w_in, 1, 2)
    b_w_in_t = jnp.swapaxes(b_w_in, 1, 2)
    tm_ln = min(256, s)
    for i in range(depth):
        j = i // 2
        if i % 2 == 0:
            h = _dsa_layer(xc, a_w_in_t, a_q_norm_g, a_w_uq, a_kidx_norm_g, a_kidx_norm_b, a_w_out, j, b, s)
        else:
            h = _fox_layer(xc, b_w_in_t, b_forget_bias, b_w_out, j, b, s)
        last = i == depth - 1
        outs = _resid_ln(h, xf, ln_g[i], ln_b[i], alpha, tm_ln, not last, "resid_ln_%d" % i)
        xf = outs[0]
        if not last:
            xc = outs[1]
    return xf.reshape(b, s, d)
```

```python
import functools
import math

import jax
import jax.numpy as jnp
from jax import lax
from jax.experimental import pallas as pl
from jax.experimental.pallas import tpu as pltpu

_CDT = jnp.bfloat16
_HEAD_DIM = 128
_GROUP = 8
_TOPK_MAX = 256
_ROPE_THETA = 10000.0
_LN_EPS = 1e-5
_RMS_EPS = 1e-6
_LANES = 128
_SUBLANES = 8
_MXU_DIM = 256
_VMEM_LIMIT = 56 * 1024 * 1024
_PROJ_TILE_BUDGET = 40 * 1024 * 1024
_VT_ROWS = _HEAD_DIM + 16
_INT_MIN = -2 ** 31
_MASKED = -1e30
_M_INIT = -1e29
_LOG2E = math.log2(math.e)
_ROW_CHUNK = 256

_NT = (((1,), (1,)), ((), ()))


def _params(*sem):
    return pltpu.CompilerParams(dimension_semantics=sem, vmem_limit_bytes=_VMEM_LIMIT)


def _rope_tile(x, cos, sin_signed):
    return x * cos + pltpu.roll(x, _HEAD_DIM // 2, axis=1) * sin_signed


class _Cols:
    def __init__(self, w, col0=0, n=None, layer=None, transposed=False):
        self.w, self.col0, self.layer, self.transposed = w, col0, layer, transposed
        self.n_total = w.shape[-2] if transposed else w.shape[-1]
        self.k = w.shape[-1] if transposed else w.shape[-2]
        self.n = self.n_total - col0 if n is None else n

    def block_aligned(self, tn):
        return self.col0 % tn == 0

    def aligned(self, tn):
        start_ok = self.col0 % _SUBLANES == 0 if self.transposed else self.block_aligned(tn)
        return start_ok and self.col0 + -(-self.n // tn) * tn <= self.n_total

    def materialize(self):
        w = self.w if self.layer is None else self.w[self.layer]
        if self.transposed:
            return _Cols(w[self.col0:self.col0 + self.n, :].astype(_CDT), transposed=True)
        return _Cols(w[:, self.col0:self.col0 + self.n].astype(_CDT))


def _proj_tiles(s, cols):
    itemsize = cols.w.dtype.itemsize
    act = jnp.dtype(_CDT).itemsize
    for tn in (1024, 512, 256, 128):
        if tn > cols.n:
            continue
        w_bytes = cols.k * tn * (2 * itemsize + (act if cols.w.dtype != _CDT else 0))
        for tm in (1024, 512, 256):
            if tm <= s and w_bytes + 2 * tm * cols.k * act <= _PROJ_TILE_BUDGET:
                return tm, tn
    raise ValueError("projection tiles do not fit VMEM")


def _proj(a, cols, epilogue, extras, extra_specs, out_shape, out_specs, tm, tn, name, row_chunk=None):
    m, k = a.shape
    assert cols.aligned(tn) and k == cols.k
    convert = cols.w.dtype != _CDT
    n_extra = len(extras)
    w_block = (tn, k) if cols.transposed else (k, tn)

    def body(a_ref, w_ref, *rest):
        if convert:
            wc_ref = rest[-1]
            rest = rest[:-1]

            @pl.when(pl.program_id(1) == 0)
            def _():
                wc_ref[...] = w_ref[...].reshape(w_block).astype(_CDT)

            w = wc_ref[...]
        else:
            w = w_ref[...].reshape(w_block)
        rc = tm if row_chunk is None else row_chunk
        for r in range(tm // rc):
            rows = slice(r * rc, (r + 1) * rc)
            if cols.transposed:
                res = lax.dot_general(a_ref[rows, :], w, _NT, preferred_element_type=jnp.float32)
            else:
                res = jnp.dot(a_ref[rows, :], w, preferred_element_type=jnp.float32)
            epilogue(res, rest[:n_extra], rest[n_extra:], rows)

    lead_index = () if cols.layer is None else (cols.layer,)
    if cols.block_aligned(tn):
        c0 = cols.col0 // tn
        lead = () if cols.layer is None else (None,)
        if cols.transposed:
            w_spec = pl.BlockSpec(lead + (tn, k), lambda j, i: lead_index + (c0 + j, 0))
        else:
            w_spec = pl.BlockSpec(lead + (k, tn), lambda j, i: lead_index + (0, c0 + j))
    else:
        assert cols.transposed
        lead = () if cols.layer is None else (pl.Element(1),)
        w_spec = pl.BlockSpec(lead + (pl.Element(tn), pl.Element(k)),
                              lambda j, i: lead_index + (pl.multiple_of(cols.col0 + j * tn, _SUBLANES), 0))
    return pl.pallas_call(
        body,
        grid=(-(-cols.n // tn), m // tm),
        in_specs=[pl.BlockSpec((tm, k), lambda j, i: (i, 0)), w_spec] + list(extra_specs),
        out_specs=out_specs,
        out_shape=out_shape,
        scratch_shapes=[pltpu.VMEM(w_block, _CDT)] if convert else [],
        compiler_params=_params("arbitrary", "arbitrary"),
        name=name,
    )(a, cols.w, *extras)


def _ep_rms(res, extras, outs, rows):
    (g_ref,) = extras
    (o_ref,) = outs
    y = res * lax.rsqrt(jnp.mean(res * res, axis=-1, keepdims=True) + _RMS_EPS)
    o_ref[rows, :] = (y * g_ref[...]).astype(o_ref.dtype)


def _ep_heads(res, extras, outs, rows, *, rope, scale):
    (o_ref,) = outs
    nh = res.shape[1] // _HEAD_DIM
    if rope:
        cos = extras[0][rows, :]
        sin = extras[1][rows, :]
    for h in range(nh):
        xh = res[:, h * _HEAD_DIM:(h + 1) * _HEAD_DIM]
        if rope:
            xh = _rope_tile(xh, cos, sin)
        if scale != 1.0:
            xh = xh * scale
        o_ref[0, h, rows, :] = xh.astype(o_ref.dtype)


def _ep_vt(res, extras, outs, rows, *, t):
    (vt_ref,) = outs
    assert rows.start % t == 0 and res.shape[0] % t == 0
    for h in range(res.shape[1] // _HEAD_DIM):
        vh = res[:, h * _HEAD_DIM:(h + 1) * _HEAD_DIM]
        for c in range(res.shape[0] // t):
            dst = rows.start // t + c
            vt_ref[0, h, dst, :_HEAD_DIM, :] = vh[c * t:(c + 1) * t, :].T.astype(vt_ref.dtype)
            vt_ref[0, h, dst, _HEAD_DIM:, :] = jnp.ones((_VT_ROWS - _HEAD_DIM, t), vt_ref.dtype)


def _ep_kidx(res, extras, outs, rows, *, idx_heads, w_scale):
    cos_ref, sin_ref, g_ref, b_ref = extras
    kidx_ref, wt_ref = outs
    x = res[:, :_HEAD_DIM]
    mu = jnp.mean(x, axis=-1, keepdims=True)
    xc = x - mu
    var = jnp.mean(xc * xc, axis=-1, keepdims=True)
    y = xc * lax.rsqrt(var + _LN_EPS) * g_ref[...] + b_ref[...]
    kidx_ref[0] = _rope_tile(y, cos_ref[...], sin_ref[...]).astype(kidx_ref.dtype)
    wt = (res[:, _HEAD_DIM:] * w_scale).T
    wt_ref[0] = wt[:idx_heads, :]


def _ep_silu(res, extras, outs, rows):
    (o_ref,) = outs
    o_ref[0, 0, rows, :] = (res * jax.nn.sigmoid(res)).astype(o_ref.dtype)


def _ep_logf(res, extras, outs, rows):
    (bias_ref,) = extras
    (o_ref,) = outs
    z = res + bias_ref[...]
    o_ref[0] = jnp.minimum(z, 0.0) - jnp.log(1.0 + jnp.exp(-jnp.abs(z)))


def _outproj(o, gate, w, tm, tn, name):
    b, g, s, c = o.shape
    d = w.shape[1]
    nsb = s // tm

    def body(o_ref, g_ref, w_ref, out_ref):
        acc = None
        for gi in range(g):
            a = (o_ref[0, gi].astype(jnp.float32) * g_ref[0, gi].astype(jnp.float32)).astype(_CDT)
            part = jnp.dot(a, w_ref[gi * c:(gi + 1) * c, :], preferred_element_type=jnp.float32)
            acc = part if acc is None else acc + part
        out_ref[...] = acc

    return pl.pallas_call(
        body,
        grid=(d // tn, b * nsb),
        in_specs=[pl.BlockSpec((1, g, tm, c), lambda j, i: (i // nsb, 0, i % nsb, 0)),
                  pl.BlockSpec((1, g, tm, c), lambda j, i: (i // nsb, 0, i % nsb, 0)),
                  pl.BlockSpec((g * c, tn), lambda j, i: (0, j))],
        out_specs=pl.BlockSpec((tm, tn), lambda j, i: (i, j)),
        out_shape=jax.ShapeDtypeStruct((b * s, d), jnp.float32),
        compiler_params=_params("arbitrary", "arbitrary"),
        name=name,
    )(o, gate, w)


def _resid_ln(h, x, g, b, alpha, tm, with_cdt, name):
    m, d = x.shape

    def body(h_ref, x_ref, g_ref, b_ref, *outs):
        y = alpha * x_ref[...] + h_ref[...]
        mu = jnp.mean(y, axis=-1, keepdims=True)
        yc = y - mu
        var = jnp.mean(yc * yc, axis=-1, keepdims=True)
        z = yc * lax.rsqrt(var + _LN_EPS) * g_ref[...] + b_ref[...]
        outs[0][...] = z
        if with_cdt:
            outs[1][...] = z.astype(_CDT)

    row = pl.BlockSpec((tm, d), lambda i: (i, 0))
    vec = pl.BlockSpec((1, d), lambda i: (0, 0))
    out_shape = [jax.ShapeDtypeStruct((m, d), jnp.float32)]
    out_specs = [row]
    if with_cdt:
        out_shape.append(jax.ShapeDtypeStruct((m, d), _CDT))
        out_specs.append(row)
    return pl.pallas_call(
        body,
        grid=(m // tm,),
        in_specs=[row, row, vec, vec],
        out_specs=out_specs,
        out_shape=out_shape,
        compiler_params=_params("arbitrary"),
        name=name,
    )(h, x, g.reshape(1, d), b.reshape(1, d))


def _col_reduce(x, op):
    rows, n = x.shape
    fold = 8 * _SUBLANES
    if rows > fold and rows % fold == 0:
        x = op(x.reshape(rows // fold, fold, n), axis=0)
    return op(x, axis=0, keepdims=True)


def _softmax_step(load_sc, m_ref, cols):
    m_old = m_ref[:, cols]
    m_new = jnp.maximum(m_old, _col_reduce(load_sc(), jnp.max))
    m_ref[:, cols] = m_new
    arg = load_sc() - m_new
    p = jnp.where(arg < -126.0, 0.0, jnp.exp2(jnp.maximum(arg, -126.0)))
    return p.astype(_CDT), jnp.exp2(m_old - m_new)


def _pv_step(vta, p, alpha, acc_ref, cols):
    pv = jnp.dot(vta, p, preferred_element_type=jnp.float32)
    acc_ref[:, cols] = pv if alpha is None else alpha * acc_ref[:, cols] + pv


def _float_sort_key(x):
    bits = pltpu.bitcast(x, jnp.int32)
    return bits ^ ((bits >> 31) & jnp.int32(0x7FFFFFFF))


def _dsa_core(q, qi, wt, kidx, k, vt, *, topk, tk1, tk, hchunk):
    b, n_heads, s, _ = q.shape
    idx_heads = qi.shape[1]
    n_kv = k.shape[1]
    tq = _LANES
    gq = _GROUP * tq
    log_s = int(math.log2(s))
    assert 1 << log_s == s and s % tk == 0 and tk % tk1 == 0 and tk1 % tq == 0
    assert idx_heads % hchunk == 0 and gq % _MXU_DIM == 0

    def body(q_ref, qi_ref, wt_ref, kidx_ref, k_ref, vt_ref, o_ref,
             key_ref, mask_ref, sc_ref, p_ref, al_ref, m_ref, acc_ref):
        i = pl.program_id(1)
        n1 = (i * tq + tq + tk1 - 1) // tk1
        n3 = (i * tq + tq + tk - 1) // tk

        q_pos1 = i * tq + lax.broadcasted_iota(jnp.int32, (tk1, tq), 1)
        s_iota1 = lax.broadcasted_iota(jnp.int32, (tk1, tq), 0)

        def score_tile(kt, carry):
            ks = pl.multiple_of(kt * tk1, tk1)
            kx = kidx_ref[0, pl.ds(ks, tk1), :]
            acc = jnp.zeros((tk1, tq), jnp.float32)
            for hc in range(idx_heads // hchunk):
                qq = qi_ref[0, hc * hchunk:(hc + 1) * hchunk].reshape(hchunk * tq, _HEAD_DIM)
                d = lax.dot_general(kx, qq, _NT, preferred_element_type=jnp.float32)
                for hh in range(hchunk):
                    h = hc * hchunk + hh
                    acc = acc + jnp.maximum(d[:, hh * tq:(hh + 1) * tq], 0.0) * wt_ref[0, h:h + 1, :]
            key = _float_sort_key(acc + 0.0)
            key_ref[pl.ds(ks, tk1), :] = jnp.where(ks + s_iota1 <= q_pos1, key, _INT_MIN)
            return carry

        lax.fori_loop(0, n1, score_tile, 0)

        def pad_tile(kt, carry):
            key_ref[pl.ds(pl.multiple_of(kt * tk1, tk1), tk1), :] = jnp.full((tk1, tq), _INT_MIN, jnp.int32)
            return carry

        lax.fori_loop(n1, n3 * (tk // tk1), pad_tile, 0)

        q_pos = i * tq + lax.broadcasted_iota(jnp.int32, (tk, tq), 1)
        s_iota = lax.broadcasted_iota(jnp.int32, (tk, tq), 0)

        fold = 8 * _SUBLANES

        def count(pred_fn):
            def tile(kt, cnt):
                ks = pl.multiple_of(kt * tk, tk)
                c = jnp.where(pred_fn(key_ref[pl.ds(ks, tk), :], ks), 1.0, 0.0)
                return cnt + jnp.sum(c.reshape(tk // fold, fold, tq), axis=0)
            cnt = lax.fori_loop(0, n3, tile, jnp.zeros((fold, tq), jnp.float32))
            return jnp.sum(cnt, axis=0, keepdims=True)

        def value_step(t, prefix):
            cand = prefix | jnp.left_shift(jnp.int32(1), 31 - t)
            cand_s = cand ^ _INT_MIN
            cnt = count(lambda key, ks: key >= cand_s)
            return jnp.where(cnt >= topk, cand, prefix)

        prefix = lax.fori_loop(0, 32, value_step, jnp.zeros((1, tq), jnp.int32))
        thr = prefix ^ _INT_MIN

        def tie_break():
            need = topk - count(lambda key, ks: key > thr)

            def index_step(t, j):
                cand = j | jnp.left_shift(jnp.int32(1), log_s - 1 - t)
                cnt = count(lambda key, ks: (key == thr) & (ks + s_iota < cand))
                return jnp.where(cnt < need, cand, j)

            return lax.fori_loop(0, log_s, index_step, jnp.zeros((1, tq), jnp.int32))

        n_ge = count(lambda key, ks: key >= thr)
        j_last = lax.cond(jnp.max(n_ge) > topk, tie_break,
                          lambda: jnp.full((1, tq), s, jnp.int32))

        def mask_tile(kt, carry):
            ks = pl.multiple_of(kt * tk, tk)
            key = key_ref[pl.ds(ks, tk), :]
            s_pos = ks + s_iota
            sel = (key > thr) | ((key == thr) & (s_pos <= j_last))
            mask_ref[pl.ds(ks, tk), :] = jnp.where(sel & (s_pos <= q_pos), 0.0, _MASKED).astype(_CDT)
            return carry

        lax.fori_loop(0, n3, mask_tile, 0)

        r = lax.broadcasted_iota(jnp.int32, (gq, tq), 0)
        c = lax.broadcasted_iota(jnp.int32, (gq, tq), 1)
        one_hot = jnp.where((r & (tq - 1)) == c, 1.0, 0.0).astype(_CDT)

        chunks = [slice(cc * _MXU_DIM, (cc + 1) * _MXU_DIM) for cc in range(gq // _MXU_DIM)]

        def kv_group(g, carry):
            qg = q_ref[0, pl.ds(g * _GROUP, _GROUP)].reshape(gq, _HEAD_DIM)
            qaug = jnp.concatenate([qg, one_hot], axis=1)
            m_ref[...] = jnp.full((1, gq), _M_INIT, jnp.float32)

            def logits(kt):
                kaug = jnp.concatenate([k_ref[0, g, pl.ds(kt * tk, tk), :],
                                        mask_ref[pl.ds(kt * tk, tk), :]], axis=1)
                for cols in chunks:
                    sc_ref[kt & 1, :, cols] = lax.dot_general(kaug, qaug[cols], _NT,
                                                              preferred_element_type=jnp.float32)

            def softmax(kt):
                for cols in chunks:
                    p_ref[kt & 1, :, cols], al_ref[kt & 1, :, cols] = _softmax_step(
                        lambda: sc_ref[kt & 1, :, cols], m_ref, cols)

            def pv(kt):
                vta = vt_ref[0, g, kt]
                for cols in chunks:
                    _pv_step(vta, p_ref[kt & 1, :, cols], None if kt == 0 else al_ref[kt & 1, :, cols],
                             acc_ref, cols)

            for n_tiles in range(1, s // tk + 1):
                @pl.when(n3 == n_tiles)
                def _():
                    for step in range(n_tiles + 2):
                        if step < n_tiles:
                            logits(step)
                        if 1 <= step <= n_tiles:
                            softmax(step - 1)
                        if step >= 2:
                            pv(step - 2)

            acc = acc_ref[...]
            ot = acc[:_HEAD_DIM] * (1.0 / acc[_HEAD_DIM:_HEAD_DIM + 1])
            for hh in range(_GROUP):
                o_ref[0, g, :, hh * _HEAD_DIM:(hh + 1) * _HEAD_DIM] = (
                    ot[:, hh * tq:(hh + 1) * tq].T.astype(o_ref.dtype))
            return carry

        lax.fori_loop(0, n_kv, kv_group, 0)

    nq = s // tq
    return pl.pallas_call(
        body,
        grid=(b, nq),
        in_specs=[pl.BlockSpec((1, n_heads, tq, _HEAD_DIM), lambda bi, i: (bi, 0, i, 0)),
                  pl.BlockSpec((1, idx_heads, tq, _HEAD_DIM), lambda bi, i: (bi, 0, i, 0)),
                  pl.BlockSpec((1, idx_heads, tq), lambda bi, i: (bi, 0, i)),
                  pl.BlockSpec((1, s, _HEAD_DIM), lambda bi, i: (bi, 0, 0)),
                  pl.BlockSpec((1, n_kv, s, _HEAD_DIM), lambda bi, i: (bi, 0, 0, 0)),
                  pl.BlockSpec((1, n_kv, s // tk, _VT_ROWS, tk), lambda bi, i: (bi, 0, 0, 0, 0))],
        out_specs=pl.BlockSpec((1, n_kv, tq, _GROUP * _HEAD_DIM), lambda bi, i: (bi, 0, i, 0)),
        out_shape=jax.ShapeDtypeStruct((b, n_kv, s, _GROUP * _HEAD_DIM), _CDT),
        scratch_shapes=[pltpu.VMEM((s, tq), jnp.int32),
                        pltpu.VMEM((s, tq), _CDT),
                        pltpu.VMEM((2, tk, gq), jnp.float32),
                        pltpu.VMEM((2, tk, gq), _CDT),
                        pltpu.VMEM((2, 1, gq), jnp.float32),
                        pltpu.VMEM((1, gq), jnp.float32),
                        pltpu.VMEM((_VT_ROWS, gq), jnp.float32)],
        compiler_params=_params("arbitrary", "arbitrary"),
        name="dsa_core",
    )(q, qi, wt, kidx, k, vt)


def _cumsum_aug(lf, n_heads):
    b, s, w = lf.shape
    blk = _LANES

    def body(lf_ref, out_ref, carry_ref):
        @pl.when(pl.program_id(1) == 0)
        def _():
            carry_ref[...] = jnp.zeros_like(carry_ref)
        r = lax.broadcasted_iota(jnp.int32, (blk, blk), 0)
        c = lax.broadcasted_iota(jnp.int32, (blk, blk), 1)
        tri = jnp.where(r >= c, 1.0, 0.0).astype(jnp.float32)
        cs = jnp.dot(tri, lf_ref[0], preferred_element_type=jnp.float32,
                     precision=lax.Precision.HIGHEST) + carry_ref[...]
        carry_ref[...] = cs[blk - 1:blk, :]
        c2 = cs * _LOG2E
        for h in range(n_heads):
            col = c2[:, h:h + 1]
            hi = col.astype(_CDT).astype(jnp.float32)
            mid = (col - hi).astype(_CDT).astype(jnp.float32)
            lo = (col - hi - mid).astype(_CDT).astype(jnp.float32)
            tile = jnp.where(c == 0, hi, jnp.where(c == 1, mid, jnp.where(c == 2, lo, 0.0)))
            out_ref[0, h] = tile.astype(out_ref.dtype)

    return pl.pallas_call(
        body,
        grid=(b, s // blk),
        in_specs=[pl.BlockSpec((1, blk, w), lambda bi, i: (bi, i, 0))],
        out_specs=pl.BlockSpec((1, n_heads, blk, _LANES), lambda bi, i: (bi, 0, i, 0)),
        out_shape=jax.ShapeDtypeStruct((b, n_heads, s, _LANES), _CDT),
        scratch_shapes=[pltpu.VMEM((1, w), jnp.float32)],
        compiler_params=_params("arbitrary", "arbitrary"),
        name="fox_cumsum",
    )(lf)


def _fox_core(q, k, caug, vt, *, t):
    b, n_heads, s, _ = q.shape
    assert t % _MXU_DIM == 0 and s % t == 0

    chunks = [slice(cc * _MXU_DIM, (cc + 1) * _MXU_DIM) for cc in range(t // _MXU_DIM)]

    tiles = [(i, j) for i in range(s // t) for j in range(i + 1)]

    def visible(i, j, cols):
        return min(t, cols.stop) if i == j else t

    def body(q_ref, k_ref, c_ref, vt_ref, o_ref, kaug_ref, sc_ref, p_ref, al_ref, m_ref, acc_ref):
        kaug_ref[:, :_HEAD_DIM] = k_ref[0, 0]
        kaug_ref[:, _HEAD_DIM:] = c_ref[0, 0]
        lane = lax.broadcasted_iota(jnp.int32, (t, _HEAD_DIM), 1)
        minus_ones = jnp.where(lane < 3, -1.0, 0.0).astype(_CDT)

        def logits(n):
            i, j = tiles[n]
            qaug = jnp.concatenate([q_ref[0, 0, i * t:(i + 1) * t, :], minus_ones], axis=1)
            ka = kaug_ref[j * t:(j + 1) * t, :]
            for cols in chunks:
                kr = visible(i, j, cols)
                sc_ref[n & 1, :kr, cols] = lax.dot_general(ka[:kr], qaug[cols], _NT,
                                                           preferred_element_type=jnp.float32)

        def softmax(n):
            i, j = tiles[n]
            if j == 0:
                m_ref[...] = jnp.full((1, t), _M_INIT, jnp.float32)
            for cols in chunks:
                kr = visible(i, j, cols)

                def load_sc(cols=cols, kr=kr):
                    sc = sc_ref[n & 1, :kr, cols]
                    if i == j:
                        key_i = lax.broadcasted_iota(jnp.int32, (kr, _MXU_DIM), 0)
                        qry_i = cols.start + lax.broadcasted_iota(jnp.int32, (kr, _MXU_DIM), 1)
                        sc = jnp.where(key_i <= qry_i, sc, -jnp.inf)
                    return sc

                p_ref[n & 1, :kr, cols], al_ref[n & 1, :, cols] = _softmax_step(load_sc, m_ref, cols)

        def pv(n):
            i, j = tiles[n]
            vta = vt_ref[0, 0, j]
            for cols in chunks:
                kr = visible(i, j, cols)
                _pv_step(vta[:, :kr], p_ref[n & 1, :kr, cols],
                         None if j == 0 else al_ref[n & 1, :, cols], acc_ref, cols)
            if j == i:
                acc = acc_ref[...]
                ot = acc[:_HEAD_DIM] * (1.0 / acc[_HEAD_DIM:_HEAD_DIM + 1])
                for cq in range(t // _LANES):
                    o_ref[0, i * t + cq * _LANES:i * t + (cq + 1) * _LANES, :] = (
                        ot[:, cq * _LANES:(cq + 1) * _LANES].T.astype(o_ref.dtype))

        for step in range(len(tiles) + 2):
            if step < len(tiles):
                logits(step)
            if 1 <= step <= len(tiles):
                softmax(step - 1)
            if step >= 2:
                pv(step - 2)

    head = pl.BlockSpec((1, 1, s, _HEAD_DIM), lambda bi, h: (bi, h, 0, 0))
    return pl.pallas_call(
        body,
        grid=(b, n_heads),
        in_specs=[head, head, head,
                  pl.BlockSpec((1, 1, s // t, _VT_ROWS, t), lambda bi, h: (bi, h, 0, 0, 0))],
        out_specs=pl.BlockSpec((1, s, _HEAD_DIM), lambda bi, h: (bi, 0, h)),
        out_shape=jax.ShapeDtypeStruct((b, s, n_heads * _HEAD_DIM), _CDT),
        scratch_shapes=[pltpu.VMEM((s, 2 * _HEAD_DIM), _CDT),
                        pltpu.VMEM((2, t, t), jnp.float32),
                        pltpu.VMEM((2, t, t), _CDT),
                        pltpu.VMEM((2, 1, t), jnp.float32),
                        pltpu.VMEM((1, t), jnp.float32),
                        pltpu.VMEM((_VT_ROWS, t), jnp.float32)],
        compiler_params=_params("arbitrary", "arbitrary"),
        name="fox_core",
    )(q, k, caug, vt)


def _rope_tables(s):
    half = _HEAD_DIM // 2
    inv = _ROPE_THETA ** (-jnp.arange(half, dtype=jnp.float32) / half)
    ang = jnp.arange(s, dtype=jnp.float32)[:, None] * inv[None, :]
    cos, sin = jnp.cos(ang), jnp.sin(ang)
    return jnp.concatenate([cos, cos], axis=1), jnp.concatenate([-sin, sin], axis=1)


def _placed(cols, s, tn_cap=None):
    tm, tn = _proj_tiles(s, cols)
    if tn_cap is not None:
        tn = min(tn, tn_cap)
    if not cols.aligned(tn):
        cols = cols.materialize()
        tm, tn = _proj_tiles(s, cols)
        if tn_cap is not None:
            tn = min(tn, tn_cap)
    return cols, tm, tn


def _heads_call(a, cols, b, s, *, rope, scale, cos, sin, name):
    cols, tm, tn = _placed(cols, s)
    nsb = s // tm
    extras, extra_specs = (), ()
    if rope:
        extras = (cos, sin)
        extra_specs = (pl.BlockSpec((tm, _HEAD_DIM), lambda j, i: (i % nsb, 0)),) * 2
    nh_t = tn // _HEAD_DIM
    return _proj(
        a, cols, functools.partial(_ep_heads, rope=rope, scale=scale), extras, extra_specs,
        jax.ShapeDtypeStruct((b, cols.n // _HEAD_DIM, s, _HEAD_DIM), _CDT),
        pl.BlockSpec((1, nh_t, tm, _HEAD_DIM), lambda j, i: (i // nsb, j, i % nsb, 0)),
        tm, tn, name, row_chunk=min(_ROW_CHUNK, tm))


def _vt_call(a, cols, b, s, t, name):
    cols, tm, tn = _placed(cols, s)
    nsb = s // tm
    nh_t = tn // _HEAD_DIM
    return _proj(
        a, cols, functools.partial(_ep_vt, t=t), (), (),
        jax.ShapeDtypeStruct((b, cols.n // _HEAD_DIM, s // t, _VT_ROWS, t), _CDT),
        pl.BlockSpec((1, nh_t, tm // t, _VT_ROWS, t), lambda j, i: (i // nsb, j, i % nsb, 0, 0)),
        tm, tn, name, row_chunk=t)


def _gate_call(a, cols, b, s, groups, name):
    c = cols.n // groups
    cols, tm, tn = _placed(cols, s, tn_cap=c)
    nsb = s // tm
    per = c // tn
    return _proj(
        a, cols, _ep_silu, (), (),
        jax.ShapeDtypeStruct((b, groups, s, c), _CDT),
        pl.BlockSpec((1, 1, tm, tn), lambda j, i: (i // nsb, j // per, i % nsb, j % per)),
        tm, tn, name, row_chunk=min(_ROW_CHUNK, tm))


def _dsa_layer(xc, w_in_t, q_norm_g, w_uq, kidx_g, kidx_b, w_out, layer, b, s):
    d = xc.shape[1]
    q_lora = w_uq.shape[1]
    branch = w_out.shape[1]
    idx_heads = (w_uq.shape[2] - branch) // _HEAD_DIM
    kv = (w_in_t.shape[1] - q_lora - _HEAD_DIM - idx_heads - branch) // 2
    w_in = functools.partial(_Cols, w_in_t, layer=layer, transposed=True)
    n_kv = kv // _HEAD_DIM
    n_heads = branch // _HEAD_DIM
    assert n_heads == n_kv * _GROUP and idx_heads <= _LANES
    topk = min(_TOPK_MAX, s // 4)
    o0 = q_lora
    o1 = o0 + 2 * kv
    o2 = o1 + _HEAD_DIM
    o3 = o2 + idx_heads
    cos, sin = _rope_tables(s)
    tk1 = min(256, s)
    tk = min(512, s)

    c_cq = w_in(0, q_lora).materialize()
    tm = min(512, s)
    cq = _proj(xc, c_cq, _ep_rms, (q_norm_g[layer].reshape(1, q_lora),),
               (pl.BlockSpec((1, q_lora), lambda j, i: (0, 0)),),
               jax.ShapeDtypeStruct((b * s, q_lora), _CDT),
               pl.BlockSpec((tm, q_lora), lambda j, i: (i, 0)), tm, q_lora, "dsa_cq")
    k = _heads_call(xc, w_in(o0, kv), b, s, rope=True, scale=1.0, cos=cos, sin=sin, name="dsa_k")
    vt = _vt_call(xc, w_in(o0 + kv, kv), b, s, tk, "dsa_v")

    c_ki = w_in(o1, 2 * _LANES)
    if not c_ki.aligned(2 * _LANES):
        w_ki = jnp.pad(w_in_t[layer][o1:o3, :], ((0, _LANES - idx_heads), (0, 0)))
        c_ki = _Cols(w_ki.astype(_CDT), transposed=True)
    nsb = s // tm
    rope_specs = (pl.BlockSpec((tm, _HEAD_DIM), lambda j, i: (i % nsb, 0)),) * 2
    w_scale = idx_heads ** -0.5 * _HEAD_DIM ** -0.5
    vec = pl.BlockSpec((1, _HEAD_DIM), lambda j, i: (0, 0))
    kidx, wt = _proj(xc, c_ki, functools.partial(_ep_kidx, idx_heads=idx_heads, w_scale=w_scale),
                     (cos, sin, kidx_g[layer].reshape(1, _HEAD_DIM), kidx_b[layer].reshape(1, _HEAD_DIM)),
                     rope_specs + (vec, vec),
                     [jax.ShapeDtypeStruct((b, s, _HEAD_DIM), _CDT),
                      jax.ShapeDtypeStruct((b, idx_heads, s), jnp.float32)],
                     [pl.BlockSpec((1, tm, _HEAD_DIM), lambda j, i: (i // nsb, i % nsb, 0)),
                      pl.BlockSpec((1, idx_heads, tm), lambda j, i: (i // nsb, 0, i % nsb))],
                     tm, 2 * _LANES, "dsa_kidx")
    gate = _gate_call(xc, w_in(o3, branch), b, s, n_kv, "dsa_gate")
    q = _heads_call(cq, _Cols(w_uq, 0, branch, layer), b, s, rope=True,
                    scale=_HEAD_DIM ** -0.5 * _LOG2E, cos=cos, sin=sin, name="dsa_q")
    qi = _heads_call(cq, _Cols(w_uq, branch, None, layer), b, s, rope=True, scale=1.0, cos=cos, sin=sin,
                     name="dsa_qi")
    o = _dsa_core(q, qi, wt, kidx, k, vt, topk=topk, tk1=tk1, tk=tk, hchunk=min(8, idx_heads))
    return _outproj(o, gate, w_out[layer].astype(_CDT), min(512, s), min(1024, d), "dsa_out")


def _fox_layer(xc, w_in_t, forget_bias, w_out, layer, b, s):
    d = xc.shape[1]
    branch = w_out.shape[1]
    n_heads = forget_bias.shape[1]
    assert branch == n_heads * _HEAD_DIM and n_heads <= _LANES
    w_in = functools.partial(_Cols, w_in_t, layer=layer, transposed=True)
    w_f = jnp.pad(w_in_t[layer][4 * branch:, :], ((0, _LANES - n_heads), (0, 0)))
    fb = jnp.pad(forget_bias[layer], (0, _LANES - n_heads)).reshape(1, _LANES)
    tm = min(512, s)
    nsb = s // tm
    t = min(512, s)
    q = _heads_call(xc, w_in(0, branch), b, s, rope=False,
                    scale=_HEAD_DIM ** -0.5 * _LOG2E, cos=None, sin=None, name="fox_q")
    k = _heads_call(xc, w_in(branch, branch), b, s, rope=False, scale=1.0,
                    cos=None, sin=None, name="fox_k")
    vt = _vt_call(xc, w_in(2 * branch, branch), b, s, t, "fox_v")
    gate = _gate_call(xc, w_in(3 * branch, branch), b, s, 1, "fox_gate")
    lf = _proj(xc, _Cols(w_f, transposed=True), _ep_logf, (fb,), (pl.BlockSpec((1, _LANES), lambda j, i: (0, 0)),),
               jax.ShapeDtypeStruct((b, s, _LANES), jnp.float32),
               pl.BlockSpec((1, tm, _LANES), lambda j, i: (i // nsb, i % nsb, 0)),
               tm, _LANES, "fox_logf")
    caug = _cumsum_aug(lf, n_heads)
    o = _fox_core(q, k, caug, vt, t=t)
    return _outproj(o.reshape(b, 1, s, branch), gate, w_out[layer].astype(_CDT), tm, min(1024, d),
                    "fox_out")


def kernel(x, a_w_in, a_q_norm_g, a_w_uq, a_kidx_norm_g, a_kidx_norm_b, a_w_out,
           b_w_in, b_forget_bias, b_w_out, ln_g, ln_b):
    b, s, d = x.shape
    depth = ln_g.shape[0]
    alpha = (2 * depth) ** 0.25
    xf = x.reshape(b * s, d)
    xc = xf.astype(_CDT)
    a_w_in_t = jnp.swapaxes(a_w_in, 1, 2)
    b_w_in_t = jnp.swapaxes(b_w_in, 1, 2)
    tm_ln = min(256, s)
    for i in range(depth):
        j = i // 2
        if i % 2 == 0:
            h = _dsa_layer(xc, a_w_in_t, a_q_norm_g, a_w_uq, a_kidx_norm_g, a_kidx_norm_b, a_w_out, j, b, s)
        else:
            h = _fox_layer(xc, b_w_in_t, b_forget_bias, b_w_out, j, b, s)
        last = i == depth - 1
        outs = _resid_ln(h, xf, ln_g[i], ln_b[i], alpha, tm_ln, not last, "resid_ln_%d" % i)
        xf = outs[0]
        if not last:
            xc = outs[1]
    return xf.reshape(b, s, d)
```

```python
import functools
import math

import jax
import jax.numpy as jnp
from jax import lax
from jax.experimental import pallas as pl
from jax.experimental.pallas import tpu as pltpu

_CDT = jnp.bfloat16
_HEAD_DIM = 128
_GROUP = 8
_TOPK_MAX = 256
_ROPE_THETA = 10000.0
_LN_EPS = 1e-5
_RMS_EPS = 1e-6
_LANES = 128
_SUBLANES = 8
_MXU_DIM = 256
_VMEM_LIMIT = 56 * 1024 * 1024
_PROJ_TILE_BUDGET = 40 * 1024 * 1024
_VT_ROWS = _HEAD_DIM + 16
_INT_MIN = -2 ** 31
_MASKED = -1e30
_M_INIT = -1e29
_LOG2E = math.log2(math.e)
_ROW_CHUNK = 256

_NT = (((1,), (1,)), ((), ()))


def _params(*sem):
    return pltpu.CompilerParams(dimension_semantics=sem, vmem_limit_bytes=_VMEM_LIMIT)


def _rope_tile(x, cos, sin_signed):
    return x * cos + pltpu.roll(x, _HEAD_DIM // 2, axis=1) * sin_signed


class _Cols:
    def __init__(self, w, col0=0, n=None, layer=None, transposed=False):
        self.w, self.col0, self.layer, self.transposed = w, col0, layer, transposed
        self.n_total = w.shape[-2] if transposed else w.shape[-1]
        self.k = w.shape[-1] if transposed else w.shape[-2]
        self.n = self.n_total - col0 if n is None else n

    def block_aligned(self, tn):
        return self.col0 % tn == 0

    def aligned(self, tn):
        start_ok = self.col0 % _SUBLANES == 0 if self.transposed else self.block_aligned(tn)
        return start_ok and self.col0 + -(-self.n // tn) * tn <= self.n_total

    def materialize(self):
        w = self.w if self.layer is None else self.w[self.layer]
        if self.transposed:
            return _Cols(w[self.col0:self.col0 + self.n, :].astype(_CDT), transposed=True)
        return _Cols(w[:, self.col0:self.col0 + self.n].astype(_CDT))


def _proj_tiles(s, cols):
    itemsize = cols.w.dtype.itemsize
    act = jnp.dtype(_CDT).itemsize
    for tn in (1024, 512, 256, 128):
        if tn > cols.n:
            continue
        w_bytes = cols.k * tn * (2 * itemsize + (act if cols.w.dtype != _CDT else 0))
        for tm in (1024, 512, 256):
            if tm <= s and w_bytes + 2 * tm * cols.k * act <= _PROJ_TILE_BUDGET:
                return tm, tn
    raise ValueError("projection tiles do not fit VMEM")


def _proj(a, cols, epilogue, extras, extra_specs, out_shape, out_specs, tm, tn, name, row_chunk=None):
    m, k = a.shape
    assert cols.aligned(tn) and k == cols.k
    convert = cols.w.dtype != _CDT
    n_extra = len(extras)
    w_block = (tn, k) if cols.transposed else (k, tn)

    def body(a_ref, w_ref, *rest):
        if convert:
            wc_ref = rest[-1]
            rest = rest[:-1]

            @pl.when(pl.program_id(1) == 0)
            def _():
                wc_ref[...] = w_ref[...].reshape(w_block).astype(_CDT)

            w = wc_ref[...]
        else:
            w = w_ref[...].reshape(w_block)
        rc = tm if row_chunk is None else row_chunk
        for r in range(tm // rc):
            rows = slice(r * rc, (r + 1) * rc)
            if cols.transposed:
                res = lax.dot_general(a_ref[rows, :], w, _NT, preferred_element_type=jnp.float32)
            else:
                res = jnp.dot(a_ref[rows, :], w, preferred_element_type=jnp.float32)
            epilogue(res, rest[:n_extra], rest[n_extra:], rows)

    lead_index = () if cols.layer is None else (cols.layer,)
    if cols.block_aligned(tn):
        c0 = cols.col0 // tn
        lead = () if cols.layer is None else (None,)
        if cols.transposed:
            w_spec = pl.BlockSpec(lead + (tn, k), lambda j, i: lead_index + (c0 + j, 0))
        else:
            w_spec = pl.BlockSpec(lead + (k, tn), lambda j, i: lead_index + (0, c0 + j))
    else:
        assert cols.transposed
        lead = () if cols.layer is None else (pl.Element(1),)
        w_spec = pl.BlockSpec(lead + (pl.Element(tn), pl.Element(k)),
                              lambda j, i: lead_index + (pl.multiple_of(cols.col0 + j * tn, _SUBLANES), 0))
    return pl.pallas_call(
        body,
        grid=(-(-cols.n // tn), m // tm),
        in_specs=[pl.BlockSpec((tm, k), lambda j, i: (i, 0)), w_spec] + list(extra_specs),
        out_specs=out_specs,
        out_shape=out_shape,
        scratch_shapes=[pltpu.VMEM(w_block, _CDT)] if convert else [],
        compiler_params=_params("arbitrary", "arbitrary"),
        name=name,
    )(a, cols.w, *extras)


def _ep_rms(res, extras, outs, rows):
    (g_ref,) = extras
    (o_ref,) = outs
    y = res * lax.rsqrt(jnp.mean(res * res, axis=-1, keepdims=True) + _RMS_EPS)
    o_ref[rows, :] = (y * g_ref[...]).astype(o_ref.dtype)


def _ep_heads(res, extras, outs, rows, *, rope, scale):
    (o_ref,) = outs
    nh = res.shape[1] // _HEAD_DIM
    if rope:
        cos = extras[0][rows, :]
        sin = extras[1][rows, :]
    for h in range(nh):
        xh = res[:, h * _HEAD_DIM:(h + 1) * _HEAD_DIM]
        if rope:
            xh = _rope_tile(xh, cos, sin)
        if scale != 1.0:
            xh = xh * scale
        o_ref[0, h, rows, :] = xh.astype(o_ref.dtype)


def _ep_vt(res, extras, outs, rows, *, t):
    (vt_ref,) = outs
    assert rows.start % t == 0 and res.shape[0] % t == 0
    for h in range(res.shape[1] // _HEAD_DIM):
        vh = res[:, h * _HEAD_DIM:(h + 1) * _HEAD_DIM]
        for c in range(res.shape[0] // t):
            dst = rows.start // t + c
            vt_ref[0, h, dst, :_HEAD_DIM, :] = vh[c * t:(c + 1) * t, :].T.astype(vt_ref.dtype)
            vt_ref[0, h, dst, _HEAD_DIM:, :] = jnp.ones((_VT_ROWS - _HEAD_DIM, t), vt_ref.dtype)


def _ep_kidx(res, extras, outs, rows, *, idx_heads, w_scale):
    cos_ref, sin_ref, g_ref, b_ref = extras
    kidx_ref, wt_ref = outs
    x = res[:, :_HEAD_DIM]
    mu = jnp.mean(x, axis=-1, keepdims=True)
    xc = x - mu
    var = jnp.mean(xc * xc, axis=-1, keepdims=True)
    y = xc * lax.rsqrt(var + _LN_EPS) * g_ref[...] + b_ref[...]
    kidx_ref[0] = _rope_tile(y, cos_ref[...], sin_ref[...]).astype(kidx_ref.dtype)
    wt = (res[:, _HEAD_DIM:] * w_scale).T
    wt_ref[0] = wt[:idx_heads, :]


def _ep_silu(res, extras, outs, rows):
    (o_ref,) = outs
    o_ref[0, 0, rows, :] = (res * jax.nn.sigmoid(res)).astype(o_ref.dtype)


def _ep_logf(res, extras, outs, rows):
    (bias_ref,) = extras
    (o_ref,) = outs
    z = res + bias_ref[...]
    o_ref[0] = jnp.minimum(z, 0.0) - jnp.log(1.0 + jnp.exp(-jnp.abs(z)))


def _outproj(o, gate, w, tm, tn, name):
    b, g, s, c = o.shape
    d = w.shape[1]
    nsb = s // tm

    def body(o_ref, g_ref, w_ref, out_ref):
        acc = None
        for gi in range(g):
            a = (o_ref[0, gi].astype(jnp.float32) * g_ref[0, gi].astype(jnp.float32)).astype(_CDT)
            part = jnp.dot(a, w_ref[gi * c:(gi + 1) * c, :], preferred_element_type=jnp.float32)
            acc = part if acc is None else acc + part
        out_ref[...] = acc

    return pl.pallas_call(
        body,
        grid=(d // tn, b * nsb),
        in_specs=[pl.BlockSpec((1, g, tm, c), lambda j, i: (i // nsb, 0, i % nsb, 0)),
                  pl.BlockSpec((1, g, tm, c), lambda j, i: (i // nsb, 0, i % nsb, 0)),
                  pl.BlockSpec((g * c, tn), lambda j, i: (0, j))],
        out_specs=pl.BlockSpec((tm, tn), lambda j, i: (i, j)),
        out_shape=jax.ShapeDtypeStruct((b * s, d), jnp.float32),
        compiler_params=_params("arbitrary", "arbitrary"),
        name=name,
    )(o, gate, w)


def _resid_ln(h, x, g, b, alpha, tm, with_cdt, name):
    m, d = x.shape

    def body(h_ref, x_ref, g_ref, b_ref, *outs):
        y = alpha * x_ref[...] + h_ref[...]
        mu = jnp.mean(y, axis=-1, keepdims=True)
        yc = y - mu
        var = jnp.mean(yc * yc, axis=-1, keepdims=True)
        z = yc * lax.rsqrt(var + _LN_EPS) * g_ref[...] + b_ref[...]
        outs[0][...] = z
        if with_cdt:
            outs[1][...] = z.astype(_CDT)

    row = pl.BlockSpec((tm, d), lambda i: (i, 0))
    vec = pl.BlockSpec((1, d), lambda i: (0, 0))
    out_shape = [jax.ShapeDtypeStruct((m, d), jnp.float32)]
    out_specs = [row]
    if with_cdt:
        out_shape.append(jax.ShapeDtypeStruct((m, d), _CDT))
        out_specs.append(row)
    return pl.pallas_call(
        body,
        grid=(m // tm,),
        in_specs=[row, row, vec, vec],
        out_specs=out_specs,
        out_shape=out_shape,
        compiler_params=_params("arbitrary"),
        name=name,
    )(h, x, g.reshape(1, d), b.reshape(1, d))


def _col_reduce(x, op):
    rows, n = x.shape
    fold = 8 * _SUBLANES
    if rows > fold and rows % fold == 0:
        x = op(x.reshape(rows // fold, fold, n), axis=0)
    return op(x, axis=0, keepdims=True)


def _softmax_step(sc, tile_max, m_ref, cols):
    m_old = m_ref[:, cols]
    m_new = jnp.maximum(m_old, tile_max)
    m_ref[:, cols] = m_new
    return jnp.exp2(sc - m_new).astype(_CDT), jnp.exp2(m_old - m_new)


def _pv_step(vta, p, alpha, acc_ref, cols):
    pv = jnp.dot(vta, p, preferred_element_type=jnp.float32)
    acc_ref[:, cols] = pv if alpha is None else alpha * acc_ref[:, cols] + pv


def _float_sort_key(x):
    bits = pltpu.bitcast(x, jnp.int32)
    return bits ^ ((bits >> 31) & jnp.int32(0x7FFFFFFF))


def _dsa_core(q, qi, wt, kidx, k, vt, *, topk, tk1, tk, hchunk):
    b, n_heads, s, _ = q.shape
    idx_heads = qi.shape[1]
    n_kv = k.shape[1]
    tq = _LANES
    gq = _GROUP * tq
    log_s = int(math.log2(s))
    assert 1 << log_s == s and s % tk == 0 and tk % tk1 == 0 and tk1 % tq == 0
    assert idx_heads % hchunk == 0 and gq % _MXU_DIM == 0

    def body(q_ref, qi_ref, wt_ref, kidx_ref, k_ref, vt_ref, o_ref,
             key_ref, mask_ref, sc_ref, p_ref, al_ref, mx_ref, m_ref, acc_ref):
        i = pl.program_id(1)
        n1 = (i * tq + tq + tk1 - 1) // tk1
        n3 = (i * tq + tq + tk - 1) // tk

        q_pos1 = i * tq + lax.broadcasted_iota(jnp.int32, (tk1, tq), 1)
        s_iota1 = lax.broadcasted_iota(jnp.int32, (tk1, tq), 0)

        def score_tile(kt, carry):
            ks = pl.multiple_of(kt * tk1, tk1)
            kx = kidx_ref[0, pl.ds(ks, tk1), :]
            acc = jnp.zeros((tk1, tq), jnp.float32)
            for hc in range(idx_heads // hchunk):
                qq = qi_ref[0, hc * hchunk:(hc + 1) * hchunk].reshape(hchunk * tq, _HEAD_DIM)
                d = lax.dot_general(kx, qq, _NT, preferred_element_type=jnp.float32)
                for hh in range(hchunk):
                    h = hc * hchunk + hh
                    acc = acc + jnp.maximum(d[:, hh * tq:(hh + 1) * tq], 0.0) * wt_ref[0, h:h + 1, :]
            key = _float_sort_key(acc + 0.0)
            key_ref[pl.ds(ks, tk1), :] = jnp.where(ks + s_iota1 <= q_pos1, key, _INT_MIN)
            return carry

        lax.fori_loop(0, n1, score_tile, 0)

        def pad_tile(kt, carry):
            key_ref[pl.ds(pl.multiple_of(kt * tk1, tk1), tk1), :] = jnp.full((tk1, tq), _INT_MIN, jnp.int32)
            return carry

        lax.fori_loop(n1, n3 * (tk // tk1), pad_tile, 0)

        q_pos = i * tq + lax.broadcasted_iota(jnp.int32, (tk, tq), 1)
        s_iota = lax.broadcasted_iota(jnp.int32, (tk, tq), 0)

        fold = 8 * _SUBLANES

        def count(pred_fn):
            def tile(kt, cnt):
                ks = pl.multiple_of(kt * tk, tk)
                c = jnp.where(pred_fn(key_ref[pl.ds(ks, tk), :], ks), 1.0, 0.0)
                return cnt + jnp.sum(c.reshape(tk // fold, fold, tq), axis=0)
            cnt = lax.fori_loop(0, n3, tile, jnp.zeros((fold, tq), jnp.float32))
            return jnp.sum(cnt, axis=0, keepdims=True)

        def value_step(t, prefix):
            cand = prefix | jnp.left_shift(jnp.int32(1), 31 - t)
            cand_s = cand ^ _INT_MIN
            cnt = count(lambda key, ks: key >= cand_s)
            return jnp.where(cnt >= topk, cand, prefix)

        prefix = lax.fori_loop(0, 32, value_step, jnp.zeros((1, tq), jnp.int32))
        thr = prefix ^ _INT_MIN

        def tie_break():
            need = topk - count(lambda key, ks: key > thr)

            def index_step(t, j):
                cand = j | jnp.left_shift(jnp.int32(1), log_s - 1 - t)
                cnt = count(lambda key, ks: (key == thr) & (ks + s_iota < cand))
                return jnp.where(cnt < need, cand, j)

            return lax.fori_loop(0, log_s, index_step, jnp.zeros((1, tq), jnp.int32))

        n_ge = count(lambda key, ks: key >= thr)
        j_last = lax.cond(jnp.max(n_ge) > topk, tie_break,
                          lambda: jnp.full((1, tq), s, jnp.int32))

        def mask_tile(kt, carry):
            ks = pl.multiple_of(kt * tk, tk)
            key = key_ref[pl.ds(ks, tk), :]
            s_pos = ks + s_iota
            sel = (key > thr) | ((key == thr) & (s_pos <= j_last))
            mask_ref[pl.ds(ks, tk), :] = jnp.where(sel & (s_pos <= q_pos), 0.0, _MASKED).astype(_CDT)
            return carry

        lax.fori_loop(0, n3, mask_tile, 0)

        r = lax.broadcasted_iota(jnp.int32, (gq, tq), 0)
        c = lax.broadcasted_iota(jnp.int32, (gq, tq), 1)
        one_hot = jnp.where((r & (tq - 1)) == c, 1.0, 0.0).astype(_CDT)

        chunks = [slice(cc * _MXU_DIM, (cc + 1) * _MXU_DIM) for cc in range(gq // _MXU_DIM)]

        def kv_group(g, carry):
            qg = q_ref[0, pl.ds(g * _GROUP, _GROUP)].reshape(gq, _HEAD_DIM)
            qaug = jnp.concatenate([qg, one_hot], axis=1)
            m_ref[...] = jnp.full((1, gq), _M_INIT, jnp.float32)

            def logits(kt):
                kaug = jnp.concatenate([k_ref[0, g, pl.ds(kt * tk, tk), :],
                                        mask_ref[pl.ds(kt * tk, tk), :]], axis=1)
                for cols in chunks:
                    sc = lax.dot_general(kaug, qaug[cols], _NT, preferred_element_type=jnp.float32)
                    sc_ref[kt & 1, :, cols] = sc
                    mx_ref[kt & 1, :, cols] = _col_reduce(sc, jnp.max)

            def softmax(kt):
                for cols in chunks:
                    p_ref[kt & 1, :, cols], al_ref[kt & 1, :, cols] = _softmax_step(
                        sc_ref[kt & 1, :, cols], mx_ref[kt & 1, :, cols], m_ref, cols)

            def pv(kt):
                vta = vt_ref[0, g, kt]
                for cols in chunks:
                    _pv_step(vta, p_ref[kt & 1, :, cols], None if kt == 0 else al_ref[kt & 1, :, cols],
                             acc_ref, cols)

            for n_tiles in range(1, s // tk + 1):
                @pl.when(n3 == n_tiles)
                def _():
                    for step in range(n_tiles + 2):
                        if step < n_tiles:
                            logits(step)
                        if 1 <= step <= n_tiles:
                            softmax(step - 1)
                        if step >= 2:
                            pv(step - 2)

            acc = acc_ref[...]
            ot = acc[:_HEAD_DIM] * (1.0 / acc[_HEAD_DIM:_HEAD_DIM + 1])
            for hh in range(_GROUP):
                o_ref[0, g, :, hh * _HEAD_DIM:(hh + 1) * _HEAD_DIM] = (
                    ot[:, hh * tq:(hh + 1) * tq].T.astype(o_ref.dtype))
            return carry

        lax.fori_loop(0, n_kv, kv_group, 0)

    nq = s // tq
    return pl.pallas_call(
        body,
        grid=(b, nq),
        in_specs=[pl.BlockSpec((1, n_heads, tq, _HEAD_DIM), lambda bi, i: (bi, 0, i, 0)),
                  pl.BlockSpec((1, idx_heads, tq, _HEAD_DIM), lambda bi, i: (bi, 0, i, 0)),
                  pl.BlockSpec((1, idx_heads, tq), lambda bi, i: (bi, 0, i)),
                  pl.BlockSpec((1, s, _HEAD_DIM), lambda bi, i: (bi, 0, 0)),
                  pl.BlockSpec((1, n_kv, s, _HEAD_DIM), lambda bi, i: (bi, 0, 0, 0)),
                  pl.BlockSpec((1, n_kv, s // tk, _VT_ROWS, tk), lambda bi, i: (bi, 0, 0, 0, 0))],
        out_specs=pl.BlockSpec((1, n_kv, tq, _GROUP * _HEAD_DIM), lambda bi, i: (bi, 0, i, 0)),
        out_shape=jax.ShapeDtypeStruct((b, n_kv, s, _GROUP * _HEAD_DIM), _CDT),
        scratch_shapes=[pltpu.VMEM((s, tq), jnp.int32),
                        pltpu.VMEM((s, tq), _CDT),
                        pltpu.VMEM((2, tk, gq), jnp.float32),
                        pltpu.VMEM((2, tk, gq), _CDT),
                        pltpu.VMEM((2, 1, gq), jnp.float32),
                        pltpu.VMEM((2, 1, gq), jnp.float32),
                        pltpu.VMEM((1, gq), jnp.float32),
                        pltpu.VMEM((_VT_ROWS, gq), jnp.float32)],
        compiler_params=_params("arbitrary", "arbitrary"),
        name="dsa_core",
    )(q, qi, wt, kidx, k, vt)


def _cumsum_aug(lf, n_heads):
    b, s, w = lf.shape
    blk = _LANES

    def body(lf_ref, out_ref, carry_ref):
        @pl.when(pl.program_id(1) == 0)
        def _():
            carry_ref[...] = jnp.zeros_like(carry_ref)
        r = lax.broadcasted_iota(jnp.int32, (blk, blk), 0)
        c = lax.broadcasted_iota(jnp.int32, (blk, blk), 1)
        tri = jnp.where(r >= c, 1.0, 0.0).astype(jnp.float32)
        cs = jnp.dot(tri, lf_ref[0], preferred_element_type=jnp.float32,
                     precision=lax.Precision.HIGHEST) + carry_ref[...]
        carry_ref[...] = cs[blk - 1:blk, :]
        c2 = cs * _LOG2E
        for h in range(n_heads):
            col = c2[:, h:h + 1]
            hi = col.astype(_CDT).astype(jnp.float32)
            mid = (col - hi).astype(_CDT).astype(jnp.float32)
            lo = (col - hi - mid).astype(_CDT).astype(jnp.float32)
            tile = jnp.where(c == 0, hi, jnp.where(c == 1, mid, jnp.where(c == 2, lo, 0.0)))
            out_ref[0, h] = tile.astype(out_ref.dtype)

    return pl.pallas_call(
        body,
        grid=(b, s // blk),
        in_specs=[pl.BlockSpec((1, blk, w), lambda bi, i: (bi, i, 0))],
        out_specs=pl.BlockSpec((1, n_heads, blk, _LANES), lambda bi, i: (bi, 0, i, 0)),
        out_shape=jax.ShapeDtypeStruct((b, n_heads, s, _LANES), _CDT),
        scratch_shapes=[pltpu.VMEM((1, w), jnp.float32)],
        compiler_params=_params("arbitrary", "arbitrary"),
        name="fox_cumsum",
    )(lf)


def _fox_core(q, k, caug, vt, *, t):
    b, n_heads, s, _ = q.shape
    assert t % _MXU_DIM == 0 and s % t == 0

    chunks = [slice(cc * _MXU_DIM, (cc + 1) * _MXU_DIM) for cc in range(t // _MXU_DIM)]

    tiles = [(i, j) for i in range(s // t) for j in range(i + 1)]

    def visible(i, j, cols):
        return min(t, cols.stop) if i == j else t

    def body(q_ref, k_ref, c_ref, vt_ref, o_ref, kaug_ref, sc_ref, p_ref, al_ref, mx_ref, m_ref, acc_ref):
        kaug_ref[:, :_HEAD_DIM] = k_ref[0, 0]
        kaug_ref[:, _HEAD_DIM:] = c_ref[0, 0]
        lane = lax.broadcasted_iota(jnp.int32, (t, _HEAD_DIM), 1)
        minus_ones = jnp.where(lane < 3, -1.0, 0.0).astype(_CDT)

        def logits(n):
            i, j = tiles[n]
            qaug = jnp.concatenate([q_ref[0, 0, i * t:(i + 1) * t, :], minus_ones], axis=1)
            ka = kaug_ref[j * t:(j + 1) * t, :]
            for cols in chunks:
                kr = visible(i, j, cols)
                sc = lax.dot_general(ka[:kr], qaug[cols], _NT, preferred_element_type=jnp.float32)
                if i == j:
                    key_i = lax.broadcasted_iota(jnp.int32, (kr, _MXU_DIM), 0)
                    qry_i = cols.start + lax.broadcasted_iota(jnp.int32, (kr, _MXU_DIM), 1)
                    sc = jnp.where(key_i <= qry_i, sc, -jnp.inf)
                sc_ref[n & 1, :kr, cols] = sc
                mx_ref[n & 1, :, cols] = _col_reduce(sc, jnp.max)

        def softmax(n):
            i, j = tiles[n]
            if j == 0:
                m_ref[...] = jnp.full((1, t), _M_INIT, jnp.float32)
            for cols in chunks:
                kr = visible(i, j, cols)
                p_ref[n & 1, :kr, cols], al_ref[n & 1, :, cols] = _softmax_step(
                    sc_ref[n & 1, :kr, cols], mx_ref[n & 1, :, cols], m_ref, cols)

        def pv(n):
            i, j = tiles[n]
            vta = vt_ref[0, 0, j]
            for cols in chunks:
                kr = visible(i, j, cols)
                _pv_step(vta[:, :kr], p_ref[n & 1, :kr, cols],
                         None if j == 0 else al_ref[n & 1, :, cols], acc_ref, cols)
            if j == i:
                acc = acc_ref[...]
                ot = acc[:_HEAD_DIM] * (1.0 / acc[_HEAD_DIM:_HEAD_DIM + 1])
                for cq in range(t // _LANES):
                    o_ref[0, i * t + cq * _LANES:i * t + (cq + 1) * _LANES, :] = (
                        ot[:, cq * _LANES:(cq + 1) * _LANES].T.astype(o_ref.dtype))

        for step in range(len(tiles) + 2):
            if step < len(tiles):
                logits(step)
            if 1 <= step <= len(tiles):
                softmax(step - 1)
            if step >= 2:
                pv(step - 2)

    head = pl.BlockSpec((1, 1, s, _HEAD_DIM), lambda bi, h: (bi, h, 0, 0))
    return pl.pallas_call(
        body,
        grid=(b, n_heads),
        in_specs=[head, head, head,
                  pl.BlockSpec((1, 1, s // t, _VT_ROWS, t), lambda bi, h: (bi, h, 0, 0, 0))],
        out_specs=pl.BlockSpec((1, s, _HEAD_DIM), lambda bi, h: (bi, 0, h)),
        out_shape=jax.ShapeDtypeStruct((b, s, n_heads * _HEAD_DIM), _CDT),
        scratch_shapes=[pltpu.VMEM((s, 2 * _HEAD_DIM), _CDT),
                        pltpu.VMEM((2, t, t), jnp.float32),
                        pltpu.VMEM((2, t, t), _CDT),
                        pltpu.VMEM((2, 1, t), jnp.float32),
                        pltpu.VMEM((2, 1, t), jnp.float32),
                        pltpu.VMEM((1, t), jnp.float32),
                        pltpu.VMEM((_VT_ROWS, t), jnp.float32)],
        compiler_params=_params("arbitrary", "arbitrary"),
        name="fox_core",
    )(q, k, caug, vt)


def _rope_tables(s):
    half = _HEAD_DIM // 2
    inv = _ROPE_THETA ** (-jnp.arange(half, dtype=jnp.float32) / half)
    ang = jnp.arange(s, dtype=jnp.float32)[:, None] * inv[None, :]
    cos, sin = jnp.cos(ang), jnp.sin(ang)
    return jnp.concatenate([cos, cos], axis=1), jnp.concatenate([-sin, sin], axis=1)


def _placed(cols, s, tn_cap=None):
    tm, tn = _proj_tiles(s, cols)
    if tn_cap is not None:
        tn = min(tn, tn_cap)
    if not cols.aligned(tn):
        cols = cols.materialize()
        tm, tn = _proj_tiles(s, cols)
        if tn_cap is not None:
            tn = min(tn, tn_cap)
    return cols, tm, tn


def _heads_call(a, cols, b, s, *, rope, scale, cos, sin, name):
    cols, tm, tn = _placed(cols, s)
    nsb = s // tm
    extras, extra_specs = (), ()
    if rope:
        extras = (cos, sin)
        extra_specs = (pl.BlockSpec((tm, _HEAD_DIM), lambda j, i: (i % nsb, 0)),) * 2
    nh_t = tn // _HEAD_DIM
    return _proj(
        a, cols, functools.partial(_ep_heads, rope=rope, scale=scale), extras, extra_specs,
        jax.ShapeDtypeStruct((b, cols.n // _HEAD_DIM, s, _HEAD_DIM), _CDT),
        pl.BlockSpec((1, nh_t, tm, _HEAD_DIM), lambda j, i: (i // nsb, j, i % nsb, 0)),
        tm, tn, name, row_chunk=min(_ROW_CHUNK, tm))


def _vt_call(a, cols, b, s, t, name):
    cols, tm, tn = _placed(cols, s)
    nsb = s // tm
    nh_t = tn // _HEAD_DIM
    return _proj(
        a, cols, functools.partial(_ep_vt, t=t), (), (),
        jax.ShapeDtypeStruct((b, cols.n // _HEAD_DIM, s // t, _VT_ROWS, t), _CDT),
        pl.BlockSpec((1, nh_t, tm // t, _VT_ROWS, t), lambda j, i: (i // nsb, j, i % nsb, 0, 0)),
        tm, tn, name, row_chunk=t)


def _gate_call(a, cols, b, s, groups, name):
    c = cols.n // groups
    cols, tm, tn = _placed(cols, s, tn_cap=c)
    nsb = s // tm
    per = c // tn
    return _proj(
        a, cols, _ep_silu, (), (),
        jax.ShapeDtypeStruct((b, groups, s, c), _CDT),
        pl.BlockSpec((1, 1, tm, tn), lambda j, i: (i // nsb, j // per, i % nsb, j % per)),
        tm, tn, name, row_chunk=min(_ROW_CHUNK, tm))


def _dsa_layer(xc, w_in_t, q_norm_g, w_uq, kidx_g, kidx_b, w_out, layer, b, s):
    d = xc.shape[1]
    q_lora = w_uq.shape[1]
    branch = w_out.shape[1]
    idx_heads = (w_uq.shape[2] - branch) // _HEAD_DIM
    kv = (w_in_t.shape[1] - q_lora - _HEAD_DIM - idx_heads - branch) // 2
    w_in = functools.partial(_Cols, w_in_t, layer=layer, transposed=True)
    n_kv = kv // _HEAD_DIM
    n_heads = branch // _HEAD_DIM
    assert n_heads == n_kv * _GROUP and idx_heads <= _LANES
    topk = min(_TOPK_MAX, s // 4)
    o0 = q_lora
    o1 = o0 + 2 * kv
    o2 = o1 + _HEAD_DIM
    o3 = o2 + idx_heads
    cos, sin = _rope_tables(s)
    tk1 = min(256, s)
    tk = min(512, s)

    c_cq = w_in(0, q_lora).materialize()
    tm = min(512, s)
    cq = _proj(xc, c_cq, _ep_rms, (q_norm_g[layer].reshape(1, q_lora),),
               (pl.BlockSpec((1, q_lora), lambda j, i: (0, 0)),),
               jax.ShapeDtypeStruct((b * s, q_lora), _CDT),
               pl.BlockSpec((tm, q_lora), lambda j, i: (i, 0)), tm, q_lora, "dsa_cq")
    k = _heads_call(xc, w_in(o0, kv), b, s, rope=True, scale=1.0, cos=cos, sin=sin, name="dsa_k")
    vt = _vt_call(xc, w_in(o0 + kv, kv), b, s, tk, "dsa_v")

    c_ki = w_in(o1, 2 * _LANES)
    if not c_ki.aligned(2 * _LANES):
        w_ki = jnp.pad(w_in_t[layer][o1:o3, :], ((0, _LANES - idx_heads), (0, 0)))
        c_ki = _Cols(w_ki.astype(_CDT), transposed=True)
    nsb = s // tm
    rope_specs = (pl.BlockSpec((tm, _HEAD_DIM), lambda j, i: (i % nsb, 0)),) * 2
    w_scale = idx_heads ** -0.5 * _HEAD_DIM ** -0.5
    vec = pl.BlockSpec((1, _HEAD_DIM), lambda j, i: (0, 0))
    kidx, wt = _proj(xc, c_ki, functools.partial(_ep_kidx, idx_heads=idx_heads, w_scale=w_scale),
                     (cos, sin, kidx_g[layer].reshape(1, _HEAD_DIM), kidx_b[layer].reshape(1, _HEAD_DIM)),
                     rope_specs + (vec, vec),
                     [jax.ShapeDtypeStruct((b, s, _HEAD_DIM), _CDT),
                      jax.ShapeDtypeStruct((b, idx_heads, s), jnp.float32)],
                     [pl.BlockSpec((1, tm, _HEAD_DIM), lambda j, i: (i // nsb, i % nsb, 0)),
                      pl.BlockSpec((1, idx_heads, tm), lambda j, i: (i // nsb, 0, i % nsb))],
                     tm, 2 * _LANES, "dsa_kidx")
    gate = _gate_call(xc, w_in(o3, branch), b, s, n_kv, "dsa_gate")
    q = _heads_call(cq, _Cols(w_uq, 0, branch, layer), b, s, rope=True,
                    scale=_HEAD_DIM ** -0.5 * _LOG2E, cos=cos, sin=sin, name="dsa_q")
    qi = _heads_call(cq, _Cols(w_uq, branch, None, layer), b, s, rope=True, scale=1.0, cos=cos, sin=sin,
                     name="dsa_qi")
    o = _dsa_core(q, qi, wt, kidx, k, vt, topk=topk, tk1=tk1, tk=tk, hchunk=min(8, idx_heads))
    return _outproj(o, gate, w_out[layer].astype(_CDT), min(512, s), min(1024, d), "dsa_out")


def _fox_layer(xc, w_in_t, forget_bias, w_out, layer, b, s):
    d = xc.shape[1]
    branch = w_out.shape[1]
    n_heads = forget_bias.shape[1]
    assert branch == n_heads * _HEAD_DIM and n_heads <= _LANES
    w_in = functools.partial(_Cols, w_in_t, layer=layer, transposed=True)
    w_f = jnp.pad(w_in_t[layer][4 * branch:, :], ((0, _LANES - n_heads), (0, 0)))
    fb = jnp.pad(forget_bias[layer], (0, _LANES - n_heads)).reshape(1, _LANES)
    tm = min(512, s)
    nsb = s // tm
    t = min(512, s)
    q = _heads_call(xc, w_in(0, branch), b, s, rope=False,
                    scale=_HEAD_DIM ** -0.5 * _LOG2E, cos=None, sin=None, name="fox_q")
    k = _heads_call(xc, w_in(branch, branch), b, s, rope=False, scale=1.0,
                    cos=None, sin=None, name="fox_k")
    vt = _vt_call(xc, w_in(2 * branch, branch), b, s, t, "fox_v")
    gate = _gate_call(xc, w_in(3 * branch, branch), b, s, 1, "fox_gate")
    lf = _proj(xc, _Cols(w_f, transposed=True), _ep_logf, (fb,), (pl.BlockSpec((1, _LANES), lambda j, i: (0, 0)),),
               jax.ShapeDtypeStruct((b, s, _LANES), jnp.float32),
               pl.BlockSpec((1, tm, _LANES), lambda j, i: (i // nsb, i % nsb, 0)),
               tm, _LANES, "fox_logf")
    caug = _cumsum_aug(lf, n_heads)
    o = _fox_core(q, k, caug, vt, t=t)
    return _outproj(o.reshape(b, 1, s, branch), gate, w_out[layer].astype(_CDT), tm, min(1024, d),
                    "fox_out")


def kernel(x, a_w_in, a_q_norm_g, a_w_uq, a_kidx_norm_g, a_kidx_norm_b, a_w_out,
           b_w_in, b_forget_bias, b_w_out, ln_g, ln_b):
    b, s, d = x.shape
    depth = ln_g.shape[0]
    alpha = (2 * depth) ** 0.25
    xf = x.reshape(b * s, d)
    xc = xf.astype(_CDT)
    a_w_in_t = jnp.swapaxes(a_w_in, 1, 2)
    b_w_in_t = jnp.swapaxes(b_w_in, 1, 2)
    tm_ln = min(256, s)
    for i in range(depth):
        j = i // 2
        if i % 2 == 0:
            h = _dsa_layer(xc, a_w_in_t, a_q_norm_g, a_w_uq, a_kidx_norm_g, a_kidx_norm_b, a_w_out, j, b, s)
        else:
            h = _fox_layer(xc, b_w_in_t, b_forget_bias, b_w_out, j, b, s)
        last = i == depth - 1
        outs = _resid_ln(h, xf, ln_g[i], ln_b[i], alpha, tm_ln, not last, "resid_ln_%d" % i)
        xf = outs[0]
        if not last:
            xc = outs[1]
    return xf.reshape(b, s, d)
```

```python
import functools
import math

import jax
import jax.numpy as jnp
from jax import lax
from jax.experimental import pallas as pl
from jax.experimental.pallas import tpu as pltpu

_CDT = jnp.bfloat16
_HEAD_DIM = 128
_GROUP = 8
_TOPK_MAX = 256
_ROPE_THETA = 10000.0
_LN_EPS = 1e-5
_RMS_EPS = 1e-6
_LANES = 128
_SUBLANES = 8
_MXU_DIM = 256
_VMEM_LIMIT = 56 * 1024 * 1024
_PROJ_TILE_BUDGET = 40 * 1024 * 1024
_VT_ROWS = _HEAD_DIM + 16
_INT_MIN = -2 ** 31
_MASKED = -1e30
_M_INIT = -1e29
_LOG2E = math.log2(math.e)
_ROW_CHUNK = 256

_NT = (((1,), (1,)), ((), ()))


def _params(*sem):
    return pltpu.CompilerParams(dimension_semantics=sem, vmem_limit_bytes=_VMEM_LIMIT)


def _rope_tile(x, cos, sin_signed):
    return x * cos + pltpu.roll(x, _HEAD_DIM // 2, axis=1) * sin_signed


class _Cols:
    def __init__(self, w, col0=0, n=None, layer=None, transposed=False):
        self.w, self.col0, self.layer, self.transposed = w, col0, layer, transposed
        self.n_total = w.shape[-2] if transposed else w.shape[-1]
        self.k = w.shape[-1] if transposed else w.shape[-2]
        self.n = self.n_total - col0 if n is None else n

    def block_aligned(self, tn):
        return self.col0 % tn == 0

    def aligned(self, tn):
        start_ok = self.col0 % _SUBLANES == 0 if self.transposed else self.block_aligned(tn)
        return start_ok and self.col0 + -(-self.n // tn) * tn <= self.n_total

    def materialize(self):
        w = self.w if self.layer is None else self.w[self.layer]
        if self.transposed:
            return _Cols(w[self.col0:self.col0 + self.n, :].astype(_CDT), transposed=True)
        return _Cols(w[:, self.col0:self.col0 + self.n].astype(_CDT))


def _proj_tiles(s, cols):
    itemsize = cols.w.dtype.itemsize
    act = jnp.dtype(_CDT).itemsize
    for tn in (1024, 512, 256, 128):
        if tn > cols.n:
            continue
        w_bytes = cols.k * tn * (2 * itemsize + (act if cols.w.dtype != _CDT else 0))
        for tm in (1024, 512, 256):
            if tm <= s and w_bytes + 2 * tm * cols.k * act <= _PROJ_TILE_BUDGET:
                return tm, tn
    raise ValueError("projection tiles do not fit VMEM")


def _proj(a, cols, epilogue, extras, extra_specs, out_shape, out_specs, tm, tn, name, row_chunk=None):
    m, k = a.shape
    assert cols.aligned(tn) and k == cols.k
    convert = cols.w.dtype != _CDT
    n_extra = len(extras)
    w_block = (tn, k) if cols.transposed else (k, tn)

    def body(a_ref, w_ref, *rest):
        if convert:
            wc_ref = rest[-1]
            rest = rest[:-1]

            @pl.when(pl.program_id(1) == 0)
            def _():
                wc_ref[...] = w_ref[...].reshape(w_block).astype(_CDT)

            w = wc_ref[...]
        else:
            w = w_ref[...].reshape(w_block)
        rc = tm if row_chunk is None else row_chunk
        for r in range(tm // rc):
            rows = slice(r * rc, (r + 1) * rc)
            if cols.transposed:
                res = lax.dot_general(a_ref[rows, :], w, _NT, preferred_element_type=jnp.float32)
            else:
                res = jnp.dot(a_ref[rows, :], w, preferred_element_type=jnp.float32)
            epilogue(res, rest[:n_extra], rest[n_extra:], rows)

    lead_index = () if cols.layer is None else (cols.layer,)
    if cols.block_aligned(tn):
        c0 = cols.col0 // tn
        lead = () if cols.layer is None else (None,)
        if cols.transposed:
            w_spec = pl.BlockSpec(lead + (tn, k), lambda j, i: lead_index + (c0 + j, 0))
        else:
            w_spec = pl.BlockSpec(lead + (k, tn), lambda j, i: lead_index + (0, c0 + j))
    else:
        assert cols.transposed
        lead = () if cols.layer is None else (pl.Element(1),)
        w_spec = pl.BlockSpec(lead + (pl.Element(tn), pl.Element(k)),
                              lambda j, i: lead_index + (pl.multiple_of(cols.col0 + j * tn, _SUBLANES), 0))
    return pl.pallas_call(
        body,
        grid=(-(-cols.n // tn), m // tm),
        in_specs=[pl.BlockSpec((tm, k), lambda j, i: (i, 0)), w_spec] + list(extra_specs),
        out_specs=out_specs,
        out_shape=out_shape,
        scratch_shapes=[pltpu.VMEM(w_block, _CDT)] if convert else [],
        compiler_params=_params("arbitrary", "arbitrary"),
        name=name,
    )(a, cols.w, *extras)


def _ep_rms(res, extras, outs, rows):
    (g_ref,) = extras
    (o_ref,) = outs
    y = res * lax.rsqrt(jnp.mean(res * res, axis=-1, keepdims=True) + _RMS_EPS)
    o_ref[rows, :] = (y * g_ref[...]).astype(o_ref.dtype)


def _ep_heads(res, extras, outs, rows, *, rope, scale):
    (o_ref,) = outs
    nh = res.shape[1] // _HEAD_DIM
    if rope:
        cos = extras[0][rows, :]
        sin = extras[1][rows, :]
    for h in range(nh):
        xh = res[:, h * _HEAD_DIM:(h + 1) * _HEAD_DIM]
        if rope:
            xh = _rope_tile(xh, cos, sin)
        if scale != 1.0:
            xh = xh * scale
        o_ref[0, h, rows, :] = xh.astype(o_ref.dtype)


def _ep_vt(res, extras, outs, rows, *, t):
    (vt_ref,) = outs
    assert rows.start % t == 0 and res.shape[0] % t == 0
    for h in range(res.shape[1] // _HEAD_DIM):
        vh = res[:, h * _HEAD_DIM:(h + 1) * _HEAD_DIM]
        for c in range(res.shape[0] // t):
            dst = rows.start // t + c
            vt_ref[0, h, dst, :_HEAD_DIM, :] = vh[c * t:(c + 1) * t, :].T.astype(vt_ref.dtype)
            vt_ref[0, h, dst, _HEAD_DIM:, :] = jnp.ones((_VT_ROWS - _HEAD_DIM, t), vt_ref.dtype)


def _ep_kidx(res, extras, outs, rows, *, idx_heads, w_scale):
    cos_ref, sin_ref, g_ref, b_ref = extras
    kidx_ref, wt_ref = outs
    x = res[:, :_HEAD_DIM]
    mu = jnp.mean(x, axis=-1, keepdims=True)
    xc = x - mu
    var = jnp.mean(xc * xc, axis=-1, keepdims=True)
    y = xc * lax.rsqrt(var + _LN_EPS) * g_ref[...] + b_ref[...]
    kidx_ref[0] = _rope_tile(y, cos_ref[...], sin_ref[...]).astype(kidx_ref.dtype)
    wt = (res[:, _HEAD_DIM:] * w_scale).T
    wt_ref[0] = wt[:idx_heads, :]


def _ep_silu(res, extras, outs, rows):
    (o_ref,) = outs
    o_ref[0, 0, rows, :] = (res * jax.nn.sigmoid(res)).astype(o_ref.dtype)


def _ep_logf(res, extras, outs, rows):
    (bias_ref,) = extras
    (o_ref,) = outs
    z = res + bias_ref[...]
    o_ref[0] = jnp.minimum(z, 0.0) - jnp.log(1.0 + jnp.exp(-jnp.abs(z)))


def _outproj(o, gate, w, tm, tn, name):
    b, g, s, c = o.shape
    d = w.shape[1]
    nsb = s // tm

    def body(o_ref, g_ref, w_ref, out_ref):
        acc = None
        for gi in range(g):
            a = (o_ref[0, gi].astype(jnp.float32) * g_ref[0, gi].astype(jnp.float32)).astype(_CDT)
            part = jnp.dot(a, w_ref[gi * c:(gi + 1) * c, :], preferred_element_type=jnp.float32)
            acc = part if acc is None else acc + part
        out_ref[...] = acc.astype(out_ref.dtype)

    return pl.pallas_call(
        body,
        grid=(d // tn, b * nsb),
        in_specs=[pl.BlockSpec((1, g, tm, c), lambda j, i: (i // nsb, 0, i % nsb, 0)),
                  pl.BlockSpec((1, g, tm, c), lambda j, i: (i // nsb, 0, i % nsb, 0)),
                  pl.BlockSpec((g * c, tn), lambda j, i: (0, j))],
        out_specs=pl.BlockSpec((tm, tn), lambda j, i: (i, j)),
        out_shape=jax.ShapeDtypeStruct((b * s, d), _CDT),
        compiler_params=_params("arbitrary", "arbitrary"),
        name=name,
    )(o, gate, w)


def _resid_ln(h, x, g, b, alpha, tm, with_cdt, name):
    m, d = x.shape

    def body(h_ref, x_ref, g_ref, b_ref, *outs):
        y = alpha * x_ref[...] + h_ref[...].astype(jnp.float32)
        mu = jnp.mean(y, axis=-1, keepdims=True)
        yc = y - mu
        var = jnp.mean(yc * yc, axis=-1, keepdims=True)
        z = yc * lax.rsqrt(var + _LN_EPS) * g_ref[...] + b_ref[...]
        outs[0][...] = z
        if with_cdt:
            outs[1][...] = z.astype(_CDT)

    row = pl.BlockSpec((tm, d), lambda i: (i, 0))
    vec = pl.BlockSpec((1, d), lambda i: (0, 0))
    out_shape = [jax.ShapeDtypeStruct((m, d), jnp.float32)]
    out_specs = [row]
    if with_cdt:
        out_shape.append(jax.ShapeDtypeStruct((m, d), _CDT))
        out_specs.append(row)
    return pl.pallas_call(
        body,
        grid=(m // tm,),
        in_specs=[row, row, vec, vec],
        out_specs=out_specs,
        out_shape=out_shape,
        compiler_params=_params("arbitrary"),
        name=name,
    )(h, x, g.reshape(1, d), b.reshape(1, d))


def _col_reduce(x, op):
    rows, n = x.shape
    fold = 8 * _SUBLANES
    if rows > fold and rows % fold == 0:
        x = op(x.reshape(rows // fold, fold, n), axis=0)
    return op(x, axis=0, keepdims=True)


def _softmax_step(sc, tile_max, m_ref, cols):
    m_old = m_ref[:, cols]
    m_new = jnp.maximum(m_old, tile_max)
    m_ref[:, cols] = m_new
    return jnp.exp2(sc - m_new).astype(_CDT), jnp.exp2(m_old - m_new)


def _pv_step(vta, p, alpha, acc_ref, cols):
    pv = jnp.dot(vta, p, preferred_element_type=jnp.float32)
    acc_ref[:, cols] = pv if alpha is None else alpha * acc_ref[:, cols] + pv


def _float_sort_key(x):
    bits = pltpu.bitcast(x, jnp.int32)
    return bits ^ ((bits >> 31) & jnp.int32(0x7FFFFFFF))


def _dsa_core(q, qi, wt, kidx, k, vt, *, topk, tk1, tk, hchunk):
    b, n_heads, s, _ = q.shape
    idx_heads = qi.shape[1]
    n_kv = k.shape[1]
    tq = _LANES
    gq = _GROUP * tq
    log_s = int(math.log2(s))
    assert 1 << log_s == s and s % tk == 0 and tk % tk1 == 0 and tk1 % tq == 0
    assert idx_heads % hchunk == 0 and gq % _MXU_DIM == 0

    def body(q_ref, qi_ref, wt_ref, kidx_ref, k_ref, vt_ref, o_ref,
             key_ref, mask_ref, sc_ref, p_ref, al_ref, mx_ref, m_ref, acc_ref):
        i = pl.program_id(1)
        n1 = (i * tq + tq + tk1 - 1) // tk1
        n3 = (i * tq + tq + tk - 1) // tk

        q_pos1 = i * tq + lax.broadcasted_iota(jnp.int32, (tk1, tq), 1)
        s_iota1 = lax.broadcasted_iota(jnp.int32, (tk1, tq), 0)

        def score_tile(kt, carry):
            ks = pl.multiple_of(kt * tk1, tk1)
            kx = kidx_ref[0, pl.ds(ks, tk1), :]
            acc = jnp.zeros((tk1, tq), jnp.float32)
            for hc in range(idx_heads // hchunk):
                qq = qi_ref[0, hc * hchunk:(hc + 1) * hchunk].reshape(hchunk * tq, _HEAD_DIM)
                d = lax.dot_general(kx, qq, _NT, preferred_element_type=jnp.float32)
                for hh in range(hchunk):
                    h = hc * hchunk + hh
                    acc = acc + jnp.maximum(d[:, hh * tq:(hh + 1) * tq], 0.0) * wt_ref[0, h:h + 1, :]
            key = _float_sort_key(acc + 0.0)
            key_ref[pl.ds(ks, tk1), :] = jnp.where(ks + s_iota1 <= q_pos1, key, _INT_MIN)
            return carry

        lax.fori_loop(0, n1, score_tile, 0)

        def pad_tile(kt, carry):
            key_ref[pl.ds(pl.multiple_of(kt * tk1, tk1), tk1), :] = jnp.full((tk1, tq), _INT_MIN, jnp.int32)
            return carry

        lax.fori_loop(n1, n3 * (tk // tk1), pad_tile, 0)

        q_pos = i * tq + lax.broadcasted_iota(jnp.int32, (tk, tq), 1)
        s_iota = lax.broadcasted_iota(jnp.int32, (tk, tq), 0)

        fold = 8 * _SUBLANES

        def select(n_tiles):
            def count(pred_fn):
                cnt = jnp.zeros((fold, tq), jnp.float32)
                for kt in range(n_tiles):
                    c = jnp.where(pred_fn(key_ref[kt * tk:(kt + 1) * tk, :], kt * tk), 1.0, 0.0)
                    cnt = cnt + jnp.sum(c.reshape(tk // fold, fold, tq), axis=0)
                return jnp.sum(cnt, axis=0, keepdims=True)

            def value_step(t, prefix):
                cand = prefix | jnp.left_shift(jnp.int32(1), 31 - t)
                cand_s = cand ^ _INT_MIN
                cnt = count(lambda key, ks: key >= cand_s)
                return jnp.where(cnt >= topk, cand, prefix)

            prefix = lax.fori_loop(0, 32, value_step, jnp.zeros((1, tq), jnp.int32))
            thr = prefix ^ _INT_MIN

            def tie_break():
                need = topk - count(lambda key, ks: key > thr)

                def index_step(t, j):
                    cand = j | jnp.left_shift(jnp.int32(1), log_s - 1 - t)
                    cnt = count(lambda key, ks: (key == thr) & (ks + s_iota < cand))
                    return jnp.where(cnt < need, cand, j)

                return lax.fori_loop(0, log_s, index_step, jnp.zeros((1, tq), jnp.int32))

            n_ge = count(lambda key, ks: key >= thr)
            j_last = lax.cond(jnp.max(n_ge) > topk, tie_break,
                              lambda: jnp.full((1, tq), s, jnp.int32))

            for kt in range(n_tiles):
                key = key_ref[kt * tk:(kt + 1) * tk, :]
                s_pos = kt * tk + s_iota
                sel = (key > thr) | ((key == thr) & (s_pos <= j_last))
                mask_ref[kt * tk:(kt + 1) * tk, :] = jnp.where(
                    sel & (s_pos <= q_pos), 0.0, _MASKED).astype(_CDT)

        for n_tiles in range(1, s // tk + 1):
            pl.when(n3 == n_tiles)(functools.partial(select, n_tiles))

        r = lax.broadcasted_iota(jnp.int32, (gq, tq), 0)
        c = lax.broadcasted_iota(jnp.int32, (gq, tq), 1)
        one_hot = jnp.where((r & (tq - 1)) == c, 1.0, 0.0).astype(_CDT)

        chunks = [slice(cc * _MXU_DIM, (cc + 1) * _MXU_DIM) for cc in range(gq // _MXU_DIM)]

        def kv_group(g, carry):
            qg = q_ref[0, pl.ds(g * _GROUP, _GROUP)].reshape(gq, _HEAD_DIM)
            qaug = jnp.concatenate([qg, one_hot], axis=1)
            m_ref[...] = jnp.full((1, gq), _M_INIT, jnp.float32)

            def logits(kt):
                kaug = jnp.concatenate([k_ref[0, g, pl.ds(kt * tk, tk), :],
                                        mask_ref[pl.ds(kt * tk, tk), :]], axis=1)
                for cols in chunks:
                    sc = lax.dot_general(kaug, qaug[cols], _NT, preferred_element_type=jnp.float32)
                    sc_ref[kt & 1, :, cols] = sc
                    mx_ref[kt & 1, :, cols] = _col_reduce(sc, jnp.max)

            def softmax(kt):
                for cols in chunks:
                    p_ref[kt & 1, :, cols], al_ref[kt & 1, :, cols] = _softmax_step(
                        sc_ref[kt & 1, :, cols], mx_ref[kt & 1, :, cols], m_ref, cols)

            def pv(kt):
                vta = vt_ref[0, g, kt]
                for cols in chunks:
                    _pv_step(vta, p_ref[kt & 1, :, cols], None if kt == 0 else al_ref[kt & 1, :, cols],
                             acc_ref, cols)

            for n_tiles in range(1, s // tk + 1):
                @pl.when(n3 == n_tiles)
                def _():
                    for step in range(n_tiles + 2):
                        if step < n_tiles:
                            logits(step)
                        if 1 <= step <= n_tiles:
                            softmax(step - 1)
                        if step >= 2:
                            pv(step - 2)

            acc = acc_ref[...]
            ot = acc[:_HEAD_DIM] * (1.0 / acc[_HEAD_DIM:_HEAD_DIM + 1])
            for hh in range(_GROUP):
                o_ref[0, g, :, hh * _HEAD_DIM:(hh + 1) * _HEAD_DIM] = (
                    ot[:, hh * tq:(hh + 1) * tq].T.astype(o_ref.dtype))
            return carry

        lax.fori_loop(0, n_kv, kv_group, 0)

    nq = s // tq
    return pl.pallas_call(
        body,
        grid=(b, nq),
        in_specs=[pl.BlockSpec((1, n_heads, tq, _HEAD_DIM), lambda bi, i: (bi, 0, i, 0)),
                  pl.BlockSpec((1, idx_heads, tq, _HEAD_DIM), lambda bi, i: (bi, 0, i, 0)),
                  pl.BlockSpec((1, idx_heads, tq), lambda bi, i: (bi, 0, i)),
                  pl.BlockSpec((1, s, _HEAD_DIM), lambda bi, i: (bi, 0, 0)),
                  pl.BlockSpec((1, n_kv, s, _HEAD_DIM), lambda bi, i: (bi, 0, 0, 0)),
                  pl.BlockSpec((1, n_kv, s // tk, _VT_ROWS, tk), lambda bi, i: (bi, 0, 0, 0, 0))],
        out_specs=pl.BlockSpec((1, n_kv, tq, _GROUP * _HEAD_DIM), lambda bi, i: (bi, 0, i, 0)),
        out_shape=jax.ShapeDtypeStruct((b, n_kv, s, _GROUP * _HEAD_DIM), _CDT),
        scratch_shapes=[pltpu.VMEM((s, tq), jnp.int32),
                        pltpu.VMEM((s, tq), _CDT),
                        pltpu.VMEM((2, tk, gq), jnp.float32),
                        pltpu.VMEM((2, tk, gq), _CDT),
                        pltpu.VMEM((2, 1, gq), jnp.float32),
                        pltpu.VMEM((2, 1, gq), jnp.float32),
                        pltpu.VMEM((1, gq), jnp.float32),
                        pltpu.VMEM((_VT_ROWS, gq), jnp.float32)],
        compiler_params=_params("arbitrary", "arbitrary"),
        name="dsa_core",
    )(q, qi, wt, kidx, k, vt)


def _cumsum_aug(lf, n_heads):
    b, s, w = lf.shape
    blk = _LANES

    def body(lf_ref, out_ref, carry_ref):
        @pl.when(pl.program_id(1) == 0)
        def _():
            carry_ref[...] = jnp.zeros_like(carry_ref)
        r = lax.broadcasted_iota(jnp.int32, (blk, blk), 0)
        c = lax.broadcasted_iota(jnp.int32, (blk, blk), 1)
        tri = jnp.where(r >= c, 1.0, 0.0).astype(jnp.float32)
        cs = jnp.dot(tri, lf_ref[0], preferred_element_type=jnp.float32,
                     precision=lax.Precision.HIGHEST) + carry_ref[...]
        carry_ref[...] = cs[blk - 1:blk, :]
        c2 = cs * _LOG2E
        for h in range(n_heads):
            col = c2[:, h:h + 1]
            hi = col.astype(_CDT).astype(jnp.float32)
            mid = (col - hi).astype(_CDT).astype(jnp.float32)
            lo = (col - hi - mid).astype(_CDT).astype(jnp.float32)
            tile = jnp.where(c == 0, hi, jnp.where(c == 1, mid, jnp.where(c == 2, lo, 0.0)))
            out_ref[0, h] = tile.astype(out_ref.dtype)

    return pl.pallas_call(
        body,
        grid=(b, s // blk),
        in_specs=[pl.BlockSpec((1, blk, w), lambda bi, i: (bi, i, 0))],
        out_specs=pl.BlockSpec((1, n_heads, blk, _LANES), lambda bi, i: (bi, 0, i, 0)),
        out_shape=jax.ShapeDtypeStruct((b, n_heads, s, _LANES), _CDT),
        scratch_shapes=[pltpu.VMEM((1, w), jnp.float32)],
        compiler_params=_params("arbitrary", "arbitrary"),
        name="fox_cumsum",
    )(lf)


def _fox_core(q, k, caug, vt, *, t):
    b, n_heads, s, _ = q.shape
    assert t % _MXU_DIM == 0 and s % t == 0

    chunks = [slice(cc * _MXU_DIM, (cc + 1) * _MXU_DIM) for cc in range(t // _MXU_DIM)]

    tiles = [(i, j) for i in range(s // t) for j in range(i + 1)]

    def visible(i, j, cols):
        return min(t, cols.stop) if i == j else t

    def body(q_ref, k_ref, c_ref, vt_ref, o_ref, kaug_ref, sc_ref, p_ref, al_ref, mx_ref, m_ref, acc_ref):
        kaug_ref[:, :_HEAD_DIM] = k_ref[0, 0]
        kaug_ref[:, _HEAD_DIM:] = c_ref[0, 0]
        lane = lax.broadcasted_iota(jnp.int32, (t, _HEAD_DIM), 1)
        minus_ones = jnp.where(lane < 3, -1.0, 0.0).astype(_CDT)

        def logits(n):
            i, j = tiles[n]
            qaug = jnp.concatenate([q_ref[0, 0, i * t:(i + 1) * t, :], minus_ones], axis=1)
            ka = kaug_ref[j * t:(j + 1) * t, :]
            for cols in chunks:
                kr = visible(i, j, cols)
                sc = lax.dot_general(ka[:kr], qaug[cols], _NT, preferred_element_type=jnp.float32)
                if i == j:
                    key_i = lax.broadcasted_iota(jnp.int32, (kr, _MXU_DIM), 0)
                    qry_i = cols.start + lax.broadcasted_iota(jnp.int32, (kr, _MXU_DIM), 1)
                    sc = jnp.where(key_i <= qry_i, sc, -jnp.inf)
                sc_ref[n & 1, :kr, cols] = sc
                mx_ref[n & 1, :, cols] = _col_reduce(sc, jnp.max)

        def softmax(n):
            i, j = tiles[n]
            if j == 0:
                m_ref[...] = jnp.full((1, t), _M_INIT, jnp.float32)
            for cols in chunks:
                kr = visible(i, j, cols)
                p_ref[n & 1, :kr, cols], al_ref[n & 1, :, cols] = _softmax_step(
                    sc_ref[n & 1, :kr, cols], mx_ref[n & 1, :, cols], m_ref, cols)

        def pv(n):
            i, j = tiles[n]
            vta = vt_ref[0, 0, j]
            for cols in chunks:
                kr = visible(i, j, cols)
                _pv_step(vta[:, :kr], p_ref[n & 1, :kr, cols],
                         None if j == 0 else al_ref[n & 1, :, cols], acc_ref, cols)
            if j == i:
                acc = acc_ref[...]
                ot = acc[:_HEAD_DIM] * (1.0 / acc[_HEAD_DIM:_HEAD_DIM + 1])
                for cq in range(t // _LANES):
                    o_ref[0, i * t + cq * _LANES:i * t + (cq + 1) * _LANES, :] = (
                        ot[:, cq * _LANES:(cq + 1) * _LANES].T.astype(o_ref.dtype))

        for step in range(len(tiles) + 2):
            if step < len(tiles):
                logits(step)
            if 1 <= step <= len(tiles):
                softmax(step - 1)
            if step >= 2:
                pv(step - 2)

    head = pl.BlockSpec((1, 1, s, _HEAD_DIM), lambda bi, h: (bi, h, 0, 0))
    return pl.pallas_call(
        body,
        grid=(b, n_heads),
        in_specs=[head, head, head,
                  pl.BlockSpec((1, 1, s // t, _VT_ROWS, t), lambda bi, h: (bi, h, 0, 0, 0))],
        out_specs=pl.BlockSpec((1, s, _HEAD_DIM), lambda bi, h: (bi, 0, h)),
        out_shape=jax.ShapeDtypeStruct((b, s, n_heads * _HEAD_DIM), _CDT),
        scratch_shapes=[pltpu.VMEM((s, 2 * _HEAD_DIM), _CDT),
                        pltpu.VMEM((2, t, t), jnp.float32),
                        pltpu.VMEM((2, t, t), _CDT),
                        pltpu.VMEM((2, 1, t), jnp.float32),
                        pltpu.VMEM((2, 1, t), jnp.float32),
                        pltpu.VMEM((1, t), jnp.float32),
                        pltpu.VMEM((_VT_ROWS, t), jnp.float32)],
        compiler_params=_params("arbitrary", "arbitrary"),
        name="fox_core",
    )(q, k, caug, vt)


def _rope_tables(s):
    half = _HEAD_DIM // 2
    inv = _ROPE_THETA ** (-jnp.arange(half, dtype=jnp.float32) / half)
    ang = jnp.arange(s, dtype=jnp.float32)[:, None] * inv[None, :]
    cos, sin = jnp.cos(ang), jnp.sin(ang)
    return jnp.concatenate([cos, cos], axis=1), jnp.concatenate([-sin, sin], axis=1)


def _placed(cols, s, tn_cap=None):
    tm, tn = _proj_tiles(s, cols)
    if tn_cap is not None:
        tn = min(tn, tn_cap)
    if not cols.aligned(tn):
        cols = cols.materialize()
        tm, tn = _proj_tiles(s, cols)
        if tn_cap is not None:
            tn = min(tn, tn_cap)
    return cols, tm, tn


def _heads_call(a, cols, b, s, *, rope, scale, cos, sin, name):
    cols, tm, tn = _placed(cols, s)
    nsb = s // tm
    extras, extra_specs = (), ()
    if rope:
        extras = (cos, sin)
        extra_specs = (pl.BlockSpec((tm, _HEAD_DIM), lambda j, i: (i % nsb, 0)),) * 2
    nh_t = tn // _HEAD_DIM
    return _proj(
        a, cols, functools.partial(_ep_heads, rope=rope, scale=scale), extras, extra_specs,
        jax.ShapeDtypeStruct((b, cols.n // _HEAD_DIM, s, _HEAD_DIM), _CDT),
        pl.BlockSpec((1, nh_t, tm, _HEAD_DIM), lambda j, i: (i // nsb, j, i % nsb, 0)),
        tm, tn, name, row_chunk=min(_ROW_CHUNK, tm))


def _vt_call(a, cols, b, s, t, name):
    cols, tm, tn = _placed(cols, s)
    nsb = s // tm
    nh_t = tn // _HEAD_DIM
    return _proj(
        a, cols, functools.partial(_ep_vt, t=t), (), (),
        jax.ShapeDtypeStruct((b, cols.n // _HEAD_DIM, s // t, _VT_ROWS, t), _CDT),
        pl.BlockSpec((1, nh_t, tm // t, _VT_ROWS, t), lambda j, i: (i // nsb, j, i % nsb, 0, 0)),
        tm, tn, name, row_chunk=t)


def _gate_call(a, cols, b, s, groups, name):
    c = cols.n // groups
    cols, tm, tn = _placed(cols, s, tn_cap=c)
    nsb = s // tm
    per = c // tn
    return _proj(
        a, cols, _ep_silu, (), (),
        jax.ShapeDtypeStruct((b, groups, s, c), _CDT),
        pl.BlockSpec((1, 1, tm, tn), lambda j, i: (i // nsb, j // per, i % nsb, j % per)),
        tm, tn, name, row_chunk=min(_ROW_CHUNK, tm))


def _dsa_layer(xc, w_in_t, q_norm_g, w_uq, kidx_g, kidx_b, w_out, layer, b, s):
    d = xc.shape[1]
    q_lora = w_uq.shape[1]
    branch = w_out.shape[1]
    idx_heads = (w_uq.shape[2] - branch) // _HEAD_DIM
    kv = (w_in_t.shape[1] - q_lora - _HEAD_DIM - idx_heads - branch) // 2
    w_in = functools.partial(_Cols, w_in_t, layer=layer, transposed=True)
    n_kv = kv // _HEAD_DIM
    n_heads = branch // _HEAD_DIM
    assert n_heads == n_kv * _GROUP and idx_heads <= _LANES
    topk = min(_TOPK_MAX, s // 4)
    o0 = q_lora
    o1 = o0 + 2 * kv
    o2 = o1 + _HEAD_DIM
    o3 = o2 + idx_heads
    cos, sin = _rope_tables(s)
    tk1 = min(256, s)
    tk = min(512, s)

    c_cq = w_in(0, q_lora).materialize()
    tm = min(512, s)
    cq = _proj(xc, c_cq, _ep_rms, (q_norm_g[layer].reshape(1, q_lora),),
               (pl.BlockSpec((1, q_lora), lambda j, i: (0, 0)),),
               jax.ShapeDtypeStruct((b * s, q_lora), _CDT),
               pl.BlockSpec((tm, q_lora), lambda j, i: (i, 0)), tm, q_lora, "dsa_cq")
    k = _heads_call(xc, w_in(o0, kv), b, s, rope=True, scale=1.0, cos=cos, sin=sin, name="dsa_k")
    vt = _vt_call(xc, w_in(o0 + kv, kv), b, s, tk, "dsa_v")

    c_ki = w_in(o1, 2 * _LANES)
    if not c_ki.aligned(2 * _LANES):
        w_ki = jnp.pad(w_in_t[layer][o1:o3, :], ((0, _LANES - idx_heads), (0, 0)))
        c_ki = _Cols(w_ki.astype(_CDT), transposed=True)
    nsb = s // tm
    rope_specs = (pl.BlockSpec((tm, _HEAD_DIM), lambda j, i: (i % nsb, 0)),) * 2
    w_scale = idx_heads ** -0.5 * _HEAD_DIM ** -0.5
    vec = pl.BlockSpec((1, _HEAD_DIM), lambda j, i: (0, 0))
    kidx, wt = _proj(xc, c_ki, functools.partial(_ep_kidx, idx_heads=idx_heads, w_scale=w_scale),
                     (cos, sin, kidx_g[layer].reshape(1, _HEAD_DIM), kidx_b[layer].reshape(1, _HEAD_DIM)),
                     rope_specs + (vec, vec),
                     [jax.ShapeDtypeStruct((b, s, _HEAD_DIM), _CDT),
                      jax.ShapeDtypeStruct((b, idx_heads, s), jnp.float32)],
                     [pl.BlockSpec((1, tm, _HEAD_DIM), lambda j, i: (i // nsb, i % nsb, 0)),
                      pl.BlockSpec((1, idx_heads, tm), lambda j, i: (i // nsb, 0, i % nsb))],
                     tm, 2 * _LANES, "dsa_kidx")
    gate = _gate_call(xc, w_in(o3, branch), b, s, n_kv, "dsa_gate")
    q = _heads_call(cq, _Cols(w_uq, 0, branch, layer), b, s, rope=True,
                    scale=_HEAD_DIM ** -0.5 * _LOG2E, cos=cos, sin=sin, name="dsa_q")
    qi = _heads_call(cq, _Cols(w_uq, branch, None, layer), b, s, rope=True, scale=1.0, cos=cos, sin=sin,
                     name="dsa_qi")
    o = _dsa_core(q, qi, wt, kidx, k, vt, topk=topk, tk1=tk1, tk=tk, hchunk=min(8, idx_heads))
    return _outproj(o, gate, w_out[layer].astype(_CDT), min(512, s), min(1024, d), "dsa_out")


def _fox_layer(xc, w_in_t, forget_bias, w_out, layer, b, s):
    d = xc.shape[1]
    branch = w_out.shape[1]
    n_heads = forget_bias.shape[1]
    assert branch == n_heads * _HEAD_DIM and n_heads <= _LANES
    w_in = functools.partial(_Cols, w_in_t, layer=layer, transposed=True)
    w_f = jnp.pad(w_in_t[layer][4 * branch:, :], ((0, _LANES - n_heads), (0, 0)))
    fb = jnp.pad(forget_bias[layer], (0, _LANES - n_heads)).reshape(1, _LANES)
    tm = min(512, s)
    nsb = s // tm
    t = min(512, s)
    q = _heads_call(xc, w_in(0, branch), b, s, rope=False,
                    scale=_HEAD_DIM ** -0.5 * _LOG2E, cos=None, sin=None, name="fox_q")
    k = _heads_call(xc, w_in(branch, branch), b, s, rope=False, scale=1.0,
                    cos=None, sin=None, name="fox_k")
    vt = _vt_call(xc, w_in(2 * branch, branch), b, s, t, "fox_v")
    gate = _gate_call(xc, w_in(3 * branch, branch), b, s, 1, "fox_gate")
    lf = _proj(xc, _Cols(w_f, transposed=True), _ep_logf, (fb,), (pl.BlockSpec((1, _LANES), lambda j, i: (0, 0)),),
               jax.ShapeDtypeStruct((b, s, _LANES), jnp.float32),
               pl.BlockSpec((1, tm, _LANES), lambda j, i: (i // nsb, i % nsb, 0)),
               tm, _LANES, "fox_logf")
    caug = _cumsum_aug(lf, n_heads)
    o = _fox_core(q, k, caug, vt, t=t)
    return _outproj(o.reshape(b, 1, s, branch), gate, w_out[layer].astype(_CDT), tm, min(1024, d),
                    "fox_out")


def kernel(x, a_w_in, a_q_norm_g, a_w_uq, a_kidx_norm_g, a_kidx_norm_b, a_w_out,
           b_w_in, b_forget_bias, b_w_out, ln_g, ln_b):
    b, s, d = x.shape
    depth = ln_g.shape[0]
    alpha = (2 * depth) ** 0.25
    xf = x.reshape(b * s, d)
    xc = xf.astype(_CDT)
    a_w_in_t = jnp.swapaxes(a_w_in, 1, 2)
    b_w_in_t = jnp.swapaxes(b_w_in, 1, 2)
    tm_ln = min(256, s)
    for i in range(depth):
        j = i // 2
        if i % 2 == 0:
            h = _dsa_layer(xc, a_w_in_t, a_q_norm_g, a_w_uq, a_kidx_norm_g, a_kidx_norm_b, a_w_out, j, b, s)
        else:
            h = _fox_layer(xc, b_w_in_t, b_forget_bias, b_w_out, j, b, s)
        last = i == depth - 1
        outs = _resid_ln(h, xf, ln_g[i], ln_b[i], alpha, tm_ln, not last, "resid_ln_%d" % i)
        xf = outs[0]
        if not last:
            xc = outs[1]
    return xf.reshape(b, s, d)
```

```python
import functools
import math

import jax
import jax.numpy as jnp
from jax import lax
from jax.experimental import pallas as pl
from jax.experimental.pallas import tpu as pltpu

_CDT = jnp.bfloat16
_HEAD_DIM = 128
_GROUP = 8
_TOPK_MAX = 256
_ROPE_THETA = 10000.0
_LN_EPS = 1e-5
_RMS_EPS = 1e-6
_LANES = 128
_SUBLANES = 8
_MXU_DIM = 256
_VMEM_LIMIT = 56 * 1024 * 1024
_PROJ_TILE_BUDGET = 40 * 1024 * 1024
_VT_ROWS = _HEAD_DIM + 16
_INT_MIN = -2 ** 31
_MASKED = -1e30
_M_INIT = -1e29
_LOG2E = math.log2(math.e)
_ROW_CHUNK = 256

_NT = (((1,), (1,)), ((), ()))


def _params(*sem):
    return pltpu.CompilerParams(dimension_semantics=sem, vmem_limit_bytes=_VMEM_LIMIT)


def _rope_tile(x, cos, sin_signed):
    return x * cos + pltpu.roll(x, _HEAD_DIM // 2, axis=1) * sin_signed


class _Cols:
    def __init__(self, w, col0=0, n=None, layer=None, transposed=False):
        self.w, self.col0, self.layer, self.transposed = w, col0, layer, transposed
        self.n_total = w.shape[-2] if transposed else w.shape[-1]
        self.k = w.shape[-1] if transposed else w.shape[-2]
        self.n = self.n_total - col0 if n is None else n

    def block_aligned(self, tn):
        return self.col0 % tn == 0

    def aligned(self, tn):
        start_ok = self.col0 % _SUBLANES == 0 if self.transposed else self.block_aligned(tn)
        return start_ok and self.col0 + -(-self.n // tn) * tn <= self.n_total

    def materialize(self):
        w = self.w if self.layer is None else self.w[self.layer]
        if self.transposed:
            return _Cols(w[self.col0:self.col0 + self.n, :].astype(_CDT), transposed=True)
        return _Cols(w[:, self.col0:self.col0 + self.n].astype(_CDT))


def _proj_tiles(s, cols):
    itemsize = cols.w.dtype.itemsize
    act = jnp.dtype(_CDT).itemsize
    for tn in (1024, 512, 256, 128):
        if tn > cols.n:
            continue
        w_bytes = cols.k * tn * (2 * itemsize + (act if cols.w.dtype != _CDT else 0))
        for tm in (1024, 512, 256):
            if tm <= s and w_bytes + 2 * tm * cols.k * act <= _PROJ_TILE_BUDGET:
                return tm, tn
    raise ValueError("projection tiles do not fit VMEM")


def _proj(a, cols, epilogue, extras, extra_specs, out_shape, out_specs, tm, tn, name, row_chunk=None):
    m, k = a.shape
    assert cols.aligned(tn) and k == cols.k
    convert = cols.w.dtype != _CDT
    n_extra = len(extras)
    w_block = (tn, k) if cols.transposed else (k, tn)

    def body(a_ref, w_ref, *rest):
        if convert:
            wc_ref = rest[-1]
            rest = rest[:-1]

            @pl.when(pl.program_id(1) == 0)
            def _():
                wc_ref[...] = w_ref[...].reshape(w_block).astype(_CDT)

            w = wc_ref[...]
        else:
            w = w_ref[...].reshape(w_block)
        rc = tm if row_chunk is None else row_chunk
        for r in range(tm // rc):
            rows = slice(r * rc, (r + 1) * rc)
            if cols.transposed:
                res = lax.dot_general(a_ref[rows, :], w, _NT, preferred_element_type=jnp.float32)
            else:
                res = jnp.dot(a_ref[rows, :], w, preferred_element_type=jnp.float32)
            epilogue(res, rest[:n_extra], rest[n_extra:], rows)

    lead_index = () if cols.layer is None else (cols.layer,)
    if cols.block_aligned(tn):
        c0 = cols.col0 // tn
        lead = () if cols.layer is None else (None,)
        if cols.transposed:
            w_spec = pl.BlockSpec(lead + (tn, k), lambda j, i: lead_index + (c0 + j, 0))
        else:
            w_spec = pl.BlockSpec(lead + (k, tn), lambda j, i: lead_index + (0, c0 + j))
    else:
        assert cols.transposed
        lead = () if cols.layer is None else (pl.Element(1),)
        w_spec = pl.BlockSpec(lead + (pl.Element(tn), pl.Element(k)),
                              lambda j, i: lead_index + (pl.multiple_of(cols.col0 + j * tn, _SUBLANES), 0))
    return pl.pallas_call(
        body,
        grid=(-(-cols.n // tn), m // tm),
        in_specs=[pl.BlockSpec((tm, k), lambda j, i: (i, 0)), w_spec] + list(extra_specs),
        out_specs=out_specs,
        out_shape=out_shape,
        scratch_shapes=[pltpu.VMEM(w_block, _CDT)] if convert else [],
        compiler_params=_params("arbitrary", "arbitrary"),
        name=name,
    )(a, cols.w, *extras)


def _ep_rms(res, extras, outs, rows):
    (g_ref,) = extras
    (o_ref,) = outs
    y = res * lax.rsqrt(jnp.mean(res * res, axis=-1, keepdims=True) + _RMS_EPS)
    o_ref[rows, :] = (y * g_ref[...]).astype(o_ref.dtype)


def _ep_heads(res, extras, outs, rows, *, rope, scale):
    (o_ref,) = outs
    nh = res.shape[1] // _HEAD_DIM
    if rope:
        cos = extras[0][rows, :]
        sin = extras[1][rows, :]
    for h in range(nh):
        xh = res[:, h * _HEAD_DIM:(h + 1) * _HEAD_DIM]
        if rope:
            xh = _rope_tile(xh, cos, sin)
        if scale != 1.0:
            xh = xh * scale
        o_ref[0, h, rows, :] = xh.astype(o_ref.dtype)


def _ep_vt(res, extras, outs, rows, *, t):
    (vt_ref,) = outs
    assert rows.start % t == 0 and res.shape[0] % t == 0
    for h in range(res.shape[1] // _HEAD_DIM):
        vh = res[:, h * _HEAD_DIM:(h + 1) * _HEAD_DIM]
        for c in range(res.shape[0] // t):
            dst = rows.start // t + c
            vt_ref[0, h, dst, :_HEAD_DIM, :] = vh[c * t:(c + 1) * t, :].T.astype(vt_ref.dtype)
            vt_ref[0, h, dst, _HEAD_DIM:, :] = jnp.ones((_VT_ROWS - _HEAD_DIM, t), vt_ref.dtype)


def _ep_kidx(res, extras, outs, rows, *, idx_heads, w_scale):
    cos_ref, sin_ref, g_ref, b_ref = extras
    kidx_ref, wt_ref = outs
    x = res[:, :_HEAD_DIM]
    mu = jnp.mean(x, axis=-1, keepdims=True)
    xc = x - mu
    var = jnp.mean(xc * xc, axis=-1, keepdims=True)
    y = xc * lax.rsqrt(var + _LN_EPS) * g_ref[...] + b_ref[...]
    kidx_ref[0] = _rope_tile(y, cos_ref[...], sin_ref[...]).astype(kidx_ref.dtype)
    wt = (res[:, _HEAD_DIM:] * w_scale).T
    wt_ref[0] = wt[:idx_heads, :]


def _ep_silu(res, extras, outs, rows):
    (o_ref,) = outs
    o_ref[0, 0, rows, :] = (res * jax.nn.sigmoid(res)).astype(o_ref.dtype)


def _ep_logf(res, extras, outs, rows):
    (bias_ref,) = extras
    (o_ref,) = outs
    z = res + bias_ref[...]
    o_ref[0] = jnp.minimum(z, 0.0) - jnp.log(1.0 + jnp.exp(-jnp.abs(z)))


def _outproj(o, gate, w, tm, tn, name):
    b, g, s, c = o.shape
    d = w.shape[1]
    nsb = s // tm

    def body(o_ref, g_ref, w_ref, out_ref):
        acc = None
        for gi in range(g):
            a = (o_ref[0, gi].astype(jnp.float32) * g_ref[0, gi].astype(jnp.float32)).astype(_CDT)
            part = jnp.dot(a, w_ref[gi * c:(gi + 1) * c, :], preferred_element_type=jnp.float32)
            acc = part if acc is None else acc + part
        out_ref[...] = acc.astype(out_ref.dtype)

    return pl.pallas_call(
        body,
        grid=(d // tn, b * nsb),
        in_specs=[pl.BlockSpec((1, g, tm, c), lambda j, i: (i // nsb, 0, i % nsb, 0)),
                  pl.BlockSpec((1, g, tm, c), lambda j, i: (i // nsb, 0, i % nsb, 0)),
                  pl.BlockSpec((g * c, tn), lambda j, i: (0, j))],
        out_specs=pl.BlockSpec((tm, tn), lambda j, i: (i, j)),
        out_shape=jax.ShapeDtypeStruct((b * s, d), _CDT),
        compiler_params=_params("arbitrary", "arbitrary"),
        name=name,
    )(o, gate, w)


def _resid_ln(h, x, g, b, alpha, tm, with_cdt, name):
    m, d = x.shape

    def body(h_ref, x_ref, g_ref, b_ref, *outs):
        y = alpha * x_ref[...] + h_ref[...].astype(jnp.float32)
        mu = jnp.mean(y, axis=-1, keepdims=True)
        yc = y - mu
        var = jnp.mean(yc * yc, axis=-1, keepdims=True)
        z = yc * lax.rsqrt(var + _LN_EPS) * g_ref[...] + b_ref[...]
        outs[0][...] = z
        if with_cdt:
            outs[1][...] = z.astype(_CDT)

    row = pl.BlockSpec((tm, d), lambda i: (i, 0))
    vec = pl.BlockSpec((1, d), lambda i: (0, 0))
    out_shape = [jax.ShapeDtypeStruct((m, d), jnp.float32)]
    out_specs = [row]
    if with_cdt:
        out_shape.append(jax.ShapeDtypeStruct((m, d), _CDT))
        out_specs.append(row)
    return pl.pallas_call(
        body,
        grid=(m // tm,),
        in_specs=[row, row, vec, vec],
        out_specs=out_specs,
        out_shape=out_shape,
        compiler_params=_params("arbitrary"),
        name=name,
    )(h, x, g.reshape(1, d), b.reshape(1, d))


def _col_reduce(x, op):
    rows, n = x.shape
    fold = 8 * _SUBLANES
    if rows > fold and rows % fold == 0:
        x = op(x.reshape(rows // fold, fold, n), axis=0)
    return op(x, axis=0, keepdims=True)


def _softmax_step(sc, tile_max, m_ref, cols):
    m_old = m_ref[:, cols]
    m_new = jnp.maximum(m_old, tile_max)
    m_ref[:, cols] = m_new
    return jnp.exp2(sc - m_new).astype(_CDT), jnp.exp2(m_old - m_new)


def _pv_step(vta, p, alpha, acc_ref, cols):
    pv = jnp.dot(vta, p, preferred_element_type=jnp.float32)
    acc_ref[:, cols] = pv if alpha is None else alpha * acc_ref[:, cols] + pv


def _float_sort_key(x):
    bits = pltpu.bitcast(x, jnp.int32)
    return bits ^ ((bits >> 31) & jnp.int32(0x7FFFFFFF))


def _dsa_core(q, qi, wt, kidx, k, vt, *, topk, tk1, tk, hchunk):
    b, n_heads, s, _ = q.shape
    idx_heads = qi.shape[1]
    n_kv = k.shape[1]
    tq = _LANES
    gq = _GROUP * tq
    log_s = int(math.log2(s))
    assert 1 << log_s == s and s % tk == 0 and tk % tk1 == 0 and tk1 % tq == 0
    assert idx_heads % hchunk == 0 and gq % _MXU_DIM == 0

    def body(q_ref, qi_ref, wt_ref, kidx_ref, k_ref, vt_ref, o_ref,
             key_ref, mask_ref, sc_ref, p_ref, al_ref, mx_ref, m_ref, acc_ref):
        i = pl.program_id(1)
        n1 = (i * tq + tq + tk1 - 1) // tk1
        n3 = (i * tq + tq + tk - 1) // tk

        q_pos1 = i * tq + lax.broadcasted_iota(jnp.int32, (tk1, tq), 1)
        s_iota1 = lax.broadcasted_iota(jnp.int32, (tk1, tq), 0)

        def score_tile(kt, carry):
            ks = pl.multiple_of(kt * tk1, tk1)
            kx = kidx_ref[0, pl.ds(ks, tk1), :]
            acc = jnp.zeros((tk1, tq), jnp.float32)
            for hc in range(idx_heads // hchunk):
                qq = qi_ref[0, hc * hchunk:(hc + 1) * hchunk].reshape(hchunk * tq, _HEAD_DIM)
                d = lax.dot_general(kx, qq, _NT, preferred_element_type=jnp.float32)
                for hh in range(hchunk):
                    h = hc * hchunk + hh
                    acc = acc + jnp.maximum(d[:, hh * tq:(hh + 1) * tq], 0.0) * wt_ref[0, h:h + 1, :]
            key = _float_sort_key(acc + 0.0)
            key_ref[pl.ds(ks, tk1), :] = jnp.where(ks + s_iota1 <= q_pos1, key, _INT_MIN)
            return carry

        lax.fori_loop(0, n1, score_tile, 0)

        def pad_tile(kt, carry):
            key_ref[pl.ds(pl.multiple_of(kt * tk1, tk1), tk1), :] = jnp.full((tk1, tq), _INT_MIN, jnp.int32)
            return carry

        lax.fori_loop(n1, n3 * (tk // tk1), pad_tile, 0)

        q_pos = i * tq + lax.broadcasted_iota(jnp.int32, (tk, tq), 1)
        s_iota = lax.broadcasted_iota(jnp.int32, (tk, tq), 0)

        fold = 8 * _SUBLANES

        def select(n_tiles):
            def count(pred_fn):
                cnt = jnp.zeros((fold, tq), jnp.float32)
                for kt in range(n_tiles):
                    c = jnp.where(pred_fn(key_ref[kt * tk:(kt + 1) * tk, :], kt * tk), 1.0, 0.0)
                    cnt = cnt + jnp.sum(c.reshape(tk // fold, fold, tq), axis=0)
                return jnp.sum(cnt, axis=0, keepdims=True)

            def value_step(t, prefix):
                cand = prefix | jnp.left_shift(jnp.int32(1), 31 - t)
                cand_s = cand ^ _INT_MIN
                cnt = count(lambda key, ks: key >= cand_s)
                return jnp.where(cnt >= topk, cand, prefix)

            prefix = lax.fori_loop(0, 32, value_step, jnp.zeros((1, tq), jnp.int32))
            thr = prefix ^ _INT_MIN

            def tie_break():
                need = topk - count(lambda key, ks: key > thr)

                def index_step(t, j):
                    cand = j | jnp.left_shift(jnp.int32(1), log_s - 1 - t)
                    cnt = count(lambda key, ks: (key == thr) & (ks + s_iota < cand))
                    return jnp.where(cnt < need, cand, j)

                return lax.fori_loop(0, log_s, index_step, jnp.zeros((1, tq), jnp.int32))

            n_ge = count(lambda key, ks: key >= thr)
            j_last = lax.cond(jnp.max(n_ge) > topk, tie_break,
                              lambda: jnp.full((1, tq), s, jnp.int32))

            for kt in range(n_tiles):
                key = key_ref[kt * tk:(kt + 1) * tk, :]
                s_pos = kt * tk + s_iota
                sel = (key > thr) | ((key == thr) & (s_pos <= j_last))
                mask_ref[kt * tk:(kt + 1) * tk, :] = jnp.where(
                    sel & (s_pos <= q_pos), 0.0, _MASKED).astype(_CDT)

        for n_tiles in range(1, s // tk + 1):
            pl.when(n3 == n_tiles)(functools.partial(select, n_tiles))

        r = lax.broadcasted_iota(jnp.int32, (gq, tq), 0)
        c = lax.broadcasted_iota(jnp.int32, (gq, tq), 1)
        one_hot = jnp.where((r & (tq - 1)) == c, 1.0, 0.0).astype(_CDT)

        chunks = [slice(cc * _MXU_DIM, (cc + 1) * _MXU_DIM) for cc in range(gq // _MXU_DIM)]

        def kv_group(g, carry):
            qg = q_ref[0, pl.ds(g * _GROUP, _GROUP)].reshape(gq, _HEAD_DIM)
            qaug = jnp.concatenate([qg, one_hot], axis=1)
            m_ref[...] = jnp.full((1, gq), _M_INIT, jnp.float32)

            def logits(kt):
                kaug = jnp.concatenate([k_ref[0, g, pl.ds(kt * tk, tk), :],
                                        mask_ref[pl.ds(kt * tk, tk), :]], axis=1)
                for cols in chunks:
                    sc = lax.dot_general(kaug, qaug[cols], _NT, preferred_element_type=jnp.float32)
                    sc_ref[kt & 1, :, cols] = sc
                    mx_ref[kt & 1, :, cols] = _col_reduce(sc, jnp.max)

            def softmax(kt):
                for cols in chunks:
                    p_ref[kt & 1, :, cols], al_ref[kt & 1, :, cols] = _softmax_step(
                        sc_ref[kt & 1, :, cols], mx_ref[kt & 1, :, cols], m_ref, cols)

            def pv(kt):
                vta = vt_ref[0, g, kt]
                for cols in chunks:
                    _pv_step(vta, p_ref[kt & 1, :, cols], None if kt == 0 else al_ref[kt & 1, :, cols],
                             acc_ref, cols)

            for n_tiles in range(1, s // tk + 1):
                @pl.when(n3 == n_tiles)
                def _():
                    for step in range(n_tiles + 2):
                        if step < n_tiles:
                            logits(step)
                        if 1 <= step <= n_tiles:
                            softmax(step - 1)
                        if step >= 2:
                            pv(step - 2)

            acc = acc_ref[...]
            ot = acc[:_HEAD_DIM] * (1.0 / acc[_HEAD_DIM:_HEAD_DIM + 1])
            for hh in range(_GROUP):
                o_ref[0, g, :, hh * _HEAD_DIM:(hh + 1) * _HEAD_DIM] = (
                    ot[:, hh * tq:(hh + 1) * tq].T.astype(o_ref.dtype))
            return carry

        lax.fori_loop(0, n_kv, kv_group, 0)

    nq = s // tq
    return pl.pallas_call(
        body,
        grid=(b, nq),
        in_specs=[pl.BlockSpec((1, n_heads, tq, _HEAD_DIM), lambda bi, i: (bi, 0, i, 0)),
                  pl.BlockSpec((1, idx_heads, tq, _HEAD_DIM), lambda bi, i: (bi, 0, i, 0)),
                  pl.BlockSpec((1, idx_heads, tq), lambda bi, i: (bi, 0, i)),
                  pl.BlockSpec((1, s, _HEAD_DIM), lambda bi, i: (bi, 0, 0)),
                  pl.BlockSpec((1, n_kv, s, _HEAD_DIM), lambda bi, i: (bi, 0, 0, 0)),
                  pl.BlockSpec((1, n_kv, s // tk, _VT_ROWS, tk), lambda bi, i: (bi, 0, 0, 0, 0))],
        out_specs=pl.BlockSpec((1, n_kv, tq, _GROUP * _HEAD_DIM), lambda bi, i: (bi, 0, i, 0)),
        out_shape=jax.ShapeDtypeStruct((b, n_kv, s, _GROUP * _HEAD_DIM), _CDT),
        scratch_shapes=[pltpu.VMEM((s, tq), jnp.int32),
                        pltpu.VMEM((s, tq), _CDT),
                        pltpu.VMEM((2, tk, gq), jnp.float32),
                        pltpu.VMEM((2, tk, gq), _CDT),
                        pltpu.VMEM((2, 1, gq), jnp.float32),
                        pltpu.VMEM((2, 1, gq), jnp.float32),
                        pltpu.VMEM((1, gq), jnp.float32),
                        pltpu.VMEM((_VT_ROWS, gq), jnp.float32)],
        compiler_params=_params("arbitrary", "arbitrary"),
        name="dsa_core",
    )(q, qi, wt, kidx, k, vt)


def _cumsum_aug(lf, n_heads):
    b, s, w = lf.shape
    blk = _LANES
    rows = jnp.arange(3 * w)[:, None]
    cols = jnp.arange(n_heads * _LANES)[None, :]
    place = ((rows % w == cols // _LANES) & (rows // w == cols % _LANES)).astype(_CDT)

    def body(lf_ref, place_ref, out_ref, carry_ref):
        @pl.when(pl.program_id(1) == 0)
        def _():
            carry_ref[...] = jnp.zeros_like(carry_ref)
        r = lax.broadcasted_iota(jnp.int32, (blk, blk), 0)
        c = lax.broadcasted_iota(jnp.int32, (blk, blk), 1)
        tri = jnp.where(r >= c, 1.0, 0.0).astype(_CDT)
        x = lf_ref[0]
        hi = x.astype(_CDT)
        mid = (x - hi.astype(jnp.float32)).astype(_CDT)
        lo = (x - hi.astype(jnp.float32) - mid.astype(jnp.float32)).astype(_CDT)
        cs = jnp.dot(tri, lo, preferred_element_type=jnp.float32)
        for part in (mid, hi):
            cs = cs + jnp.dot(tri, part, preferred_element_type=jnp.float32)
        cs = cs + carry_ref[...]
        carry_ref[...] = cs[blk - 1:blk, :]
        c2 = cs * _LOG2E
        hi = c2.astype(_CDT)
        mid = (c2 - hi.astype(jnp.float32)).astype(_CDT)
        lo = (c2 - hi.astype(jnp.float32) - mid.astype(jnp.float32)).astype(_CDT)
        aug = jnp.dot(jnp.concatenate([hi, mid, lo], axis=1), place_ref[...],
                      preferred_element_type=jnp.float32)
        for h in range(n_heads):
            out_ref[0, h] = aug[:, h * _LANES:(h + 1) * _LANES].astype(out_ref.dtype)

    return pl.pallas_call(
        body,
        grid=(b, s // blk),
        in_specs=[pl.BlockSpec((1, blk, w), lambda bi, i: (bi, i, 0)),
                  pl.BlockSpec((3 * w, n_heads * _LANES), lambda bi, i: (0, 0))],
        out_specs=pl.BlockSpec((1, n_heads, blk, _LANES), lambda bi, i: (bi, 0, i, 0)),
        out_shape=jax.ShapeDtypeStruct((b, n_heads, s, _LANES), _CDT),
        scratch_shapes=[pltpu.VMEM((1, w), jnp.float32)],
        compiler_params=_params("arbitrary", "arbitrary"),
        name="fox_cumsum",
    )(lf, place)


def _fox_core(q, k, caug, vt, *, t):
    b, n_heads, s, _ = q.shape
    assert t % _MXU_DIM == 0 and s % t == 0

    chunks = [slice(cc * _MXU_DIM, (cc + 1) * _MXU_DIM) for cc in range(t // _MXU_DIM)]

    tiles = [(i, j) for i in range(s // t) for j in range(i + 1)]

    def visible(i, j, cols):
        return min(t, cols.stop) if i == j else t

    def body(q_ref, k_ref, c_ref, vt_ref, o_ref, kaug_ref, sc_ref, p_ref, al_ref, mx_ref, m_ref, acc_ref):
        kaug_ref[:, :_HEAD_DIM] = k_ref[0, 0]
        kaug_ref[:, _HEAD_DIM:] = c_ref[0, 0]
        lane = lax.broadcasted_iota(jnp.int32, (t, _HEAD_DIM), 1)
        minus_ones = jnp.where(lane < 3, -1.0, 0.0).astype(_CDT)

        def logits(n):
            i, j = tiles[n]
            qaug = jnp.concatenate([q_ref[0, 0, i * t:(i + 1) * t, :], minus_ones], axis=1)
            ka = kaug_ref[j * t:(j + 1) * t, :]
            for cols in chunks:
                kr = visible(i, j, cols)
                sc = lax.dot_general(ka[:kr], qaug[cols], _NT, preferred_element_type=jnp.float32)
                if i == j:
                    key_i = lax.broadcasted_iota(jnp.int32, (kr, _MXU_DIM), 0)
                    qry_i = cols.start + lax.broadcasted_iota(jnp.int32, (kr, _MXU_DIM), 1)
                    sc = jnp.where(key_i <= qry_i, sc, -jnp.inf)
                sc_ref[n & 1, :kr, cols] = sc
                mx_ref[n & 1, :, cols] = _col_reduce(sc, jnp.max)

        def softmax(n):
            i, j = tiles[n]
            if j == 0:
                m_ref[...] = jnp.full((1, t), _M_INIT, jnp.float32)
            for cols in chunks:
                kr = visible(i, j, cols)
                p_ref[n & 1, :kr, cols], al_ref[n & 1, :, cols] = _softmax_step(
                    sc_ref[n & 1, :kr, cols], mx_ref[n & 1, :, cols], m_ref, cols)

        def pv(n):
            i, j = tiles[n]
            vta = vt_ref[0, 0, j]
            for cols in chunks:
                kr = visible(i, j, cols)
                _pv_step(vta[:, :kr], p_ref[n & 1, :kr, cols],
                         None if j == 0 else al_ref[n & 1, :, cols], acc_ref, cols)
            if j == i:
                acc = acc_ref[...]
                ot = acc[:_HEAD_DIM] * (1.0 / acc[_HEAD_DIM:_HEAD_DIM + 1])
                for cq in range(t // _LANES):
                    o_ref[0, i * t + cq * _LANES:i * t + (cq + 1) * _LANES, :] = (
                        ot[:, cq * _LANES:(cq + 1) * _LANES].T.astype(o_ref.dtype))

        for step in range(len(tiles) + 2):
            if step >= 2:
                pv(step - 2)
            if 1 <= step <= len(tiles):
                softmax(step - 1)
            if step < len(tiles):
                logits(step)

    head = pl.BlockSpec((1, 1, s, _HEAD_DIM), lambda bi, h: (bi, h, 0, 0))
    return pl.pallas_call(
        body,
        grid=(b, n_heads),
        in_specs=[head, head, head,
                  pl.BlockSpec((1, 1, s // t, _VT_ROWS, t), lambda bi, h: (bi, h, 0, 0, 0))],
        out_specs=pl.BlockSpec((1, s, _HEAD_DIM), lambda bi, h: (bi, 0, h)),
        out_shape=jax.ShapeDtypeStruct((b, s, n_heads * _HEAD_DIM), _CDT),
        scratch_shapes=[pltpu.VMEM((s, 2 * _HEAD_DIM), _CDT),
                        pltpu.VMEM((2, t, t), jnp.float32),
                        pltpu.VMEM((2, t, t), _CDT),
                        pltpu.VMEM((2, 1, t), jnp.float32),
                        pltpu.VMEM((2, 1, t), jnp.float32),
                        pltpu.VMEM((1, t), jnp.float32),
                        pltpu.VMEM((_VT_ROWS, t), jnp.float32)],
        compiler_params=_params("arbitrary", "arbitrary"),
        name="fox_core",
    )(q, k, caug, vt)


def _rope_tables(s):
    half = _HEAD_DIM // 2
    inv = _ROPE_THETA ** (-jnp.arange(half, dtype=jnp.float32) / half)
    ang = jnp.arange(s, dtype=jnp.float32)[:, None] * inv[None, :]
    cos, sin = jnp.cos(ang), jnp.sin(ang)
    return jnp.concatenate([cos, cos], axis=1), jnp.concatenate([-sin, sin], axis=1)


def _placed(cols, s, tn_cap=None):
    tm, tn = _proj_tiles(s, cols)
    if tn_cap is not None:
        tn = min(tn, tn_cap)
    if not cols.aligned(tn):
        cols = cols.materialize()
        tm, tn = _proj_tiles(s, cols)
        if tn_cap is not None:
            tn = min(tn, tn_cap)
    return cols, tm, tn


def _heads_call(a, cols, b, s, *, rope, scale, cos, sin, name):
    cols, tm, tn = _placed(cols, s)
    nsb = s // tm
    extras, extra_specs = (), ()
    if rope:
        extras = (cos, sin)
        extra_specs = (pl.BlockSpec((tm, _HEAD_DIM), lambda j, i: (i % nsb, 0)),) * 2
    nh_t = tn // _HEAD_DIM
    return _proj(
        a, cols, functools.partial(_ep_heads, rope=rope, scale=scale), extras, extra_specs,
        jax.ShapeDtypeStruct((b, cols.n // _HEAD_DIM, s, _HEAD_DIM), _CDT),
        pl.BlockSpec((1, nh_t, tm, _HEAD_DIM), lambda j, i: (i // nsb, j, i % nsb, 0)),
        tm, tn, name, row_chunk=min(_ROW_CHUNK, tm))


def _vt_call(a, cols, b, s, t, name):
    cols, tm, tn = _placed(cols, s)
    nsb = s // tm
    nh_t = tn // _HEAD_DIM
    return _proj(
        a, cols, functools.partial(_ep_vt, t=t), (), (),
        jax.ShapeDtypeStruct((b, cols.n // _HEAD_DIM, s // t, _VT_ROWS, t), _CDT),
        pl.BlockSpec((1, nh_t, tm // t, _VT_ROWS, t), lambda j, i: (i // nsb, j, i % nsb, 0, 0)),
        tm, tn, name, row_chunk=t)


def _gate_call(a, cols, b, s, groups, name):
    c = cols.n // groups
    cols, tm, tn = _placed(cols, s, tn_cap=c)
    nsb = s // tm
    per = c // tn
    return _proj(
        a, cols, _ep_silu, (), (),
        jax.ShapeDtypeStruct((b, groups, s, c), _CDT),
        pl.BlockSpec((1, 1, tm, tn), lambda j, i: (i // nsb, j // per, i % nsb, j % per)),
        tm, tn, name, row_chunk=min(_ROW_CHUNK, tm))


def _dsa_layer(xc, w_in_t, q_norm_g, w_uq, kidx_g, kidx_b, w_out, layer, b, s):
    d = xc.shape[1]
    q_lora = w_uq.shape[1]
    branch = w_out.shape[1]
    idx_heads = (w_uq.shape[2] - branch) // _HEAD_DIM
    kv = (w_in_t.shape[1] - q_lora - _HEAD_DIM - idx_heads - branch) // 2
    w_in = functools.partial(_Cols, w_in_t, layer=layer, transposed=True)
    n_kv = kv // _HEAD_DIM
    n_heads = branch // _HEAD_DIM
    assert n_heads == n_kv * _GROUP and idx_heads <= _LANES
    topk = min(_TOPK_MAX, s // 4)
    o0 = q_lora
    o1 = o0 + 2 * kv
    o2 = o1 + _HEAD_DIM
    o3 = o2 + idx_heads
    cos, sin = _rope_tables(s)
    tk1 = min(256, s)
    tk = min(512, s)

    c_cq = w_in(0, q_lora).materialize()
    tm = min(512, s)
    cq = _proj(xc, c_cq, _ep_rms, (q_norm_g[layer].reshape(1, q_lora),),
               (pl.BlockSpec((1, q_lora), lambda j, i: (0, 0)),),
               jax.ShapeDtypeStruct((b * s, q_lora), _CDT),
               pl.BlockSpec((tm, q_lora), lambda j, i: (i, 0)), tm, q_lora, "dsa_cq")
    k = _heads_call(xc, w_in(o0, kv), b, s, rope=True, scale=1.0, cos=cos, sin=sin, name="dsa_k")
    vt = _vt_call(xc, w_in(o0 + kv, kv), b, s, tk, "dsa_v")

    c_ki = w_in(o1, 2 * _LANES)
    if not c_ki.aligned(2 * _LANES):
        w_ki = jnp.pad(w_in_t[layer][o1:o3, :], ((0, _LANES - idx_heads), (0, 0)))
        c_ki = _Cols(w_ki.astype(_CDT), transposed=True)
    nsb = s // tm
    rope_specs = (pl.BlockSpec((tm, _HEAD_DIM), lambda j, i: (i % nsb, 0)),) * 2
    w_scale = idx_heads ** -0.5 * _HEAD_DIM ** -0.5
    vec = pl.BlockSpec((1, _HEAD_DIM), lambda j, i: (0, 0))
    kidx, wt = _proj(xc, c_ki, functools.partial(_ep_kidx, idx_heads=idx_heads, w_scale=w_scale),
                     (cos, sin, kidx_g[layer].reshape(1, _HEAD_DIM), kidx_b[layer].reshape(1, _HEAD_DIM)),
                     rope_specs + (vec, vec),
                     [jax.ShapeDtypeStruct((b, s, _HEAD_DIM), _CDT),
                      jax.ShapeDtypeStruct((b, idx_heads, s), jnp.float32)],
                     [pl.BlockSpec((1, tm, _HEAD_DIM), lambda j, i: (i // nsb, i % nsb, 0)),
                      pl.BlockSpec((1, idx_heads, tm), lambda j, i: (i // nsb, 0, i % nsb))],
                     tm, 2 * _LANES, "dsa_kidx")
    gate = _gate_call(xc, w_in(o3, branch), b, s, n_kv, "dsa_gate")
    q = _heads_call(cq, _Cols(w_uq, 0, branch, layer), b, s, rope=True,
                    scale=_HEAD_DIM ** -0.5 * _LOG2E, cos=cos, sin=sin, name="dsa_q")
    qi = _heads_call(cq, _Cols(w_uq, branch, None, layer), b, s, rope=True, scale=1.0, cos=cos, sin=sin,
                     name="dsa_qi")
    o = _dsa_core(q, qi, wt, kidx, k, vt, topk=topk, tk1=tk1, tk=tk, hchunk=min(8, idx_heads))
    return _outproj(o, gate, w_out[layer].astype(_CDT), min(512, s), min(1024, d), "dsa_out")


def _fox_layer(xc, w_in_t, forget_bias, w_out, layer, b, s):
    d = xc.shape[1]
    branch = w_out.shape[1]
    n_heads = forget_bias.shape[1]
    assert branch == n_heads * _HEAD_DIM and n_heads <= _LANES
    w_in = functools.partial(_Cols, w_in_t, layer=layer, transposed=True)
    w_f = jnp.pad(w_in_t[layer][4 * branch:, :], ((0, _LANES - n_heads), (0, 0)))
    fb = jnp.pad(forget_bias[layer], (0, _LANES - n_heads)).reshape(1, _LANES)
    tm = min(512, s)
    nsb = s // tm
    t = min(512, s)
    q = _heads_call(xc, w_in(0, branch), b, s, rope=False,
                    scale=_HEAD_DIM ** -0.5 * _LOG2E, cos=None, sin=None, name="fox_q")
    k = _heads_call(xc, w_in(branch, branch), b, s, rope=False, scale=1.0,
                    cos=None, sin=None, name="fox_k")
    vt = _vt_call(xc, w_in(2 * branch, branch), b, s, t, "fox_v")
    gate = _gate_call(xc, w_in(3 * branch, branch), b, s, 1, "fox_gate")
    lf = _proj(xc, _Cols(w_f, transposed=True), _ep_logf, (fb,), (pl.BlockSpec((1, _LANES), lambda j, i: (0, 0)),),
               jax.ShapeDtypeStruct((b, s, _LANES), jnp.float32),
               pl.BlockSpec((1, tm, _LANES), lambda j, i: (i // nsb, i % nsb, 0)),
               tm, _LANES, "fox_logf")
    caug = _cumsum_aug(lf, n_heads)
    o = _fox_core(q, k, caug, vt, t=t)
    return _outproj(o.reshape(b, 1, s, branch), gate, w_out[layer].astype(_CDT), tm, min(1024, d),
                    "fox_out")


def kernel(x, a_w_in, a_q_norm_g, a_w_uq, a_kidx_norm_g, a_kidx_norm_b, a_w_out,
           b_w_in, b_forget_bias, b_w_out, ln_g, ln_b):
    b, s, d = x.shape
    depth = ln_g.shape[0]
    alpha = (2 * depth) ** 0.25
    xf = x.reshape(b * s, d)
    xc = xf.astype(_CDT)
    a_w_in_t = jnp.swapaxes(a_w_in, 1, 2)
    b_w_in_t = jnp.swapaxes(b_w_in, 1, 2)
    tm_ln = min(256, s)
    for i in range(depth):
        j = i // 2
        if i % 2 == 0:
            h = _dsa_layer(xc, a_w_in_t, a_q_norm_g, a_w_uq, a_kidx_norm_g, a_kidx_norm_b, a_w_out, j, b, s)
        else:
            h = _fox_layer(xc, b_w_in_t, b_forget_bias, b_w_out, j, b, s)
        last = i == depth - 1
        outs = _resid_ln(h, xf, ln_g[i], ln_b[i], alpha, tm_ln, not last, "resid_ln_%d" % i)
        xf = outs[0]
        if not last:
            xc = outs[1]
    return xf.reshape(b, s, d)
```

```python
import functools
import math

import jax
import jax.numpy as jnp
from jax import lax
from jax.experimental import pallas as pl
from jax.experimental.pallas import tpu as pltpu

_CDT = jnp.bfloat16
_HEAD_DIM = 128
_GROUP = 8
_TOPK_MAX = 256
_ROPE_THETA = 10000.0
_LN_EPS = 1e-5
_RMS_EPS = 1e-6
_LANES = 128
_SUBLANES = 8
_MXU_DIM = 256
_VMEM_LIMIT = 56 * 1024 * 1024
_PROJ_TILE_BUDGET = 40 * 1024 * 1024
_VT_ROWS = _HEAD_DIM + 16
_INT_MIN = -2 ** 31
_MASKED = -1e30
_M_INIT = -1e29
_LOG2E = math.log2(math.e)
_ROW_CHUNK = 256

_NT = (((1,), (1,)), ((), ()))


def _params(*sem):
    return pltpu.CompilerParams(dimension_semantics=sem, vmem_limit_bytes=_VMEM_LIMIT)


def _rope_tile(x, cos, sin_signed):
    return x * cos + pltpu.roll(x, _HEAD_DIM // 2, axis=1) * sin_signed


class _Cols:
    def __init__(self, w, col0=0, n=None, layer=None, transposed=False):
        self.w, self.col0, self.layer, self.transposed = w, col0, layer, transposed
        self.n_total = w.shape[-2] if transposed else w.shape[-1]
        self.k = w.shape[-1] if transposed else w.shape[-2]
        self.n = self.n_total - col0 if n is None else n

    def block_aligned(self, tn):
        return self.col0 % tn == 0

    def aligned(self, tn):
        start_ok = self.col0 % _SUBLANES == 0 if self.transposed else self.block_aligned(tn)
        return start_ok and self.col0 + -(-self.n // tn) * tn <= self.n_total

    def materialize(self):
        w = self.w if self.layer is None else self.w[self.layer]
        if self.transposed:
            return _Cols(w[self.col0:self.col0 + self.n, :].astype(_CDT), transposed=True)
        return _Cols(w[:, self.col0:self.col0 + self.n].astype(_CDT))


def _proj_tiles(s, cols):
    itemsize = cols.w.dtype.itemsize
    act = jnp.dtype(_CDT).itemsize
    for tn in (1024, 512, 256, 128):
        if tn > cols.n:
            continue
        w_bytes = cols.k * tn * (2 * itemsize + (act if cols.w.dtype != _CDT else 0))
        for tm in (1024, 512, 256):
            if tm <= s and w_bytes + 2 * tm * cols.k * act <= _PROJ_TILE_BUDGET:
                return tm, tn
    raise ValueError("projection tiles do not fit VMEM")


def _proj(a, cols, epilogue, extras, extra_specs, out_shape, out_specs, tm, tn, name, row_chunk=None,
          emit_a=False):
    m, k = a.shape
    assert cols.aligned(tn) and k == cols.k
    convert = cols.w.dtype != _CDT
    n_extra = len(extras)
    w_block = (tn, k) if cols.transposed else (k, tn)

    def body(a_ref, w_ref, *rest):
        if convert:
            wc_ref = rest[-1]
            rest = rest[:-1]

            @pl.when(pl.program_id(1) == 0)
            def _():
                wc_ref[...] = w_ref[...].reshape(w_block).astype(_CDT)

            w = wc_ref[...]
        else:
            w = w_ref[...].reshape(w_block)
        if emit_a:
            xa_ref = rest[-1]
            rest = rest[:-1]
        rc = tm if row_chunk is None else row_chunk
        for r in range(tm // rc):
            rows = slice(r * rc, (r + 1) * rc)
            a_tile = a_ref[rows, :].astype(_CDT)
            if emit_a:
                xa_ref[rows, :] = a_tile
            if cols.transposed:
                res = lax.dot_general(a_tile, w, _NT, preferred_element_type=jnp.float32)
            else:
                res = jnp.dot(a_tile, w, preferred_element_type=jnp.float32)
            epilogue(res, rest[:n_extra], rest[n_extra:], rows)

    lead_index = () if cols.layer is None else (cols.layer,)
    if cols.block_aligned(tn):
        c0 = cols.col0 // tn
        lead = () if cols.layer is None else (None,)
        if cols.transposed:
            w_spec = pl.BlockSpec(lead + (tn, k), lambda j, i: lead_index + (c0 + j, 0))
        else:
            w_spec = pl.BlockSpec(lead + (k, tn), lambda j, i: lead_index + (0, c0 + j))
    else:
        assert cols.transposed
        lead = () if cols.layer is None else (pl.Element(1),)
        w_spec = pl.BlockSpec(lead + (pl.Element(tn), pl.Element(k)),
                              lambda j, i: lead_index + (pl.multiple_of(cols.col0 + j * tn, _SUBLANES), 0))
    return pl.pallas_call(
        body,
        grid=(-(-cols.n // tn), m // tm),
        in_specs=[pl.BlockSpec((tm, k), lambda j, i: (i, 0)), w_spec] + list(extra_specs),
        out_specs=out_specs,
        out_shape=out_shape,
        scratch_shapes=[pltpu.VMEM(w_block, _CDT)] if convert else [],
        compiler_params=_params("arbitrary", "arbitrary"),
        name=name,
    )(a, cols.w, *extras)


def _ep_rms(res, extras, outs, rows):
    (g_ref,) = extras
    (o_ref,) = outs
    y = res * lax.rsqrt(jnp.mean(res * res, axis=-1, keepdims=True) + _RMS_EPS)
    o_ref[rows, :] = (y * g_ref[...]).astype(o_ref.dtype)


def _ep_heads(res, extras, outs, rows, *, rope, scale):
    (o_ref,) = outs
    nh = res.shape[1] // _HEAD_DIM
    if rope:
        cos = extras[0][rows, :]
        sin = extras[1][rows, :]
    for h in range(nh):
        xh = res[:, h * _HEAD_DIM:(h + 1) * _HEAD_DIM]
        if rope:
            xh = _rope_tile(xh, cos, sin)
        if scale != 1.0:
            xh = xh * scale
        o_ref[0, h, rows, :] = xh.astype(o_ref.dtype)


def _ep_vt(res, extras, outs, rows, *, t):
    (vt_ref,) = outs
    assert rows.start % t == 0 and res.shape[0] % t == 0
    for h in range(res.shape[1] // _HEAD_DIM):
        vh = res[:, h * _HEAD_DIM:(h + 1) * _HEAD_DIM]
        for c in range(res.shape[0] // t):
            dst = rows.start // t + c
            vt_ref[0, h, dst, :_HEAD_DIM, :] = vh[c * t:(c + 1) * t, :].T.astype(vt_ref.dtype)
            vt_ref[0, h, dst, _HEAD_DIM:, :] = jnp.ones((_VT_ROWS - _HEAD_DIM, t), vt_ref.dtype)


def _ep_kidx(res, extras, outs, rows, *, idx_heads, w_scale):
    cos_ref, sin_ref, g_ref, b_ref = extras
    kidx_ref, wt_ref = outs
    x = res[:, :_HEAD_DIM]
    mu = jnp.mean(x, axis=-1, keepdims=True)
    xc = x - mu
    var = jnp.mean(xc * xc, axis=-1, keepdims=True)
    y = xc * lax.rsqrt(var + _LN_EPS) * g_ref[...] + b_ref[...]
    kidx_ref[0] = _rope_tile(y, cos_ref[...], sin_ref[...]).astype(kidx_ref.dtype)
    wt = (res[:, _HEAD_DIM:] * w_scale).T
    wt_ref[0] = wt[:idx_heads, :]


def _ep_silu(res, extras, outs, rows):
    (o_ref,) = outs
    o_ref[0, 0, rows, :] = (res * jax.nn.sigmoid(res)).astype(o_ref.dtype)


def _ep_logf(res, extras, outs, rows):
    (bias_ref,) = extras
    (o_ref,) = outs
    z = res + bias_ref[...]
    o_ref[0] = jnp.minimum(z, 0.0) - jnp.log(1.0 + jnp.exp(-jnp.abs(z)))


def _outproj(o, gate, w, tm, tn, name):
    b, g, s, c = o.shape
    d = w.shape[1]
    nsb = s // tm

    def body(o_ref, g_ref, w_ref, out_ref):
        acc = None
        for gi in range(g):
            a = (o_ref[0, gi].astype(jnp.float32) * g_ref[0, gi].astype(jnp.float32)).astype(_CDT)
            part = jnp.dot(a, w_ref[gi * c:(gi + 1) * c, :], preferred_element_type=jnp.float32)
            acc = part if acc is None else acc + part
        out_ref[...] = acc.astype(out_ref.dtype)

    return pl.pallas_call(
        body,
        grid=(d // tn, b * nsb),
        in_specs=[pl.BlockSpec((1, g, tm, c), lambda j, i: (i // nsb, 0, i % nsb, 0)),
                  pl.BlockSpec((1, g, tm, c), lambda j, i: (i // nsb, 0, i % nsb, 0)),
                  pl.BlockSpec((g * c, tn), lambda j, i: (0, j))],
        out_specs=pl.BlockSpec((tm, tn), lambda j, i: (i, j)),
        out_shape=jax.ShapeDtypeStruct((b * s, d), _CDT),
        compiler_params=_params("arbitrary", "arbitrary"),
        name=name,
    )(o, gate, w)


def _resid_ln(h, x, g, b, alpha, tm, with_cdt, name):
    m, d = x.shape

    def body(h_ref, x_ref, g_ref, b_ref, *outs):
        y = alpha * x_ref[...] + h_ref[...].astype(jnp.float32)
        mu = jnp.mean(y, axis=-1, keepdims=True)
        yc = y - mu
        var = jnp.mean(yc * yc, axis=-1, keepdims=True)
        z = yc * lax.rsqrt(var + _LN_EPS) * g_ref[...] + b_ref[...]
        outs[0][...] = z
        if with_cdt:
            outs[1][...] = z.astype(_CDT)

    row = pl.BlockSpec((tm, d), lambda i: (i, 0))
    vec = pl.BlockSpec((1, d), lambda i: (0, 0))
    out_shape = [jax.ShapeDtypeStruct((m, d), jnp.float32)]
    out_specs = [row]
    if with_cdt:
        out_shape.append(jax.ShapeDtypeStruct((m, d), _CDT))
        out_specs.append(row)
    return pl.pallas_call(
        body,
        grid=(m // tm,),
        in_specs=[row, row, vec, vec],
        out_specs=out_specs,
        out_shape=out_shape,
        compiler_params=_params("arbitrary"),
        name=name,
    )(h, x, g.reshape(1, d), b.reshape(1, d))


def _col_reduce(x, op):
    rows, n = x.shape
    fold = 8 * _SUBLANES
    if rows > fold and rows % fold == 0:
        x = op(x.reshape(rows // fold, fold, n), axis=0)
    return op(x, axis=0, keepdims=True)


def _softmax_step(sc, tile_max, m_ref, cols):
    m_old = m_ref[:, cols]
    m_new = jnp.maximum(m_old, tile_max)
    m_ref[:, cols] = m_new
    return jnp.exp2(sc - m_new).astype(_CDT), jnp.exp2(m_old - m_new)


def _pv_step(vta, p, alpha, acc_ref, cols):
    pv = jnp.dot(vta, p, preferred_element_type=jnp.float32)
    acc_ref[:, cols] = pv if alpha is None else alpha * acc_ref[:, cols] + pv


def _float_sort_key(x):
    bits = pltpu.bitcast(x, jnp.int32)
    return bits ^ ((bits >> 31) & jnp.int32(0x7FFFFFFF))


def _dsa_core(q, qi, wt, kidx, k, vt, *, topk, tk1, tk, hchunk):
    b, n_heads, s, _ = q.shape
    idx_heads = qi.shape[1]
    n_kv = k.shape[1]
    tq = _LANES
    gq = _GROUP * tq
    log_s = int(math.log2(s))
    assert 1 << log_s == s and s % tk == 0 and tk % tk1 == 0 and tk1 % tq == 0
    assert idx_heads % hchunk == 0 and gq % _MXU_DIM == 0

    def body(q_ref, qi_ref, wt_ref, kidx_ref, k_ref, vt_ref, o_ref,
             key_ref, mask_ref, sc_ref, p_ref, al_ref, mx_ref, m_ref, acc_ref):
        i = pl.program_id(1)
        n1 = (i * tq + tq + tk1 - 1) // tk1
        n3 = (i * tq + tq + tk - 1) // tk

        q_pos1 = i * tq + lax.broadcasted_iota(jnp.int32, (tk1, tq), 1)
        s_iota1 = lax.broadcasted_iota(jnp.int32, (tk1, tq), 0)

        def score_tile(kt, carry):
            ks = pl.multiple_of(kt * tk1, tk1)
            kx = kidx_ref[0, pl.ds(ks, tk1), :]
            acc = jnp.zeros((tk1, tq), jnp.float32)
            for hc in range(idx_heads // hchunk):
                qq = qi_ref[0, hc * hchunk:(hc + 1) * hchunk].reshape(hchunk * tq, _HEAD_DIM)
                d = lax.dot_general(kx, qq, _NT, preferred_element_type=jnp.float32)
                for hh in range(hchunk):
                    h = hc * hchunk + hh
                    acc = acc + jnp.maximum(d[:, hh * tq:(hh + 1) * tq], 0.0) * wt_ref[0, h:h + 1, :]
            key = _float_sort_key(acc + 0.0)
            key_ref[pl.ds(ks, tk1), :] = jnp.where(ks + s_iota1 <= q_pos1, key, _INT_MIN)
            return carry

        lax.fori_loop(0, n1, score_tile, 0)

        def pad_tile(kt, carry):
            key_ref[pl.ds(pl.multiple_of(kt * tk1, tk1), tk1), :] = jnp.full((tk1, tq), _INT_MIN, jnp.int32)
            return carry

        lax.fori_loop(n1, n3 * (tk // tk1), pad_tile, 0)

        q_pos = i * tq + lax.broadcasted_iota(jnp.int32, (tk, tq), 1)
        s_iota = lax.broadcasted_iota(jnp.int32, (tk, tq), 0)

        fold = 8 * _SUBLANES

        def select(n_tiles):
            def count(pred_fn):
                cnt = jnp.zeros((fold, tq), jnp.float32)
                for kt in range(n_tiles):
                    c = jnp.where(pred_fn(key_ref[kt * tk:(kt + 1) * tk, :], kt * tk), 1.0, 0.0)
                    cnt = cnt + jnp.sum(c.reshape(tk // fold, fold, tq), axis=0)
                return jnp.sum(cnt, axis=0, keepdims=True)

            def value_step(t, prefix):
                cand = prefix | jnp.left_shift(jnp.int32(1), 31 - t)
                cand_s = cand ^ _INT_MIN
                cnt = count(lambda key, ks: key >= cand_s)
                return jnp.where(cnt >= topk, cand, prefix)

            prefix = lax.fori_loop(0, 32, value_step, jnp.zeros((1, tq), jnp.int32))
            thr = prefix ^ _INT_MIN

            def tie_break():
                need = topk - count(lambda key, ks: key > thr)

                def index_step(t, j):
                    cand = j | jnp.left_shift(jnp.int32(1), log_s - 1 - t)
                    cnt = count(lambda key, ks: (key == thr) & (ks + s_iota < cand))
                    return jnp.where(cnt < need, cand, j)

                return lax.fori_loop(0, log_s, index_step, jnp.zeros((1, tq), jnp.int32))

            n_ge = count(lambda key, ks: key >= thr)
            j_last = lax.cond(jnp.max(n_ge) > topk, tie_break,
                              lambda: jnp.full((1, tq), s, jnp.int32))

            for kt in range(n_tiles):
                key = key_ref[kt * tk:(kt + 1) * tk, :]
                s_pos = kt * tk + s_iota
                sel = (key > thr) | ((key == thr) & (s_pos <= j_last))
                mask_ref[kt * tk:(kt + 1) * tk, :] = jnp.where(
                    sel & (s_pos <= q_pos), 0.0, _MASKED).astype(_CDT)

        for n_tiles in range(1, s // tk + 1):
            pl.when(n3 == n_tiles)(functools.partial(select, n_tiles))

        r = lax.broadcasted_iota(jnp.int32, (gq, tq), 0)
        c = lax.broadcasted_iota(jnp.int32, (gq, tq), 1)
        one_hot = jnp.where((r & (tq - 1)) == c, 1.0, 0.0).astype(_CDT)

        chunks = [slice(cc * _MXU_DIM, (cc + 1) * _MXU_DIM) for cc in range(gq // _MXU_DIM)]

        def kv_group(g, carry):
            qg = q_ref[0, pl.ds(g * _GROUP, _GROUP)].reshape(gq, _HEAD_DIM)
            qaug = jnp.concatenate([qg, one_hot], axis=1)
            m_ref[...] = jnp.full((1, gq), _M_INIT, jnp.float32)

            def logits(kt):
                kaug = jnp.concatenate([k_ref[0, g, pl.ds(kt * tk, tk), :],
                                        mask_ref[pl.ds(kt * tk, tk), :]], axis=1)
                for cols in chunks:
                    sc = lax.dot_general(kaug, qaug[cols], _NT, preferred_element_type=jnp.float32)
                    sc_ref[kt & 1, :, cols] = sc
                    mx_ref[kt & 1, :, cols] = _col_reduce(sc, jnp.max)

            def softmax(kt):
                for cols in chunks:
                    p_ref[kt & 1, :, cols], al_ref[kt & 1, :, cols] = _softmax_step(
                        sc_ref[kt & 1, :, cols], mx_ref[kt & 1, :, cols], m_ref, cols)

            def pv(kt):
                vta = vt_ref[0, g, kt]
                for cols in chunks:
                    _pv_step(vta, p_ref[kt & 1, :, cols], None if kt == 0 else al_ref[kt & 1, :, cols],
                             acc_ref, cols)

            for n_tiles in range(1, s // tk + 1):
                @pl.when(n3 == n_tiles)
                def _():
                    for step in range(n_tiles + 2):
                        if step < n_tiles:
                            logits(step)
                        if 1 <= step <= n_tiles:
                            softmax(step - 1)
                        if step >= 2:
                            pv(step - 2)

            acc = acc_ref[...]
            ot = acc[:_HEAD_DIM] * (1.0 / acc[_HEAD_DIM:_HEAD_DIM + 1])
            for hh in range(_GROUP):
                o_ref[0, g, :, hh * _HEAD_DIM:(hh + 1) * _HEAD_DIM] = (
                    ot[:, hh * tq:(hh + 1) * tq].T.astype(o_ref.dtype))
            return carry

        lax.fori_loop(0, n_kv, kv_group, 0)

    nq = s // tq
    return pl.pallas_call(
        body,
        grid=(b, nq),
        in_specs=[pl.BlockSpec((1, n_heads, tq, _HEAD_DIM), lambda bi, i: (bi, 0, i, 0)),
                  pl.BlockSpec((1, idx_heads, tq, _HEAD_DIM), lambda bi, i: (bi, 0, i, 0)),
                  pl.BlockSpec((1, idx_heads, tq), lambda bi, i: (bi, 0, i)),
                  pl.BlockSpec((1, s, _HEAD_DIM), lambda bi, i: (bi, 0, 0)),
                  pl.BlockSpec((1, n_kv, s, _HEAD_DIM), lambda bi, i: (bi, 0, 0, 0)),
                  pl.BlockSpec((1, n_kv, s // tk, _VT_ROWS, tk), lambda bi, i: (bi, 0, 0, 0, 0))],
        out_specs=pl.BlockSpec((1, n_kv, tq, _GROUP * _HEAD_DIM), lambda bi, i: (bi, 0, i, 0)),
        out_shape=jax.ShapeDtypeStruct((b, n_kv, s, _GROUP * _HEAD_DIM), _CDT),
        scratch_shapes=[pltpu.VMEM((s, tq), jnp.int32),
                        pltpu.VMEM((s, tq), _CDT),
                        pltpu.VMEM((2, tk, gq), jnp.float32),
                        pltpu.VMEM((2, tk, gq), _CDT),
                        pltpu.VMEM((2, 1, gq), jnp.float32),
                        pltpu.VMEM((2, 1, gq), jnp.float32),
                        pltpu.VMEM((1, gq), jnp.float32),
                        pltpu.VMEM((_VT_ROWS, gq), jnp.float32)],
        compiler_params=_params("arbitrary", "arbitrary"),
        name="dsa_core",
    )(q, qi, wt, kidx, k, vt)


def _cumsum_aug(lf, n_heads):
    b, s, w = lf.shape
    blk = _LANES
    rows = jnp.arange(3 * w)[:, None]
    cols = jnp.arange(n_heads * _LANES)[None, :]
    place = ((rows % w == cols // _LANES) & (rows // w == cols % _LANES)).astype(_CDT)

    def body(lf_ref, place_ref, out_ref, carry_ref):
        @pl.when(pl.program_id(1) == 0)
        def _():
            carry_ref[...] = jnp.zeros_like(carry_ref)
        r = lax.broadcasted_iota(jnp.int32, (blk, blk), 0)
        c = lax.broadcasted_iota(jnp.int32, (blk, blk), 1)
        tri = jnp.where(r >= c, 1.0, 0.0).astype(_CDT)
        x = lf_ref[0]
        hi = x.astype(_CDT)
        mid = (x - hi.astype(jnp.float32)).astype(_CDT)
        lo = (x - hi.astype(jnp.float32) - mid.astype(jnp.float32)).astype(_CDT)
        cs = jnp.dot(tri, lo, preferred_element_type=jnp.float32)
        for part in (mid, hi):
            cs = cs + jnp.dot(tri, part, preferred_element_type=jnp.float32)
        cs = cs + carry_ref[...]
        carry_ref[...] = cs[blk - 1:blk, :]
        c2 = cs * _LOG2E
        hi = c2.astype(_CDT)
        mid = (c2 - hi.astype(jnp.float32)).astype(_CDT)
        lo = (c2 - hi.astype(jnp.float32) - mid.astype(jnp.float32)).astype(_CDT)
        aug = jnp.dot(jnp.concatenate([hi, mid, lo], axis=1), place_ref[...],
                      preferred_element_type=jnp.float32)
        for h in range(n_heads):
            out_ref[0, h] = aug[:, h * _LANES:(h + 1) * _LANES].astype(out_ref.dtype)

    return pl.pallas_call(
        body,
        grid=(b, s // blk),
        in_specs=[pl.BlockSpec((1, blk, w), lambda bi, i: (bi, i, 0)),
                  pl.BlockSpec((3 * w, n_heads * _LANES), lambda bi, i: (0, 0))],
        out_specs=pl.BlockSpec((1, n_heads, blk, _LANES), lambda bi, i: (bi, 0, i, 0)),
        out_shape=jax.ShapeDtypeStruct((b, n_heads, s, _LANES), _CDT),
        scratch_shapes=[pltpu.VMEM((1, w), jnp.float32)],
        compiler_params=_params("arbitrary", "arbitrary"),
        name="fox_cumsum",
    )(lf, place)


def _fox_core(q, k, caug, vt, *, t):
    b, n_heads, s, _ = q.shape
    assert t % _MXU_DIM == 0 and s % t == 0

    chunks = [slice(cc * _MXU_DIM, (cc + 1) * _MXU_DIM) for cc in range(t // _MXU_DIM)]

    tiles = [(i, j) for i in range(s // t) for j in range(i + 1)]

    def visible(i, j, cols):
        return min(t, cols.stop) if i == j else t

    def body(q_ref, k_ref, c_ref, vt_ref, o_ref, kaug_ref, sc_ref, p_ref, al_ref, mx_ref, m_ref, acc_ref):
        kaug_ref[:, :_HEAD_DIM] = k_ref[0, 0]
        kaug_ref[:, _HEAD_DIM:] = c_ref[0, 0]
        lane = lax.broadcasted_iota(jnp.int32, (t, _HEAD_DIM), 1)
        minus_ones = jnp.where(lane < 3, -1.0, 0.0).astype(_CDT)

        def logits(n):
            i, j = tiles[n]
            qaug = jnp.concatenate([q_ref[0, 0, i * t:(i + 1) * t, :], minus_ones], axis=1)
            ka = kaug_ref[j * t:(j + 1) * t, :]
            for cols in chunks:
                kr = visible(i, j, cols)
                sc = lax.dot_general(ka[:kr], qaug[cols], _NT, preferred_element_type=jnp.float32)
                if i == j:
                    key_i = lax.broadcasted_iota(jnp.int32, (kr, _MXU_DIM), 0)
                    qry_i = cols.start + lax.broadcasted_iota(jnp.int32, (kr, _MXU_DIM), 1)
                    sc = jnp.where(key_i <= qry_i, sc, -jnp.inf)
                sc_ref[n & 1, :kr, cols] = sc
                mx_ref[n & 1, :, cols] = _col_reduce(sc, jnp.max)

        def softmax(n):
            i, j = tiles[n]
            if j == 0:
                m_ref[...] = jnp.full((1, t), _M_INIT, jnp.float32)
            for cols in chunks:
                kr = visible(i, j, cols)
                p_ref[n & 1, :kr, cols], al_ref[n & 1, :, cols] = _softmax_step(
                    sc_ref[n & 1, :kr, cols], mx_ref[n & 1, :, cols], m_ref, cols)

        def pv(n):
            i, j = tiles[n]
            vta = vt_ref[0, 0, j]
            for cols in chunks:
                kr = visible(i, j, cols)
                _pv_step(vta[:, :kr], p_ref[n & 1, :kr, cols],
                         None if j == 0 else al_ref[n & 1, :, cols], acc_ref, cols)
            if j == i:
                acc = acc_ref[...]
                ot = acc[:_HEAD_DIM] * (1.0 / acc[_HEAD_DIM:_HEAD_DIM + 1])
                for cq in range(t // _LANES):
                    o_ref[0, i * t + cq * _LANES:i * t + (cq + 1) * _LANES, :] = (
                        ot[:, cq * _LANES:(cq + 1) * _LANES].T.astype(o_ref.dtype))

        for step in range(len(tiles) + 2):
            if step >= 2:
                pv(step - 2)
            if 1 <= step <= len(tiles):
                softmax(step - 1)
            if step < len(tiles):
                logits(step)

    head = pl.BlockSpec((1, 1, s, _HEAD_DIM), lambda bi, h: (bi, h, 0, 0))
    return pl.pallas_call(
        body,
        grid=(b, n_heads),
        in_specs=[head, head, head,
                  pl.BlockSpec((1, 1, s // t, _VT_ROWS, t), lambda bi, h: (bi, h, 0, 0, 0))],
        out_specs=pl.BlockSpec((1, s, _HEAD_DIM), lambda bi, h: (bi, 0, h)),
        out_shape=jax.ShapeDtypeStruct((b, s, n_heads * _HEAD_DIM), _CDT),
        scratch_shapes=[pltpu.VMEM((s, 2 * _HEAD_DIM), _CDT),
                        pltpu.VMEM((2, t, t), jnp.float32),
                        pltpu.VMEM((2, t, t), _CDT),
                        pltpu.VMEM((2, 1, t), jnp.float32),
                        pltpu.VMEM((2, 1, t), jnp.float32),
                        pltpu.VMEM((1, t), jnp.float32),
                        pltpu.VMEM((_VT_ROWS, t), jnp.float32)],
        compiler_params=_params("arbitrary", "arbitrary"),
        name="fox_core",
    )(q, k, caug, vt)


def _rope_tables(s):
    half = _HEAD_DIM // 2
    inv = _ROPE_THETA ** (-jnp.arange(half, dtype=jnp.float32) / half)
    ang = jnp.arange(s, dtype=jnp.float32)[:, None] * inv[None, :]
    cos, sin = jnp.cos(ang), jnp.sin(ang)
    return jnp.concatenate([cos, cos], axis=1), jnp.concatenate([-sin, sin], axis=1)


def _placed(cols, s, tn_cap=None):
    tm, tn = _proj_tiles(s, cols)
    if tn_cap is not None:
        tn = min(tn, tn_cap)
    if not cols.aligned(tn):
        cols = cols.materialize()
        tm, tn = _proj_tiles(s, cols)
        if tn_cap is not None:
            tn = min(tn, tn_cap)
    return cols, tm, tn


def _heads_call(a, cols, b, s, *, rope, scale, cos, sin, name):
    cols, tm, tn = _placed(cols, s)
    nsb = s // tm
    extras, extra_specs = (), ()
    if rope:
        extras = (cos, sin)
        extra_specs = (pl.BlockSpec((tm, _HEAD_DIM), lambda j, i: (i % nsb, 0)),) * 2
    nh_t = tn // _HEAD_DIM
    return _proj(
        a, cols, functools.partial(_ep_heads, rope=rope, scale=scale), extras, extra_specs,
        jax.ShapeDtypeStruct((b, cols.n // _HEAD_DIM, s, _HEAD_DIM), _CDT),
        pl.BlockSpec((1, nh_t, tm, _HEAD_DIM), lambda j, i: (i // nsb, j, i % nsb, 0)),
        tm, tn, name, row_chunk=min(_ROW_CHUNK, tm))


def _vt_call(a, cols, b, s, t, name):
    cols, tm, tn = _placed(cols, s)
    nsb = s // tm
    nh_t = tn // _HEAD_DIM
    return _proj(
        a, cols, functools.partial(_ep_vt, t=t), (), (),
        jax.ShapeDtypeStruct((b, cols.n // _HEAD_DIM, s // t, _VT_ROWS, t), _CDT),
        pl.BlockSpec((1, nh_t, tm // t, _VT_ROWS, t), lambda j, i: (i // nsb, j, i % nsb, 0, 0)),
        tm, tn, name, row_chunk=t)


def _gate_call(a, cols, b, s, groups, name):
    c = cols.n // groups
    cols, tm, tn = _placed(cols, s, tn_cap=c)
    nsb = s // tm
    per = c // tn
    return _proj(
        a, cols, _ep_silu, (), (),
        jax.ShapeDtypeStruct((b, groups, s, c), _CDT),
        pl.BlockSpec((1, 1, tm, tn), lambda j, i: (i // nsb, j // per, i % nsb, j % per)),
        tm, tn, name, row_chunk=min(_ROW_CHUNK, tm))


def _dsa_layer(xc, w_in_t, q_norm_g, w_uq, kidx_g, kidx_b, w_out, layer, b, s):
    d = xc.shape[1]
    q_lora = w_uq.shape[1]
    branch = w_out.shape[1]
    idx_heads = (w_uq.shape[2] - branch) // _HEAD_DIM
    kv = (w_in_t.shape[1] - q_lora - _HEAD_DIM - idx_heads - branch) // 2
    w_in = functools.partial(_Cols, w_in_t, layer=layer, transposed=True)
    n_kv = kv // _HEAD_DIM
    n_heads = branch // _HEAD_DIM
    assert n_heads == n_kv * _GROUP and idx_heads <= _LANES
    topk = min(_TOPK_MAX, s // 4)
    o0 = q_lora
    o1 = o0 + 2 * kv
    o2 = o1 + _HEAD_DIM
    o3 = o2 + idx_heads
    cos, sin = _rope_tables(s)
    tk1 = min(256, s)
    tk = min(512, s)

    c_cq = w_in(0, q_lora).materialize()
    tm = min(512, s)
    g_spec = (pl.BlockSpec((1, q_lora), lambda j, i: (0, 0)),)
    if xc.dtype == _CDT:
        cq = _proj(xc, c_cq, _ep_rms, (q_norm_g[layer].reshape(1, q_lora),), g_spec,
                   jax.ShapeDtypeStruct((b * s, q_lora), _CDT),
                   pl.BlockSpec((tm, q_lora), lambda j, i: (i, 0)), tm, q_lora, "dsa_cq")
    else:
        tm_in = min(256, s)
        cq, xc = _proj(xc, c_cq, _ep_rms, (q_norm_g[layer].reshape(1, q_lora),), g_spec,
                       [jax.ShapeDtypeStruct((b * s, q_lora), _CDT),
                        jax.ShapeDtypeStruct((b * s, d), _CDT)],
                       [pl.BlockSpec((tm_in, q_lora), lambda j, i: (i, 0)),
                        pl.BlockSpec((tm_in, d), lambda j, i: (i, 0))],
                       tm_in, q_lora, "dsa_cq", emit_a=True)
    k = _heads_call(xc, w_in(o0, kv), b, s, rope=True, scale=1.0, cos=cos, sin=sin, name="dsa_k")
    vt = _vt_call(xc, w_in(o0 + kv, kv), b, s, tk, "dsa_v")

    c_ki = w_in(o1, 2 * _LANES)
    if not c_ki.aligned(2 * _LANES):
        w_ki = jnp.pad(w_in_t[layer][o1:o3, :], ((0, _LANES - idx_heads), (0, 0)))
        c_ki = _Cols(w_ki.astype(_CDT), transposed=True)
    nsb = s // tm
    rope_specs = (pl.BlockSpec((tm, _HEAD_DIM), lambda j, i: (i % nsb, 0)),) * 2
    w_scale = idx_heads ** -0.5 * _HEAD_DIM ** -0.5
    vec = pl.BlockSpec((1, _HEAD_DIM), lambda j, i: (0, 0))
    kidx, wt = _proj(xc, c_ki, functools.partial(_ep_kidx, idx_heads=idx_heads, w_scale=w_scale),
                     (cos, sin, kidx_g[layer].reshape(1, _HEAD_DIM), kidx_b[layer].reshape(1, _HEAD_DIM)),
                     rope_specs + (vec, vec),
                     [jax.ShapeDtypeStruct((b, s, _HEAD_DIM), _CDT),
                      jax.ShapeDtypeStruct((b, idx_heads, s), jnp.float32)],
                     [pl.BlockSpec((1, tm, _HEAD_DIM), lambda j, i: (i // nsb, i % nsb, 0)),
                      pl.BlockSpec((1, idx_heads, tm), lambda j, i: (i // nsb, 0, i % nsb))],
                     tm, 2 * _LANES, "dsa_kidx")
    gate = _gate_call(xc, w_in(o3, branch), b, s, n_kv, "dsa_gate")
    q = _heads_call(cq, _Cols(w_uq, 0, branch, layer), b, s, rope=True,
                    scale=_HEAD_DIM ** -0.5 * _LOG2E, cos=cos, sin=sin, name="dsa_q")
    qi = _heads_call(cq, _Cols(w_uq, branch, None, layer), b, s, rope=True, scale=1.0, cos=cos, sin=sin,
                     name="dsa_qi")
    o = _dsa_core(q, qi, wt, kidx, k, vt, topk=topk, tk1=tk1, tk=tk, hchunk=min(8, idx_heads))
    return _outproj(o, gate, w_out[layer].astype(_CDT), min(512, s), min(1024, d), "dsa_out")


def _fox_layer(xc, w_in_t, forget_bias, w_out, layer, b, s):
    d = xc.shape[1]
    branch = w_out.shape[1]
    n_heads = forget_bias.shape[1]
    assert branch == n_heads * _HEAD_DIM and n_heads <= _LANES
    w_in = functools.partial(_Cols, w_in_t, layer=layer, transposed=True)
    w_f = jnp.pad(w_in_t[layer][4 * branch:, :], ((0, _LANES - n_heads), (0, 0)))
    fb = jnp.pad(forget_bias[layer], (0, _LANES - n_heads)).reshape(1, _LANES)
    tm = min(512, s)
    nsb = s // tm
    t = min(512, s)
    q = _heads_call(xc, w_in(0, branch), b, s, rope=False,
                    scale=_HEAD_DIM ** -0.5 * _LOG2E, cos=None, sin=None, name="fox_q")
    k = _heads_call(xc, w_in(branch, branch), b, s, rope=False, scale=1.0,
                    cos=None, sin=None, name="fox_k")
    vt = _vt_call(xc, w_in(2 * branch, branch), b, s, t, "fox_v")
    gate = _gate_call(xc, w_in(3 * branch, branch), b, s, 1, "fox_gate")
    lf = _proj(xc, _Cols(w_f, transposed=True), _ep_logf, (fb,), (pl.BlockSpec((1, _LANES), lambda j, i: (0, 0)),),
               jax.ShapeDtypeStruct((b, s, _LANES), jnp.float32),
               pl.BlockSpec((1, tm, _LANES), lambda j, i: (i // nsb, i % nsb, 0)),
               tm, _LANES, "fox_logf")
    caug = _cumsum_aug(lf, n_heads)
    o = _fox_core(q, k, caug, vt, t=t)
    return _outproj(o.reshape(b, 1, s, branch), gate, w_out[layer].astype(_CDT), tm, min(1024, d),
                    "fox_out")


def kernel(x, a_w_in, a_q_norm_g, a_w_uq, a_kidx_norm_g, a_kidx_norm_b, a_w_out,
           b_w_in, b_forget_bias, b_w_out, ln_g, ln_b):
    b, s, d = x.shape
    depth = ln_g.shape[0]
    alpha = (2 * depth) ** 0.25
    xf = x.reshape(b * s, d)
    xc = xf
    a_w_in_t = jnp.swapaxes(a_w_in, 1, 2)
    b_w_in_t = jnp.swapaxes(b_w_in, 1, 2)
    tm_ln = min(256, s)
    for i in range(depth):
        j = i // 2
        if i % 2 == 0:
            h = _dsa_layer(xc, a_w_in_t, a_q_norm_g, a_w_uq, a_kidx_norm_g, a_kidx_norm_b, a_w_out, j, b, s)
        else:
            h = _fox_layer(xc.astype(_CDT), b_w_in_t, b_forget_bias, b_w_out, j, b, s)
        last = i == depth - 1
        outs = _resid_ln(h, xf, ln_g[i], ln_b[i], alpha, tm_ln, not last, "resid_ln_%d" % i)
        xf = outs[0]
        if not last:
            xc = outs[1]
    return xf.reshape(b, s, d)
```
